```python
import math
import jax
import jax.numpy as jnp
from jax import lax
import numpy as np

D_MODEL = 2048
BATCH = 4
SEQ = 2048
DEPTH = 1

N_META = 16
EPS = 1e-6

GLA_HEADS = 4
GLA_DK = D_MODEL // (2 * GLA_HEADS)
GLA_DV = D_MODEL // GLA_HEADS
GLA_QK = GLA_HEADS * GLA_DK
GLA_V = GLA_HEADS * GLA_DV
GLA_LOWRANK = 16
GLA_TAU = 16.0
GLA_CHUNK = 64
GLA_FRONT_PAD = GLA_CHUNK - N_META

DIFF_HEADS = 8
DIFF_DH = D_MODEL // (2 * DIFF_HEADS)
DIFF_DV = 2 * DIFF_DH
DIFF_QK = DIFF_HEADS * 2 * DIFF_DH
DIFF_V = DIFF_HEADS * DIFF_DV
Q_BLOCK = 128

ROPE_THETA = 500000.0
ROPE_DIMS = DIFF_DH // 4

PEER_HEADS = 8
PEER_NKEYS = 128
PEER_N = PEER_NKEYS * PEER_NKEYS
PEER_DKEY = 128
PEER_TOPK = 16
PEER_TOKEN_BLOCK = 64

IN_SIZES = (GLA_QK, GLA_QK, GLA_V, GLA_V, 2 * GLA_LOWRANK, DIFF_QK, DIFF_QK, DIFF_V, D_MODEL, D_MODEL)
IN_WIDTH = sum(IN_SIZES)

kernel_name = 'hybrid_gla_diffattn_peer_encoder'


def rmsnorm(x, g):
    xf = x.astype(jnp.float32)
    y = xf * lax.rsqrt(jnp.mean(xf * xf, axis=-1, keepdims=True) + EPS)
    return (y * g.astype(jnp.float32)).astype(x.dtype)


def split_points(sizes):
    pts, acc = [], 0
    for s in sizes[:-1]:
        acc += s
        pts.append(acc)
    return pts


def partial_rope(x, pos):
    half = ROPE_DIMS // 2
    inv = 1.0 / (ROPE_THETA ** (jnp.arange(half, dtype=jnp.float32) / half))
    ang = pos.astype(jnp.float32)[:, None] * inv[None, :]
    cos = jnp.cos(ang).astype(x.dtype)
    sin = jnp.sin(ang).astype(x.dtype)
    x1 = x[..., :half]
    x2 = x[..., half:ROPE_DIMS]
    return jnp.concatenate([x1 * cos - x2 * sin, x1 * sin + x2 * cos, x[..., ROPE_DIMS:]], axis=-1)


def gla_direction(q, k, v, logf, inclusive):
    C = q.shape[-2]
    b = jnp.cumsum(logf, axis=-2)
    b_last = b[..., -1:, :]
    q_in = q * jnp.exp(b)
    k_in = k * jnp.exp(-b)
    k_st = k * jnp.exp(b_last - b)
    mask = jnp.tril(jnp.ones((C, C), dtype=bool), 0 if inclusive else -1)
    a = jnp.where(mask, jnp.einsum('bhntd,bhnsd->bhnts', q_in, k_in), 0.0)
    o_intra = jnp.einsum('bhnts,bhnse->bhnte', a, v)
    d_state = jnp.einsum('bhnsd,bhnse->nbhde', k_st, v)
    decay = jnp.moveaxis(jnp.exp(b_last[..., 0, :]), 2, 0)

    def step(s, inp):
        dec, ds = inp
        return s * dec[..., None] + ds, s

    s0 = jnp.zeros(d_state.shape[1:], d_state.dtype)
    _, s_in = lax.scan(step, s0, (decay, d_state))
    o_inter = jnp.einsum('bhntd,nbhde->bhnte', q_in, s_in)
    return o_intra + o_inter


def gla_mixer(q, k, v, r, lr, w2_f, b_f, w2_b, b_b, g_norm):
    B, L, _ = q.shape
    C = GLA_CHUNK
    lr_f, lr_b = jnp.split(lr, 2, axis=-1)
    logf_f = jax.nn.log_sigmoid((lr_f @ w2_f + b_f).astype(jnp.float32)) / GLA_TAU
    logf_b = jax.nn.log_sigmoid((lr_b @ w2_b + b_b).astype(jnp.float32)) / GLA_TAU
    Lp = L + GLA_FRONT_PAD
    N = Lp // C

    def heads(t, dh):
        t = jnp.pad(t, ((0, 0), (GLA_FRONT_PAD, 0), (0, 0)))
        return t.reshape(B, N, C, GLA_HEADS, dh).transpose(0, 3, 1, 2, 4)

    qh = heads(q * (GLA_DK ** -0.5), GLA_DK)
    kh = heads(k, GLA_DK)
    vh = heads(v, GLA_DV)
    ff = heads(logf_f, GLA_DK)
    fb = heads(logf_b, GLA_DK)
    flip = lambda t: jnp.flip(t, axis=(2, 3))
    o = gla_direction(qh, kh, vh, ff, True) + flip(gla_direction(flip(qh), flip(kh), flip(vh), flip(fb), False))
    o = o.transpose(0, 2, 3, 1, 4).reshape(B, Lp, GLA_HEADS, GLA_DV)[:, GLA_FRONT_PAD:]
    o = rmsnorm(o, g_norm).reshape(B, L, GLA_V)
    return (o * jax.nn.silu(r.astype(jnp.float32))).astype(v.dtype)


def diff_attention(q, k, v, pos, lq1, lk1, lq2, lk2, g_norm, lam_init):
    B, L, _ = q.shape
    qh = partial_rope(q.reshape(B, L, DIFF_HEADS, 2, DIFF_DH).transpose(3, 0, 2, 1, 4), pos)
    kh = partial_rope(k.reshape(B, L, DIFF_HEADS, 2, DIFF_DH).transpose(3, 0, 2, 1, 4), pos)
    vh = v.reshape(B, L, DIFF_HEADS, DIFF_DV).transpose(0, 2, 1, 3)
    lam = (jnp.exp(jnp.sum(lq1.astype(jnp.float32) * lk1.astype(jnp.float32)))
           - jnp.exp(jnp.sum(lq2.astype(jnp.float32) * lk2.astype(jnp.float32))) + lam_init)
    n_blk = -(-L // Q_BLOCK)
    Lq = n_blk * Q_BLOCK
    qp = jnp.pad(qh, ((0, 0), (0, 0), (0, 0), (0, Lq - L), (0, 0)))
    qb = qp.reshape(2, B, DIFF_HEADS, n_blk, Q_BLOCK, DIFF_DH).transpose(3, 0, 1, 2, 4, 5)
    scale = DIFF_DH ** -0.5

    def block(qblk):
        s = jnp.einsum('ibhqd,ibhkd->ibhqk', qblk, kh, preferred_element_type=jnp.float32) * scale
        p = jax.nn.softmax(s, axis=-1)
        w = p[0] - lam * p[1]
        return jnp.einsum('bhqk,bhke->bhqe', w.astype(vh.dtype), vh)

    o = lax.map(block, qb)
    o = o.transpose(1, 0, 3, 2, 4).reshape(B, Lq, DIFF_HEADS, DIFF_DV)[:, :L]
    o = rmsnorm(o, g_norm) * (1.0 - lam_init)
    return o.reshape(B, L, DIFF_V).astype(v.dtype)


def peer(h, w_q, sub_keys, u_tab, v_tab):
    B, L, D = h.shape
    T = B * L
    TB = PEER_TOKEN_BLOCK
    n_blk = -(-T // TB)
    xp = jnp.pad(h.reshape(T, D), ((0, n_blk * TB - T), (0, 0))).reshape(n_blk, TB, D)
    K = PEER_TOPK

    def block(xb):
        qb = (xb @ w_q).reshape(TB, PEER_HEADS, 2, PEER_DKEY)
        s = jnp.einsum('thpc,hpnc->thpn', qb, sub_keys).astype(jnp.float32)
        sv, si = lax.top_k(s, K)
        cand = (sv[:, :, 0, :, None] + sv[:, :, 1, None, :]).reshape(TB, PEER_HEADS, K * K)
        cidx = (si[:, :, 0, :, None] * PEER_NKEYS + si[:, :, 1, None, :]).reshape(TB, PEER_HEADS, K * K)
        cv, ci = lax.top_k(cand, K)
        eidx = jnp.take_along_axis(cidx, ci, axis=-1)
        g = jax.nn.softmax(cv, axis=-1)
        u = u_tab[eidx]
        vv = v_tab[eidx]
        a = jax.nn.gelu(jnp.einsum('thkd,td->thk', u, xb).astype(jnp.float32), approximate=False)
        return jnp.einsum('thk,thkd->td', (g * a).astype(vv.dtype), vv)

    out = lax.map(block, xp).reshape(n_blk * TB, D)[:T]
    return out.reshape(B, L, D)


def setup_inputs(seed: int = 0) -> dict:
    key = jax.random.key(seed)
    ks = jax.random.split(key, 24)
    nrm = lambda k, shape, s: jax.random.normal(k, shape, jnp.float32) * s
    gain = lambda k, shape: 1.0 + 0.02 * jax.random.normal(k, shape, jnp.float32)
    P = DEPTH
    return {
        'x': nrm(ks[0], (BATCH, SEQ, D_MODEL), 1.0),
        'meta_tokens': nrm(ks[1], (N_META, D_MODEL), 1.0),
        'g_mix': gain(ks[2], (P, D_MODEL)),
        'w_in': nrm(ks[3], (P, D_MODEL, IN_WIDTH), D_MODEL ** -0.5),
        'gla_w2_fwd': nrm(ks[4], (P, GLA_LOWRANK, GLA_QK), GLA_LOWRANK ** -0.5),
        'gla_b_fwd': nrm(ks[5], (P, GLA_QK), 0.1),
        'gla_w2_bwd': nrm(ks[6], (P, GLA_LOWRANK, GLA_QK), GLA_LOWRANK ** -0.5),
        'gla_b_bwd': nrm(ks[7], (P, GLA_QK), 0.1),
        'gla_g_norm': gain(ks[8], (P, GLA_DV)),
        'diff_lq1': nrm(ks[9], (P, DIFF_DH), 0.1),
        'diff_lk1': nrm(ks[10], (P, DIFF_DH), 0.1),
        'diff_lq2': nrm(ks[11], (P, DIFF_DH), 0.1),
        'diff_lk2': nrm(ks[12], (P, DIFF_DH), 0.1),
        'diff_g_norm': gain(ks[13], (P, DIFF_DV)),
        'w_branch_gla': nrm(ks[14], (P, GLA_V, D_MODEL), GLA_V ** -0.5),
        'w_branch_diff': nrm(ks[15], (P, DIFF_V, D_MODEL), DIFF_V ** -0.5),
        'w_out': nrm(ks[16], (P, D_MODEL, D_MODEL), D_MODEL ** -0.5),
        'g_ffn': gain(ks[17], (P, D_MODEL)),
        'peer_w_q': nrm(ks[18], (P, D_MODEL, PEER_HEADS * 2 * PEER_DKEY), D_MODEL ** -0.5),
        'peer_sub_keys': nrm(ks[19], (P, PEER_HEADS, 2, PEER_NKEYS, PEER_DKEY), PEER_DKEY ** -0.5),
        'peer_u': nrm(ks[20], (P, PEER_N, D_MODEL), D_MODEL ** -0.5),
        'peer_v': nrm(ks[21], (P, PEER_N, D_MODEL), 0.25),
        'g_final': gain(ks[22], (D_MODEL,)),
    }


def reference(x, meta_tokens, g_mix, w_in, gla_w2_fwd, gla_b_fwd, gla_w2_bwd, gla_b_bwd, gla_g_norm,
              diff_lq1, diff_lk1, diff_lq2, diff_lk2, diff_g_norm, w_branch_gla, w_branch_diff, w_out,
              g_ffn, peer_w_q, peer_sub_keys, peer_u, peer_v, g_final):
    B = x.shape[0]
    meta = jnp.broadcast_to(meta_tokens[None].astype(x.dtype), (B, N_META, D_MODEL))
    hs = jnp.concatenate([meta, x], axis=1)
    L = hs.shape[1]
    pos = jnp.arange(L, dtype=jnp.int32)
    pts = split_points(IN_SIZES)
    for l in range(DEPTH):
        lam_init = 0.8 - 0.6 * math.exp(-0.3 * l)
        h = rmsnorm(hs, g_mix[l])
        gq, gk, gv, gr, glr, dq, dk, dv, za, zb = jnp.split(h @ w_in[l], pts, axis=-1)
        o_gla = gla_mixer(gq, gk, gv, gr, glr, gla_w2_fwd[l], gla_b_fwd[l], gla_w2_bwd[l], gla_b_bwd[l],
                          gla_g_norm[l])
        o_diff = diff_attention(dq, dk, dv, pos, diff_lq1[l], diff_lk1[l], diff_lq2[l], diff_lk2[l],
                                diff_g_norm[l], lam_init)
        y = jax.nn.sigmoid(za) * (o_gla @ w_branch_gla[l]) + jax.nn.sigmoid(zb) * (o_diff @ w_branch_diff[l])
        hs = hs + y @ w_out[l]
        hs = hs + peer(rmsnorm(hs, g_ffn[l]), peer_w_q[l], peer_sub_keys[l], peer_u[l], peer_v[l])
    out = rmsnorm(hs, g_final)
    return out[:, N_META:]
```

```python
import functools
import math

import jax
import jax.numpy as jnp
from jax import lax
from jax.experimental import pallas as pl
from jax.experimental.pallas import tpu as pltpu

f32 = jnp.float32
bf16 = jnp.bfloat16

D_MODEL = 2048
BATCH = 4
SEQ = 2048
N_META = 16
EPS = 1e-6
L_REAL = SEQ + N_META
LANE = 128
LP = -(-L_REAL // LANE) * LANE
TP = BATCH * LP

GLA_HEADS = 4
GLA_DK = 256
GLA_DV = 512
GLA_QK = GLA_HEADS * GLA_DK
GLA_V = GLA_HEADS * GLA_DV
GLA_LOWRANK = 16
GLA_TAU = 16.0
GLA_CHUNK = 64

DIFF_HEADS = 8
DIFF_DH = 128
DIFF_DV = 256
DIFF_QK = DIFF_HEADS * 2 * DIFF_DH
DIFF_V = DIFF_HEADS * DIFF_DV
ROPE_THETA = 500000.0
ROPE_DIMS = DIFF_DH // 4
ROPE_HALF = ROPE_DIMS // 2

PEER_HEADS = 8
PEER_NKEYS = 128
PEER_N = PEER_NKEYS * PEER_NKEYS
PEER_DKEY = 128
PEER_TOPK = 16

IN_SIZES = (GLA_QK, GLA_QK, GLA_V, GLA_V, 2 * GLA_LOWRANK, DIFF_QK, DIFF_QK, DIFF_V, D_MODEL, D_MODEL)

VMEM_LIMIT = 56 * 1024 * 1024
NEG_BIG = -1e30

ROW_BLOCK = 512
COL_BLOCK = 2048
DIFF_TQ = LP // 4
SEL_TB = 256
PEER_TB = 512
PEER_EB = 512


def _params(*sem):
    return pltpu.CompilerParams(dimension_semantics=sem, vmem_limit_bytes=VMEM_LIMIT)


def _rms(x, g):
    return x * lax.rsqrt(jnp.mean(x * x, axis=-1, keepdims=True) + EPS) * g


def _dot(a, b):
    return jnp.dot(a, b, preferred_element_type=f32)


def _dot_nt(a, b):
    return lax.dot_general(a, b, (((1,), (1,)), ((), ())), preferred_element_type=f32)


def _dot_tn(a, b, precision=None):
    return lax.dot_general(a, b, (((0,), (0,)), ((), ())), preferred_element_type=f32, precision=precision)


def _norm_kernel(x_ref, g_ref, o_ref):
    o_ref[...] = _rms(x_ref[...], g_ref[...]).astype(o_ref.dtype)


def _norm_rows(x, g):
    return pl.pallas_call(
        _norm_kernel,
        out_shape=jax.ShapeDtypeStruct((TP, D_MODEL), bf16),
        grid=(TP // ROW_BLOCK,),
        in_specs=[pl.BlockSpec((ROW_BLOCK, D_MODEL), lambda i: (i, 0)),
                  pl.BlockSpec((1, D_MODEL), lambda i: (0, 0))],
        out_specs=pl.BlockSpec((ROW_BLOCK, D_MODEL), lambda i: (i, 0)),
        compiler_params=_params("parallel"),
        name="mix_norm",
    )(x, g)


def _mm_kernel(a_ref, w_ref, o_ref):
    o_ref[...] = _dot(a_ref[...], w_ref[...]).astype(o_ref.dtype)


def _project(h, w, out_dtype, name):
    n = w.shape[1]
    return pl.pallas_call(
        _mm_kernel,
        out_shape=jax.ShapeDtypeStruct((TP, n), out_dtype),
        grid=(n // COL_BLOCK, TP // ROW_BLOCK),
        in_specs=[pl.BlockSpec((ROW_BLOCK, D_MODEL), lambda j, i: (i, 0)),
                  pl.BlockSpec((D_MODEL, COL_BLOCK), lambda j, i: (0, j))],
        out_specs=pl.BlockSpec((ROW_BLOCK, COL_BLOCK), lambda j, i: (i, j)),
        compiler_params=_params("parallel", "parallel"),
        name=name,
    )(h, w)


def _decay_kernel(h_ref, wlr_ref, w2f_ref, w2b_ref, bf_ref, bb_ref, of_ref, ob_ref):
    lr = _dot(h_ref[...], wlr_ref[...]).astype(bf16)
    zf = _dot(lr, w2f_ref[...]) + bf_ref[...]
    zb = _dot(lr, w2b_ref[...]) + bb_ref[...]
    of_ref[...] = jax.nn.log_sigmoid(zf) * (1.0 / GLA_TAU)
    ob_ref[...] = jax.nn.log_sigmoid(zb) * (1.0 / GLA_TAU)


def _decay(h, wlr, w2f, w2b, b_f, b_b):
    row = lambda i: (i, 0)
    fixed = lambda i: (0, 0)
    return pl.pallas_call(
        _decay_kernel,
        out_shape=(jax.ShapeDtypeStruct((TP, GLA_QK), f32),) * 2,
        grid=(TP // ROW_BLOCK,),
        in_specs=[pl.BlockSpec((ROW_BLOCK, D_MODEL), row),
                  pl.BlockSpec((D_MODEL, LANE), fixed),
                  pl.BlockSpec((LANE, GLA_QK), fixed),
                  pl.BlockSpec((LANE, GLA_QK), fixed),
                  pl.BlockSpec((1, GLA_QK), fixed),
                  pl.BlockSpec((1, GLA_QK), fixed)],
        out_specs=(pl.BlockSpec((ROW_BLOCK, GLA_QK), row),) * 2,
        compiler_params=_params("parallel"),
        name="gla_decay",
    )(h, wlr, w2f, w2b, b_f, b_b)


def _gla_kernel(q_ref, k_ref, v_ref, ff_ref, fb_ref, r_ref, g_ref, o_ref, acc_ref, s_ref):
    C = GLA_CHUNK
    n_chunks = LP // C
    row = lax.broadcasted_iota(jnp.int32, (C, C), 0)
    col = lax.broadcasted_iota(jnp.int32, (C, C), 1)
    ones = jnp.ones((C, LANE), f32)
    hi = lax.Precision.HIGHEST

    def chunk(n, f_ref, cum_mask, keep_mask, total_row, accumulate):
        r0 = pl.multiple_of(n * C, C)
        rows = pl.ds(r0, C)
        lf = f_ref[rows, :]
        cum = jnp.dot(cum_mask.astype(f32), lf, precision=hi, preferred_element_type=f32)
        tot = cum[total_row:total_row + 1, :]
        tot_col = _dot_tn(lf, ones, precision=hi)
        q = q_ref[rows, :].astype(f32) * (GLA_DK ** -0.5)
        k = k_ref[rows, :].astype(f32)
        v = v_ref[rows, :]
        q_in = (q * jnp.exp(cum)).astype(bf16)
        k_in = (k * jnp.exp(-cum)).astype(bf16)
        k_st = (k * jnp.exp(tot - cum)).astype(bf16)
        a = jnp.where(keep_mask, _dot_nt(q_in, k_in), 0.0)
        o = _dot(a.astype(bf16), v) + _dot(q_in, s_ref[...].astype(bf16))
        if accumulate:
            acc_ref[rows, :] += o
        else:
            acc_ref[rows, :] = o
        dec = jnp.exp(tot_col)
        dec = jnp.concatenate([dec] * (GLA_DV // LANE), axis=1)
        s_ref[...] = s_ref[...] * dec + _dot_tn(k_st, v)

    s_ref[...] = jnp.zeros_like(s_ref)

    def fwd(n, carry):
        chunk(n, ff_ref, col <= row, col <= row, C - 1, False)
        return carry

    lax.fori_loop(0, n_chunks, fwd, 0)

    s_ref[...] = jnp.zeros_like(s_ref)

    def bwd(n, carry):
        chunk(n_chunks - 1 - n, fb_ref, col >= row, col > row, 0, True)
        return carry

    lax.fori_loop(0, n_chunks, bwd, 0)

    def fin(n, carry):
        rows = pl.ds(pl.multiple_of(n * LANE, LANE), LANE)
        o = _rms(acc_ref[rows, :], g_ref[...])
        o_ref[rows, :] = (o * jax.nn.silu(r_ref[rows, :])).astype(o_ref.dtype)
        return carry

    lax.fori_loop(0, LP // LANE, fin, 0)


def _gla(proj_a, proj_b, logf_f, logf_b, g_norm):
    kq = GLA_QK // GLA_DK
    kv = (2 * GLA_QK) // GLA_DV
    return pl.pallas_call(
        _gla_kernel,
        out_shape=jax.ShapeDtypeStruct((TP, GLA_V), bf16),
        grid=(BATCH, GLA_HEADS),
        in_specs=[pl.BlockSpec((LP, GLA_DK), lambda b, h: (b, h)),
                  pl.BlockSpec((LP, GLA_DK), lambda b, h: (b, kq + h)),
                  pl.BlockSpec((LP, GLA_DV), lambda b, h: (b, kv + h)),
                  pl.BlockSpec((LP, GLA_DK), lambda b, h: (b, h)),
                  pl.BlockSpec((LP, GLA_DK), lambda b, h: (b, h)),
                  pl.BlockSpec((LP, GLA_DV), lambda b, h: (b, h)),
                  pl.BlockSpec((1, GLA_DV), lambda b, h: (0, 0))],
        out_specs=pl.BlockSpec((LP, GLA_DV), lambda b, h: (b, h)),
        scratch_shapes=[pltpu.VMEM((LP, GLA_DV), f32), pltpu.VMEM((GLA_DK, GLA_DV), f32)],
        compiler_params=_params("parallel", "parallel"),
        name="gla_mixer",
    )(proj_a, proj_a, proj_a, logf_f, logf_b, proj_b, g_norm)


def _rope(x, tab):
    return (x * tab[0]
            + pltpu.roll(x, LANE - ROPE_HALF, axis=1) * tab[1]
            + pltpu.roll(x, ROPE_HALF, axis=1) * tab[2])


def _diff_kernel(lam_init, q_ref, k_ref, v_ref, tq_ref, tk_ref, lq1_ref, lk1_ref, lq2_ref, lk2_ref,
                 g_ref, o_ref, kr_ref):
    @pl.when(pl.program_id(2) == 0)
    def _():
        tk = tk_ref[...]
        for m in range(2):
            kx = k_ref[:, m * DIFF_DH:(m + 1) * DIFF_DH].astype(f32)
            kr_ref[m] = _rope(kx, tk).astype(bf16)

    lam = (jnp.exp(jnp.sum(lq1_ref[...] * lk1_ref[...], axis=1, keepdims=True))
           - jnp.exp(jnp.sum(lq2_ref[...] * lk2_ref[...], axis=1, keepdims=True)) + lam_init)
    tq = tq_ref[...]
    key_ok = lax.broadcasted_iota(jnp.int32, (DIFF_TQ, LP), 1) < L_REAL
    probs = []
    for m in range(2):
        qx = q_ref[:, m * DIFF_DH:(m + 1) * DIFF_DH].astype(f32)
        qr = _rope(qx, tq).astype(bf16)
        s = _dot_nt(qr, kr_ref[m]) * (DIFF_DH ** -0.5)
        s = jnp.where(key_ok, s, NEG_BIG)
        e = jnp.exp(s - jnp.max(s, axis=1, keepdims=True))
        probs.append(e * (1.0 / jnp.sum(e, axis=1, keepdims=True)))
    w = (probs[0] - lam * probs[1]).astype(bf16)
    o = _dot(w, v_ref[...])
    o_ref[...] = (_rms(o, g_ref[...]) * (1.0 - lam_init)).astype(o_ref.dtype)


def _diff_attention(proj_a, rope_tab, lq1, lk1, lq2, lk2, g_norm, lam_init):
    nq = LP // DIFF_TQ
    cq = (2 * GLA_QK + GLA_V) // DIFF_DV
    ck = cq + DIFF_HEADS
    cv = ck + DIFF_HEADS
    vec = pl.BlockSpec((1, DIFF_DH), lambda b, h, i: (0, 0))
    return pl.pallas_call(
        functools.partial(_diff_kernel, lam_init),
        out_shape=jax.ShapeDtypeStruct((TP, DIFF_V), bf16),
        grid=(BATCH, DIFF_HEADS, nq),
        in_specs=[pl.BlockSpec((DIFF_TQ, DIFF_DV), lambda b, h, i: (b * nq + i, cq + h)),
                  pl.BlockSpec((LP, DIFF_DV), lambda b, h, i: (b, ck + h)),
                  pl.BlockSpec((LP, DIFF_DV), lambda b, h, i: (b, cv + h)),
                  pl.BlockSpec((3, DIFF_TQ, DIFF_DH), lambda b, h, i: (0, i, 0)),
                  pl.BlockSpec((3, LP, DIFF_DH), lambda b, h, i: (0, 0, 0)),
                  vec, vec, vec, vec,
                  pl.BlockSpec((1, DIFF_DV), lambda b, h, i: (0, 0))],
        out_specs=pl.BlockSpec((DIFF_TQ, DIFF_DV), lambda b, h, i: (b * nq + i, h)),
        scratch_shapes=[pltpu.VMEM((2, LP, DIFF_DH), bf16)],
        compiler_params=_params("parallel", "parallel", "arbitrary"),
        name="diff_attention",
    )(proj_a, proj_a, proj_a, rope_tab, rope_tab, lq1, lk1, lq2, lk2, g_norm)


def _merge_kernel(og_ref, od_ref, za_ref, zb_ref, hs_ref, wa_ref, wb_ref, wo_ref, g_ref, hs2_ref, hn_ref):
    y = (jax.nn.sigmoid(za_ref[...]) * _dot(og_ref[...], wa_ref[...])
         + jax.nn.sigmoid(zb_ref[...]) * _dot(od_ref[...], wb_ref[...]))
    hs2 = hs_ref[...] + _dot(y.astype(bf16), wo_ref[...])
    hs2_ref[...] = hs2
    hn_ref[...] = _rms(hs2, g_ref[...]).astype(hn_ref.dtype)


def _merge(o_gla, o_diff, proj_b, hs, wa, wb, wo, g_ffn):
    tm = 256
    row = lambda i: (i, 0)
    fixed = lambda i: (0, 0)
    wspec = pl.BlockSpec((D_MODEL, D_MODEL), fixed, pipeline_mode=pl.Buffered(1))
    return pl.pallas_call(
        _merge_kernel,
        out_shape=(jax.ShapeDtypeStruct((TP, D_MODEL), f32), jax.ShapeDtypeStruct((TP, D_MODEL), bf16)),
        grid=(TP // tm,),
        in_specs=[pl.BlockSpec((tm, GLA_V), row),
                  pl.BlockSpec((tm, DIFF_V), row),
                  pl.BlockSpec((tm, D_MODEL), lambda i: (i, 1)),
                  pl.BlockSpec((tm, D_MODEL), lambda i: (i, 2)),
                  pl.BlockSpec((tm, D_MODEL), row),
                  wspec, wspec, wspec,
                  pl.BlockSpec((1, D_MODEL), fixed)],
        out_specs=(pl.BlockSpec((tm, D_MODEL), row), pl.BlockSpec((tm, D_MODEL), row)),
        compiler_params=_params("parallel"),
        name="branch_merge",
    )(o_gla, o_diff, proj_b, proj_b, hs, wa, wb, wo, g_ffn)


def _top16(s, iota):
    rank = jnp.full(s.shape, float(PEER_TOPK), f32)
    vals = []
    for r in range(PEER_TOPK):
        m = jnp.max(s, axis=0, keepdims=True)
        first = jnp.min(jnp.where(s == m, iota, float(PEER_NKEYS)), axis=0, keepdims=True)
        hit = iota == first
        rank = jnp.where(hit, float(r), rank)
        s = jnp.where(hit, -jnp.inf, s)
        vals.append(m)
    return vals, rank


def _peer_select_kernel(hn_ref, wq_ref, keys_ref, rank1_ref, cnt0_ref, e0_ref, e1_ref, q_sc):
    tb = SEL_TB
    K = PEER_TOPK
    q_sc[...] = _dot(hn_ref[...], wq_ref[...]).astype(bf16)
    iota = lax.broadcasted_iota(jnp.int32, (PEER_NKEYS, tb), 0).astype(f32)
    i16 = lax.broadcasted_iota(jnp.int32, (K, tb), 0).astype(f32)
    i8 = lax.broadcasted_iota(jnp.int32, (8, tb), 0).astype(f32)
    ids = jnp.concatenate([i16] + [a * float(K) + i8 for a in range(1, 8)] + [(i8 + 8.0) * float(K)], axis=0)
    n_cand = ids.shape[0]

    def head(h, carry):
        c0 = pl.multiple_of(h * (2 * PEER_DKEY), 2 * PEER_DKEY)
        s0 = _dot_nt(keys_ref[h, 0], q_sc[:, pl.ds(c0, PEER_DKEY)])
        s1 = _dot_nt(keys_ref[h, 1], q_sc[:, pl.ds(c0 + PEER_DKEY, PEER_DKEY)])
        v0, rank0 = _top16(s0, iota)
        v1, rank1 = _top16(s1, iota)
        sa = jnp.concatenate(v0, axis=0)
        sb = jnp.concatenate(v1, axis=0)
        cand = jnp.concatenate([sa[0:1] + sb]
                               + [sa[a:a + 1] + sb[0:8] for a in range(1, 8)]
                               + [sa[8:16] + sb[0:1]], axis=0)
        top = cand[0:1]
        taken = jnp.zeros((n_cand, tb), f32)
        z = jnp.zeros((1, tb), f32)
        for _ in range(K):
            m = jnp.max(cand, axis=0, keepdims=True)
            first = jnp.min(jnp.where(cand == m, ids, 1e9), axis=0, keepdims=True)
            hit = ids == first
            taken = jnp.where(hit, 1.0, taken)
            cand = jnp.where(hit, -jnp.inf, cand)
            z = z + jnp.exp(m - top)
        cnt = ([jnp.sum(taken[0:16], axis=0, keepdims=True)]
               + [jnp.sum(taken[8 + 8 * a:16 + 8 * a], axis=0, keepdims=True) for a in range(1, 8)]
               + [taken[72 + a:73 + a] for a in range(8)])
        cnt0 = jnp.zeros((PEER_NKEYS, tb), f32)
        for a in range(K):
            cnt0 = jnp.where(rank0 == float(a), cnt[a], cnt0)
        rank1_ref[h] = rank1
        cnt0_ref[h] = cnt0
        e0_ref[h] = jnp.exp(s0 - v0[0]) * (1.0 / z)
        e1_ref[h] = jnp.exp(s1 - v1[0])
        return carry

    lax.fori_loop(0, PEER_HEADS, head, 0)


def _peer_select(hn, wq, keys):
    tb = SEL_TB
    sel = jax.ShapeDtypeStruct((PEER_HEADS, PEER_NKEYS, TP), f32)
    sel_spec = pl.BlockSpec((PEER_HEADS, PEER_NKEYS, tb), lambda i: (0, 0, i))
    return pl.pallas_call(
        _peer_select_kernel,
        out_shape=(sel,) * 4,
        grid=(TP // tb,),
        in_specs=[pl.BlockSpec((tb, D_MODEL), lambda i: (i, 0)),
                  pl.BlockSpec((D_MODEL, PEER_HEADS * 2 * PEER_DKEY), lambda i: (0, 0)),
                  pl.BlockSpec((PEER_HEADS, 2, PEER_NKEYS, PEER_DKEY), lambda i: (0, 0, 0, 0))],
        out_specs=(sel_spec,) * 4,
        scratch_shapes=[pltpu.VMEM((tb, PEER_HEADS * 2 * PEER_DKEY), bf16)],
        compiler_params=_params("parallel"),
        name="peer_select",
    )(hn, wq, keys)


def _peer_kernel(hn_ref, u_ref, vt_ref, rank1_ref, cnt0_ref, e0_ref, e1_ref, hs_ref, g_ref, o_ref, acc_ref, p_ref):
    j = pl.program_id(1)
    groups = PEER_EB // PEER_NKEYS

    @pl.when(j == 0)
    def _():
        acc_ref[...] = jnp.zeros_like(acc_ref)

    a = _dot_nt(u_ref[...], hn_ref[...])
    act = 0.5 * a * (1.0 + lax.erf(a * (2.0 ** -0.5)))
    for gi in range(groups):
        i = j * groups + gi
        w = jnp.zeros((PEER_NKEYS, PEER_TB), f32)
        for h in range(PEER_HEADS):
            cnt = cnt0_ref[h, pl.ds(i, 1), :]
            e0 = e0_ref[h, pl.ds(i, 1), :]
            w = w + jnp.where(rank1_ref[h] < cnt, e1_ref[h], 0.0) * e0
        rows = slice(gi * PEER_NKEYS, (gi + 1) * PEER_NKEYS)
        p_ref[rows, :] = (w * act[rows, :]).astype(bf16)
    acc_ref[...] += _dot(vt_ref[...], p_ref[...])

    @pl.when(j == pl.num_programs(1) - 1)
    def _():
        hs3 = hs_ref[...] + acc_ref[...].T
        o_ref[...] = _rms(hs3, g_ref[...])


def _peer(hn, u, vt, rank1, cnt0, e0, e1, hs2, g_final):
    tb, eb = PEER_TB, PEER_EB
    sel_spec = pl.BlockSpec((PEER_HEADS, PEER_NKEYS, tb), lambda i, j: (0, 0, i))
    return pl.pallas_call(
        _peer_kernel,
        out_shape=jax.ShapeDtypeStruct((TP, D_MODEL), f32),
        grid=(TP // tb, PEER_N // eb),
        in_specs=[pl.BlockSpec((tb, D_MODEL), lambda i, j: (i, 0)),
                  pl.BlockSpec((eb, D_MODEL), lambda i, j: (j, 0)),
                  pl.BlockSpec((D_MODEL, eb), lambda i, j: (0, j)),
                  sel_spec, sel_spec, sel_spec, sel_spec,
                  pl.BlockSpec((tb, D_MODEL), lambda i, j: (i, 0)),
                  pl.BlockSpec((1, D_MODEL), lambda i, j: (0, 0))],
        out_specs=pl.BlockSpec((tb, D_MODEL), lambda i, j: (i, 0)),
        scratch_shapes=[pltpu.VMEM((D_MODEL, tb), f32), pltpu.VMEM((eb, tb), bf16)],
        compiler_params=_params("parallel", "arbitrary"),
        name="peer_experts",
    )(hn, u, vt, rank1, cnt0, e0, e1, hs2, g_final)


def _rope_tables():
    inv = 1.0 / (ROPE_THETA ** (jnp.arange(ROPE_HALF, dtype=f32) / ROPE_HALF))
    ang = jnp.arange(LP, dtype=jnp.int32).astype(f32)[:, None] * inv[None, :]
    cos, sin = jnp.cos(ang), jnp.sin(ang)
    rest = DIFF_DH - ROPE_DIMS
    zero, zrest = jnp.zeros_like(sin), jnp.zeros((LP, rest), f32)
    return jnp.stack([
        jnp.concatenate([cos, cos, jnp.ones((LP, rest), f32)], axis=1),
        jnp.concatenate([-sin, zero, zrest], axis=1),
        jnp.concatenate([zero, sin, zrest], axis=1)])


def _split_cols(w):
    parts, start = [], 0
    for n in IN_SIZES:
        parts.append(w[:, start:start + n])
        start += n
    return parts


def kernel(x, meta_tokens, g_mix, w_in, gla_w2_fwd, gla_b_fwd, gla_w2_bwd, gla_b_bwd, gla_g_norm, diff_lq1, diff_lk1, diff_lq2, diff_lk2, diff_g_norm, w_branch_gla, w_branch_diff, w_out, g_ffn, peer_w_q, peer_sub_keys, peer_u, peer_v, g_final):
    assert w_in.shape[0] == 1, "single-layer block only"
    l = 0
    lam_init = 0.8 - 0.6 * math.exp(-0.3 * l)
    meta = jnp.broadcast_to(meta_tokens[None].astype(x.dtype), (BATCH, N_META, D_MODEL))
    pad = jnp.zeros((BATCH, LP - L_REAL, D_MODEL), x.dtype)
    hs = jnp.concatenate([meta, x, pad], axis=1).reshape(TP, D_MODEL)
    rope_tab = _rope_tables()

    gq, gk, gv, gr, glr, dq, dk, dv, za, zb = _split_cols(w_in[l])
    w_a = jnp.concatenate([gq, gk, gv, dq, dk, dv], axis=1).astype(bf16)
    w_b = jnp.concatenate([gr, za, zb], axis=1).astype(bf16)
    w_lr = jnp.pad(glr, ((0, 0), (0, LANE - 2 * GLA_LOWRANK))).astype(bf16)
    w2f = jnp.pad(gla_w2_fwd[l], ((0, LANE - GLA_LOWRANK), (0, 0))).astype(bf16)
    w2b = jnp.pad(gla_w2_bwd[l], ((GLA_LOWRANK, LANE - 2 * GLA_LOWRANK), (0, 0))).astype(bf16)

    h = _norm_rows(hs, g_mix[l][None])
    proj_a = _project(h, w_a, bf16, "in_proj_qkv")
    proj_b = _project(h, w_b, f32, "in_proj_gates")
    logf_f, logf_b = _decay(h, w_lr, w2f, w2b, gla_b_fwd[l][None], gla_b_bwd[l][None])
    o_gla = _gla(proj_a, proj_b, logf_f, logf_b, gla_g_norm[l][None])
    o_diff = _diff_attention(proj_a, rope_tab, diff_lq1[l][None], diff_lk1[l][None], diff_lq2[l][None],
                             diff_lk2[l][None], diff_g_norm[l][None], lam_init)
    hs2, hn = _merge(o_gla, o_diff, proj_b, hs, w_branch_gla[l].astype(bf16), w_branch_diff[l].astype(bf16),
                     w_out[l].astype(bf16), g_ffn[l][None])
    rank1, cnt0, e0, e1 = _peer_select(hn, peer_w_q[l].astype(bf16), peer_sub_keys[l].astype(bf16))
    out = _peer(hn, peer_u[l].astype(bf16), peer_v[l].T.astype(bf16), rank1, cnt0, e0, e1, hs2, g_final[None])
    return out.reshape(BATCH, LP, D_MODEL)[:, N_META:L_REAL]
```

```python
import functools
import math

import jax
import jax.numpy as jnp
from jax import lax
from jax.experimental import pallas as pl
from jax.experimental.pallas import tpu as pltpu

f32 = jnp.float32
bf16 = jnp.bfloat16

D_MODEL = 2048
BATCH = 4
SEQ = 2048
N_META = 16
EPS = 1e-6
L_REAL = SEQ + N_META
LANE = 128
LP = -(-L_REAL // LANE) * LANE
TP = BATCH * LP

GLA_HEADS = 4
GLA_DK = 256
GLA_DV = 512
GLA_QK = GLA_HEADS * GLA_DK
GLA_V = GLA_HEADS * GLA_DV
GLA_LOWRANK = 16
GLA_TAU = 16.0
GLA_CHUNK = 64

DIFF_HEADS = 8
DIFF_DH = 128
DIFF_DV = 256
DIFF_QK = DIFF_HEADS * 2 * DIFF_DH
DIFF_V = DIFF_HEADS * DIFF_DV
ROPE_THETA = 500000.0
ROPE_DIMS = DIFF_DH // 4
ROPE_HALF = ROPE_DIMS // 2

PEER_HEADS = 8
PEER_NKEYS = 128
PEER_N = PEER_NKEYS * PEER_NKEYS
PEER_DKEY = 128
PEER_TOPK = 16

IN_SIZES = (GLA_QK, GLA_QK, GLA_V, GLA_V, 2 * GLA_LOWRANK, DIFF_QK, DIFF_QK, DIFF_V, D_MODEL, D_MODEL)

VMEM_LIMIT = 56 * 1024 * 1024
NEG_BIG = -1e30

ROW_BLOCK = 512
COL_BLOCK = 2048
DIFF_TQ = LP // 4
SEL_TB = 256
PEER_TB = 512
PEER_EB = 512


def _params(*sem):
    return pltpu.CompilerParams(dimension_semantics=sem, vmem_limit_bytes=VMEM_LIMIT)


def _rms(x, g):
    return x * lax.rsqrt(jnp.mean(x * x, axis=-1, keepdims=True) + EPS) * g


def _dot(a, b):
    return jnp.dot(a, b, preferred_element_type=f32)


def _dot_nt(a, b):
    return lax.dot_general(a, b, (((1,), (1,)), ((), ())), preferred_element_type=f32)


def _dot_tn(a, b, precision=None):
    return lax.dot_general(a, b, (((0,), (0,)), ((), ())), preferred_element_type=f32, precision=precision)


def _norm_kernel(x_ref, g_ref, o_ref):
    o_ref[...] = _rms(x_ref[...], g_ref[...]).astype(o_ref.dtype)


def _norm_rows(x, g):
    return pl.pallas_call(
        _norm_kernel,
        out_shape=jax.ShapeDtypeStruct((TP, D_MODEL), bf16),
        grid=(TP // ROW_BLOCK,),
        in_specs=[pl.BlockSpec((ROW_BLOCK, D_MODEL), lambda i: (i, 0)),
                  pl.BlockSpec((1, D_MODEL), lambda i: (0, 0))],
        out_specs=pl.BlockSpec((ROW_BLOCK, D_MODEL), lambda i: (i, 0)),
        compiler_params=_params("parallel"),
        name="mix_norm",
    )(x, g)


def _mm_kernel(a_ref, w_ref, o_ref):
    o_ref[...] = _dot(a_ref[...], w_ref[...]).astype(o_ref.dtype)


def _project(h, w, out_dtype, name):
    n = w.shape[1]
    return pl.pallas_call(
        _mm_kernel,
        out_shape=jax.ShapeDtypeStruct((TP, n), out_dtype),
        grid=(n // COL_BLOCK, TP // ROW_BLOCK),
        in_specs=[pl.BlockSpec((ROW_BLOCK, D_MODEL), lambda j, i: (i, 0)),
                  pl.BlockSpec((D_MODEL, COL_BLOCK), lambda j, i: (0, j))],
        out_specs=pl.BlockSpec((ROW_BLOCK, COL_BLOCK), lambda j, i: (i, j)),
        compiler_params=_params("parallel", "parallel"),
        name=name,
    )(h, w)


def _decay_kernel(h_ref, wlr_ref, w2f_ref, w2b_ref, bf_ref, bb_ref, of_ref, ob_ref):
    lr = _dot(h_ref[...], wlr_ref[...]).astype(bf16)
    zf = _dot(lr, w2f_ref[...]) + bf_ref[...]
    zb = _dot(lr, w2b_ref[...]) + bb_ref[...]
    of_ref[...] = jax.nn.log_sigmoid(zf) * (1.0 / GLA_TAU)
    ob_ref[...] = jax.nn.log_sigmoid(zb) * (1.0 / GLA_TAU)


def _decay(h, wlr, w2f, w2b, b_f, b_b):
    row = lambda i: (i, 0)
    fixed = lambda i: (0, 0)
    return pl.pallas_call(
        _decay_kernel,
        out_shape=(jax.ShapeDtypeStruct((TP, GLA_QK), f32),) * 2,
        grid=(TP // ROW_BLOCK,),
        in_specs=[pl.BlockSpec((ROW_BLOCK, D_MODEL), row),
                  pl.BlockSpec((D_MODEL, LANE), fixed),
                  pl.BlockSpec((LANE, GLA_QK), fixed),
                  pl.BlockSpec((LANE, GLA_QK), fixed),
                  pl.BlockSpec((1, GLA_QK), fixed),
                  pl.BlockSpec((1, GLA_QK), fixed)],
        out_specs=(pl.BlockSpec((ROW_BLOCK, GLA_QK), row),) * 2,
        compiler_params=_params("parallel"),
        name="gla_decay",
    )(h, wlr, w2f, w2b, b_f, b_b)


def _gla_kernel(q_ref, k_ref, v_ref, ff_ref, fb_ref, r_ref, g_ref, o_ref, acc_ref, s_ref):
    C = GLA_CHUNK
    n_chunks = LP // C
    row = lax.broadcasted_iota(jnp.int32, (C, C), 0)
    col = lax.broadcasted_iota(jnp.int32, (C, C), 1)
    ones = jnp.ones((C, LANE), f32)
    hi = lax.Precision.HIGHEST

    def chunk(n, f_ref, cum_mask, keep_mask, total_row, accumulate):
        r0 = pl.multiple_of(n * C, C)
        rows = pl.ds(r0, C)
        lf = f_ref[rows, :]
        cum = jnp.dot(cum_mask.astype(f32), lf, precision=hi, preferred_element_type=f32)
        tot = cum[total_row:total_row + 1, :]
        tot_col = _dot_tn(lf, ones, precision=hi)
        q = q_ref[rows, :].astype(f32) * (GLA_DK ** -0.5)
        k = k_ref[rows, :].astype(f32)
        v = v_ref[rows, :]
        q_in = (q * jnp.exp(cum)).astype(bf16)
        k_in = (k * jnp.exp(-cum)).astype(bf16)
        k_st = (k * jnp.exp(tot - cum)).astype(bf16)
        a = jnp.where(keep_mask, _dot_nt(q_in, k_in), 0.0)
        o = _dot(a.astype(bf16), v) + _dot(q_in, s_ref[...].astype(bf16))
        if accumulate:
            acc_ref[rows, :] += o
        else:
            acc_ref[rows, :] = o
        dec = jnp.exp(tot_col)
        dec = jnp.concatenate([dec] * (GLA_DV // LANE), axis=1)
        s_ref[...] = s_ref[...] * dec + _dot_tn(k_st, v)

    s_ref[...] = jnp.zeros_like(s_ref)

    def fwd(n, carry):
        chunk(n, ff_ref, col <= row, col <= row, C - 1, False)
        return carry

    lax.fori_loop(0, n_chunks, fwd, 0)

    s_ref[...] = jnp.zeros_like(s_ref)

    def bwd(n, carry):
        chunk(n_chunks - 1 - n, fb_ref, col >= row, col > row, 0, True)
        return carry

    lax.fori_loop(0, n_chunks, bwd, 0)

    def fin(n, carry):
        rows = pl.ds(pl.multiple_of(n * LANE, LANE), LANE)
        o = _rms(acc_ref[rows, :], g_ref[...])
        o_ref[rows, :] = (o * jax.nn.silu(r_ref[rows, :])).astype(o_ref.dtype)
        return carry

    lax.fori_loop(0, LP // LANE, fin, 0)


def _gla(proj_a, proj_b, logf_f, logf_b, g_norm):
    kq = GLA_QK // GLA_DK
    kv = (2 * GLA_QK) // GLA_DV
    return pl.pallas_call(
        _gla_kernel,
        out_shape=jax.ShapeDtypeStruct((TP, GLA_V), bf16),
        grid=(BATCH, GLA_HEADS),
        in_specs=[pl.BlockSpec((LP, GLA_DK), lambda b, h: (b, h)),
                  pl.BlockSpec((LP, GLA_DK), lambda b, h: (b, kq + h)),
                  pl.BlockSpec((LP, GLA_DV), lambda b, h: (b, kv + h)),
                  pl.BlockSpec((LP, GLA_DK), lambda b, h: (b, h)),
                  pl.BlockSpec((LP, GLA_DK), lambda b, h: (b, h)),
                  pl.BlockSpec((LP, GLA_DV), lambda b, h: (b, h)),
                  pl.BlockSpec((1, GLA_DV), lambda b, h: (0, 0))],
        out_specs=pl.BlockSpec((LP, GLA_DV), lambda b, h: (b, h)),
        scratch_shapes=[pltpu.VMEM((LP, GLA_DV), f32), pltpu.VMEM((GLA_DK, GLA_DV), f32)],
        compiler_params=_params("parallel", "parallel"),
        name="gla_mixer",
    )(proj_a, proj_a, proj_a, logf_f, logf_b, proj_b, g_norm)


def _rope(x, tab):
    return (x * tab[0]
            + pltpu.roll(x, LANE - ROPE_HALF, axis=1) * tab[1]
            + pltpu.roll(x, ROPE_HALF, axis=1) * tab[2])


def _diff_kernel(lam_init, q_ref, k_ref, v_ref, tq_ref, tk_ref, lq1_ref, lk1_ref, lq2_ref, lk2_ref,
                 g_ref, o_ref, kr_ref):
    @pl.when(pl.program_id(2) == 0)
    def _():
        tk = tk_ref[...]
        for m in range(2):
            kx = k_ref[:, m * DIFF_DH:(m + 1) * DIFF_DH].astype(f32)
            kr_ref[m] = _rope(kx, tk).astype(bf16)

    lam = (jnp.exp(jnp.sum(lq1_ref[...] * lk1_ref[...], axis=1, keepdims=True))
           - jnp.exp(jnp.sum(lq2_ref[...] * lk2_ref[...], axis=1, keepdims=True)) + lam_init)
    tq = tq_ref[...]
    key_ok = lax.broadcasted_iota(jnp.int32, (DIFF_TQ, LP), 1) < L_REAL
    probs = []
    for m in range(2):
        qx = q_ref[:, m * DIFF_DH:(m + 1) * DIFF_DH].astype(f32)
        qr = _rope(qx, tq).astype(bf16)
        s = _dot_nt(qr, kr_ref[m]) * (DIFF_DH ** -0.5)
        s = jnp.where(key_ok, s, NEG_BIG)
        e = jnp.exp(s - jnp.max(s, axis=1, keepdims=True))
        probs.append(e * (1.0 / jnp.sum(e, axis=1, keepdims=True)))
    w = (probs[0] - lam * probs[1]).astype(bf16)
    o = _dot(w, v_ref[...])
    o_ref[...] = (_rms(o, g_ref[...]) * (1.0 - lam_init)).astype(o_ref.dtype)


def _diff_attention(proj_a, rope_tab, lq1, lk1, lq2, lk2, g_norm, lam_init):
    nq = LP // DIFF_TQ
    cq = (2 * GLA_QK + GLA_V) // DIFF_DV
    ck = cq + DIFF_HEADS
    cv = ck + DIFF_HEADS
    vec = pl.BlockSpec((1, DIFF_DH), lambda b, h, i: (0, 0))
    return pl.pallas_call(
        functools.partial(_diff_kernel, lam_init),
        out_shape=jax.ShapeDtypeStruct((TP, DIFF_V), bf16),
        grid=(BATCH, DIFF_HEADS, nq),
        in_specs=[pl.BlockSpec((DIFF_TQ, DIFF_DV), lambda b, h, i: (b * nq + i, cq + h)),
                  pl.BlockSpec((LP, DIFF_DV), lambda b, h, i: (b, ck + h)),
                  pl.BlockSpec((LP, DIFF_DV), lambda b, h, i: (b, cv + h)),
                  pl.BlockSpec((3, DIFF_TQ, DIFF_DH), lambda b, h, i: (0, i, 0)),
                  pl.BlockSpec((3, LP, DIFF_DH), lambda b, h, i: (0, 0, 0)),
                  vec, vec, vec, vec,
                  pl.BlockSpec((1, DIFF_DV), lambda b, h, i: (0, 0))],
        out_specs=pl.BlockSpec((DIFF_TQ, DIFF_DV), lambda b, h, i: (b * nq + i, h)),
        scratch_shapes=[pltpu.VMEM((2, LP, DIFF_DH), bf16)],
        compiler_params=_params("parallel", "parallel", "arbitrary"),
        name="diff_attention",
    )(proj_a, proj_a, proj_a, rope_tab, rope_tab, lq1, lk1, lq2, lk2, g_norm)


def _merge_kernel(og_ref, od_ref, za_ref, zb_ref, hs_ref, wa_ref, wb_ref, wo_ref, g_ref, hs2_ref, hn_ref, hnt_ref):
    y = (jax.nn.sigmoid(za_ref[...]) * _dot(og_ref[...], wa_ref[...])
         + jax.nn.sigmoid(zb_ref[...]) * _dot(od_ref[...], wb_ref[...]))
    hs2 = hs_ref[...] + _dot(y.astype(bf16), wo_ref[...])
    hs2_ref[...] = hs2
    hn = _rms(hs2, g_ref[...])
    hn_ref[...] = hn.astype(hn_ref.dtype)
    hnt_ref[...] = hn.T.astype(hnt_ref.dtype)


def _merge(o_gla, o_diff, proj_b, hs, wa, wb, wo, g_ffn):
    tm = 256
    row = lambda i: (i, 0)
    fixed = lambda i: (0, 0)
    wspec = pl.BlockSpec((D_MODEL, D_MODEL), fixed, pipeline_mode=pl.Buffered(1))
    return pl.pallas_call(
        _merge_kernel,
        out_shape=(jax.ShapeDtypeStruct((TP, D_MODEL), f32), jax.ShapeDtypeStruct((TP, D_MODEL), bf16),
                   jax.ShapeDtypeStruct((D_MODEL, TP), bf16)),
        grid=(TP // tm,),
        in_specs=[pl.BlockSpec((tm, GLA_V), row),
                  pl.BlockSpec((tm, DIFF_V), row),
                  pl.BlockSpec((tm, D_MODEL), lambda i: (i, 1)),
                  pl.BlockSpec((tm, D_MODEL), lambda i: (i, 2)),
                  pl.BlockSpec((tm, D_MODEL), row),
                  wspec, wspec, wspec,
                  pl.BlockSpec((1, D_MODEL), fixed)],
        out_specs=(pl.BlockSpec((tm, D_MODEL), row), pl.BlockSpec((tm, D_MODEL), row),
                   pl.BlockSpec((D_MODEL, tm), lambda i: (0, i))),
        compiler_params=_params("parallel"),
        name="branch_merge",
    )(o_gla, o_diff, proj_b, proj_b, hs, wa, wb, wo, g_ffn)


def _top16(s, iota):
    rank = jnp.full(s.shape, float(PEER_TOPK), f32)
    vals = []
    for r in range(PEER_TOPK):
        m = jnp.max(s, axis=0, keepdims=True)
        first = jnp.min(jnp.where(s == m, iota, float(PEER_NKEYS)), axis=0, keepdims=True)
        hit = iota == first
        rank = jnp.where(hit, float(r), rank)
        s = jnp.where(hit, -jnp.inf, s)
        vals.append(m)
    return vals, rank


def _peer_select_kernel(hn_ref, wq_ref, keys_ref, rank1_ref, cnt0_ref, e0_ref, e1_ref, q_sc):
    tb = SEL_TB
    K = PEER_TOPK
    q_sc[...] = _dot(hn_ref[...], wq_ref[...]).astype(bf16)
    iota = lax.broadcasted_iota(jnp.int32, (PEER_NKEYS, tb), 0).astype(f32)
    i16 = lax.broadcasted_iota(jnp.int32, (K, tb), 0).astype(f32)
    i8 = lax.broadcasted_iota(jnp.int32, (8, tb), 0).astype(f32)
    ids = jnp.concatenate([i16] + [a * float(K) + i8 for a in range(1, 8)] + [(i8 + 8.0) * float(K)], axis=0)
    n_cand = ids.shape[0]

    def head(h, carry):
        c0 = pl.multiple_of(h * (2 * PEER_DKEY), 2 * PEER_DKEY)
        s0 = _dot_nt(keys_ref[h, 0], q_sc[:, pl.ds(c0, PEER_DKEY)])
        s1 = _dot_nt(keys_ref[h, 1], q_sc[:, pl.ds(c0 + PEER_DKEY, PEER_DKEY)])
        v0, rank0 = _top16(s0, iota)
        v1, rank1 = _top16(s1, iota)
        sa = jnp.concatenate(v0, axis=0)
        sb = jnp.concatenate(v1, axis=0)
        cand = jnp.concatenate([sa[0:1] + sb]
                               + [sa[a:a + 1] + sb[0:8] for a in range(1, 8)]
                               + [sa[8:16] + sb[0:1]], axis=0)
        top = cand[0:1]
        taken = jnp.zeros((n_cand, tb), f32)
        z = jnp.zeros((1, tb), f32)
        for _ in range(K):
            m = jnp.max(cand, axis=0, keepdims=True)
            first = jnp.min(jnp.where(cand == m, ids, 1e9), axis=0, keepdims=True)
            hit = ids == first
            taken = jnp.where(hit, 1.0, taken)
            cand = jnp.where(hit, -jnp.inf, cand)
            z = z + jnp.exp(m - top)
        cnt = ([jnp.sum(taken[0:16], axis=0, keepdims=True)]
               + [jnp.sum(taken[8 + 8 * a:16 + 8 * a], axis=0, keepdims=True) for a in range(1, 8)]
               + [taken[72 + a:73 + a] for a in range(8)])
        cnt0 = jnp.zeros((PEER_NKEYS, tb), f32)
        for a in range(K):
            cnt0 = jnp.where(rank0 == float(a), cnt[a], cnt0)
        rank1_ref[h] = rank1
        cnt0_ref[h] = cnt0
        e0_ref[h] = jnp.exp(s0 - v0[0]) * (1.0 / z)
        e1_ref[h] = jnp.exp(s1 - v1[0])
        return carry

    lax.fori_loop(0, PEER_HEADS, head, 0)


def _peer_select(hn, wq, keys):
    tb = SEL_TB
    sel = jax.ShapeDtypeStruct((PEER_HEADS, PEER_NKEYS, TP), f32)
    sel_spec = pl.BlockSpec((PEER_HEADS, PEER_NKEYS, tb), lambda i: (0, 0, i))
    return pl.pallas_call(
        _peer_select_kernel,
        out_shape=(sel,) * 4,
        grid=(TP // tb,),
        in_specs=[pl.BlockSpec((tb, D_MODEL), lambda i: (i, 0)),
                  pl.BlockSpec((D_MODEL, PEER_HEADS * 2 * PEER_DKEY), lambda i: (0, 0)),
                  pl.BlockSpec((PEER_HEADS, 2, PEER_NKEYS, PEER_DKEY), lambda i: (0, 0, 0, 0))],
        out_specs=(sel_spec,) * 4,
        scratch_shapes=[pltpu.VMEM((tb, PEER_HEADS * 2 * PEER_DKEY), bf16)],
        compiler_params=_params("parallel"),
        name="peer_select",
    )(hn, wq, keys)


def _peer_kernel(hnt_ref, u0_ref, un_ref, vt_ref, rank1_ref, cnt0_ref, e0_ref, e1_ref, hs_ref, g_ref, o_ref,
                 acc_ref, a_even_ref, a_odd_ref, p_ref):
    j = pl.program_id(1)
    groups = PEER_EB // PEER_NKEYS

    @pl.when(j == 0)
    def _():
        acc_ref[...] = jnp.zeros_like(acc_ref)
        a_even_ref[...] = _dot(u0_ref[...], hnt_ref[...])

    def step(a_cur_ref, a_next_ref):
        a_next_ref[...] = _dot(un_ref[...], hnt_ref[...])
        for gi in range(groups):
            i = j * groups + gi
            rows = slice(gi * PEER_NKEYS, (gi + 1) * PEER_NKEYS)
            cnts = [cnt0_ref[h, pl.ds(i, 1), :] for h in range(PEER_HEADS)]
            e0s = [e0_ref[h, pl.ds(i, 1), :] for h in range(PEER_HEADS)]
            for c in range(PEER_TB // LANE):
                cols = slice(c * LANE, (c + 1) * LANE)
                a = a_cur_ref[rows, cols]
                act = 0.5 * a * (1.0 + lax.erf(a * (2.0 ** -0.5)))
                w = jnp.zeros((PEER_NKEYS, LANE), f32)
                for h in range(PEER_HEADS):
                    w = w + jnp.where(rank1_ref[h, :, cols] < cnts[h][:, cols], e1_ref[h, :, cols], 0.0) * e0s[h][:, cols]
                p_ref[rows, cols] = (w * act).astype(bf16)
        acc_ref[...] += _dot(vt_ref[...], p_ref[...])

    @pl.when(j % 2 == 0)
    def _():
        step(a_even_ref, a_odd_ref)

    @pl.when(j % 2 == 1)
    def _():
        step(a_odd_ref, a_even_ref)

    @pl.when(j == pl.num_programs(1) - 1)
    def _():
        hs3 = hs_ref[...] + acc_ref[...].T
        o_ref[...] = _rms(hs3, g_ref[...])


def _peer(hnt, u, vt, rank1, cnt0, e0, e1, hs2, g_final):
    tb, eb = PEER_TB, PEER_EB
    n_blocks = PEER_N // eb
    sel_spec = pl.BlockSpec((PEER_HEADS, PEER_NKEYS, tb), lambda i, j: (0, 0, i))
    once = pl.Buffered(1)
    return pl.pallas_call(
        _peer_kernel,
        out_shape=jax.ShapeDtypeStruct((TP, D_MODEL), f32),
        grid=(TP // tb, n_blocks),
        in_specs=[pl.BlockSpec((D_MODEL, tb), lambda i, j: (0, i)),
                  pl.BlockSpec((eb, D_MODEL), lambda i, j: (0, 0), pipeline_mode=once),
                  pl.BlockSpec((eb, D_MODEL), lambda i, j: (jnp.minimum(j + 1, n_blocks - 1), 0)),
                  pl.BlockSpec((D_MODEL, eb), lambda i, j: (0, j)),
                  sel_spec, sel_spec, sel_spec, sel_spec,
                  pl.BlockSpec((tb, D_MODEL), lambda i, j: (i, 0), pipeline_mode=once),
                  pl.BlockSpec((1, D_MODEL), lambda i, j: (0, 0))],
        out_specs=pl.BlockSpec((tb, D_MODEL), lambda i, j: (i, 0)),
        scratch_shapes=[pltpu.VMEM((D_MODEL, tb), f32), pltpu.VMEM((eb, tb), f32), pltpu.VMEM((eb, tb), f32),
                        pltpu.VMEM((eb, tb), bf16)],
        compiler_params=_params("parallel", "arbitrary"),
        name="peer_experts",
    )(hnt, u, u, vt, rank1, cnt0, e0, e1, hs2, g_final)


def _rope_tables():
    inv = 1.0 / (ROPE_THETA ** (jnp.arange(ROPE_HALF, dtype=f32) / ROPE_HALF))
    ang = jnp.arange(LP, dtype=jnp.int32).astype(f32)[:, None] * inv[None, :]
    cos, sin = jnp.cos(ang), jnp.sin(ang)
    rest = DIFF_DH - ROPE_DIMS
    zero, zrest = jnp.zeros_like(sin), jnp.zeros((LP, rest), f32)
    return jnp.stack([
        jnp.concatenate([cos, cos, jnp.ones((LP, rest), f32)], axis=1),
        jnp.concatenate([-sin, zero, zrest], axis=1),
        jnp.concatenate([zero, sin, zrest], axis=1)])


def _split_cols(w):
    parts, start = [], 0
    for n in IN_SIZES:
        parts.append(w[:, start:start + n])
        start += n
    return parts


def kernel(x, meta_tokens, g_mix, w_in, gla_w2_fwd, gla_b_fwd, gla_w2_bwd, gla_b_bwd, gla_g_norm, diff_lq1, diff_lk1, diff_lq2, diff_lk2, diff_g_norm, w_branch_gla, w_branch_diff, w_out, g_ffn, peer_w_q, peer_sub_keys, peer_u, peer_v, g_final):
    assert w_in.shape[0] == 1, "single-layer block only"
    l = 0
    lam_init = 0.8 - 0.6 * math.exp(-0.3 * l)
    meta = jnp.broadcast_to(meta_tokens[None].astype(x.dtype), (BATCH, N_META, D_MODEL))
    pad = jnp.zeros((BATCH, LP - L_REAL, D_MODEL), x.dtype)
    hs = jnp.concatenate([meta, x, pad], axis=1).reshape(TP, D_MODEL)
    rope_tab = _rope_tables()

    gq, gk, gv, gr, glr, dq, dk, dv, za, zb = _split_cols(w_in[l])
    w_a = jnp.concatenate([gq, gk, gv, dq, dk, dv], axis=1).astype(bf16)
    w_b = jnp.concatenate([gr, za, zb], axis=1).astype(bf16)
    w_lr = jnp.pad(glr, ((0, 0), (0, LANE - 2 * GLA_LOWRANK))).astype(bf16)
    w2f = jnp.pad(gla_w2_fwd[l], ((0, LANE - GLA_LOWRANK), (0, 0))).astype(bf16)
    w2b = jnp.pad(gla_w2_bwd[l], ((GLA_LOWRANK, LANE - 2 * GLA_LOWRANK), (0, 0))).astype(bf16)

    h = _norm_rows(hs, g_mix[l][None])
    proj_a = _project(h, w_a, bf16, "in_proj_qkv")
    proj_b = _project(h, w_b, f32, "in_proj_gates")
    logf_f, logf_b = _decay(h, w_lr, w2f, w2b, gla_b_fwd[l][None], gla_b_bwd[l][None])
    o_gla = _gla(proj_a, proj_b, logf_f, logf_b, gla_g_norm[l][None])
    o_diff = _diff_attention(proj_a, rope_tab, diff_lq1[l][None], diff_lk1[l][None], diff_lq2[l][None],
                             diff_lk2[l][None], diff_g_norm[l][None], lam_init)
    hs2, hn, hnt = _merge(o_gla, o_diff, proj_b, hs, w_branch_gla[l].astype(bf16), w_branch_diff[l].astype(bf16),
                     w_out[l].astype(bf16), g_ffn[l][None])
    rank1, cnt0, e0, e1 = _peer_select(hn, peer_w_q[l].astype(bf16), peer_sub_keys[l].astype(bf16))
    out = _peer(hnt, peer_u[l].astype(bf16), peer_v[l].T.astype(bf16), rank1, cnt0, e0, e1, hs2, g_final[None])
    return out.reshape(BATCH, LP, D_MODEL)[:, N_META:L_REAL]
```

```python
import functools
import math

import jax
import jax.numpy as jnp
from jax import lax
from jax.experimental import pallas as pl
from jax.experimental.pallas import tpu as pltpu

f32 = jnp.float32
bf16 = jnp.bfloat16

D_MODEL = 2048
BATCH = 4
SEQ = 2048
N_META = 16
EPS = 1e-6
L_REAL = SEQ + N_META
LANE = 128
LP = -(-L_REAL // LANE) * LANE
TP = BATCH * LP

GLA_HEADS = 4
GLA_DK = 256
GLA_DV = 512
GLA_QK = GLA_HEADS * GLA_DK
GLA_V = GLA_HEADS * GLA_DV
GLA_LOWRANK = 16
GLA_TAU = 16.0
GLA_CHUNK = 64

DIFF_HEADS = 8
DIFF_DH = 128
DIFF_DV = 256
DIFF_QK = DIFF_HEADS * 2 * DIFF_DH
DIFF_V = DIFF_HEADS * DIFF_DV
ROPE_THETA = 500000.0
ROPE_DIMS = DIFF_DH // 4
ROPE_HALF = ROPE_DIMS // 2

PEER_HEADS = 8
PEER_NKEYS = 128
PEER_N = PEER_NKEYS * PEER_NKEYS
PEER_DKEY = 128
PEER_TOPK = 16

IN_SIZES = (GLA_QK, GLA_QK, GLA_V, GLA_V, 2 * GLA_LOWRANK, DIFF_QK, DIFF_QK, DIFF_V, D_MODEL, D_MODEL)

VMEM_LIMIT = 56 * 1024 * 1024
NEG_BIG = -1e30

ROW_BLOCK = 512
COL_BLOCK = 2048
DIFF_TQ = LP // 4
SEL_TB = 256
PEER_TB = 512
PEER_EB = 512


def _params(*sem):
    return pltpu.CompilerParams(dimension_semantics=sem, vmem_limit_bytes=VMEM_LIMIT)


def _rms(x, g):
    return x * lax.rsqrt(jnp.mean(x * x, axis=-1, keepdims=True) + EPS) * g


def _dot(a, b):
    return jnp.dot(a, b, preferred_element_type=f32)


def _dot_nt(a, b):
    return lax.dot_general(a, b, (((1,), (1,)), ((), ())), preferred_element_type=f32)


def _dot_tn(a, b, precision=None):
    return lax.dot_general(a, b, (((0,), (0,)), ((), ())), preferred_element_type=f32, precision=precision)


def _norm_kernel(x_ref, g_ref, o_ref):
    o_ref[...] = _rms(x_ref[...], g_ref[...]).astype(o_ref.dtype)


def _norm_rows(x, g):
    return pl.pallas_call(
        _norm_kernel,
        out_shape=jax.ShapeDtypeStruct((TP, D_MODEL), bf16),
        grid=(TP // ROW_BLOCK,),
        in_specs=[pl.BlockSpec((ROW_BLOCK, D_MODEL), lambda i: (i, 0)),
                  pl.BlockSpec((1, D_MODEL), lambda i: (0, 0))],
        out_specs=pl.BlockSpec((ROW_BLOCK, D_MODEL), lambda i: (i, 0)),
        compiler_params=_params("parallel"),
        name="mix_norm",
    )(x, g)


def _mm_kernel(a_ref, w_ref, o_ref):
    o_ref[...] = _dot(a_ref[...], w_ref[...]).astype(o_ref.dtype)


def _project(h, w, out_dtype, name):
    n = w.shape[1]
    return pl.pallas_call(
        _mm_kernel,
        out_shape=jax.ShapeDtypeStruct((TP, n), out_dtype),
        grid=(n // COL_BLOCK, TP // ROW_BLOCK),
        in_specs=[pl.BlockSpec((ROW_BLOCK, D_MODEL), lambda j, i: (i, 0)),
                  pl.BlockSpec((D_MODEL, COL_BLOCK), lambda j, i: (0, j))],
        out_specs=pl.BlockSpec((ROW_BLOCK, COL_BLOCK), lambda j, i: (i, j)),
        compiler_params=_params("parallel", "parallel"),
        name=name,
    )(h, w)


def _decay_kernel(h_ref, wlr_ref, w2f_ref, w2b_ref, bf_ref, bb_ref, of_ref, ob_ref):
    lr = _dot(h_ref[...], wlr_ref[...]).astype(bf16)
    zf = _dot(lr, w2f_ref[...]) + bf_ref[...]
    zb = _dot(lr, w2b_ref[...]) + bb_ref[...]
    of_ref[...] = jax.nn.log_sigmoid(zf) * (1.0 / GLA_TAU)
    ob_ref[...] = jax.nn.log_sigmoid(zb) * (1.0 / GLA_TAU)


def _decay(h, wlr, w2f, w2b, b_f, b_b):
    row = lambda i: (i, 0)
    fixed = lambda i: (0, 0)
    return pl.pallas_call(
        _decay_kernel,
        out_shape=(jax.ShapeDtypeStruct((TP, GLA_QK), f32),) * 2,
        grid=(TP // ROW_BLOCK,),
        in_specs=[pl.BlockSpec((ROW_BLOCK, D_MODEL), row),
                  pl.BlockSpec((D_MODEL, LANE), fixed),
                  pl.BlockSpec((LANE, GLA_QK), fixed),
                  pl.BlockSpec((LANE, GLA_QK), fixed),
                  pl.BlockSpec((1, GLA_QK), fixed),
                  pl.BlockSpec((1, GLA_QK), fixed)],
        out_specs=(pl.BlockSpec((ROW_BLOCK, GLA_QK), row),) * 2,
        compiler_params=_params("parallel"),
        name="gla_decay",
    )(h, wlr, w2f, w2b, b_f, b_b)


def _gla_kernel(q_ref, k_ref, v_ref, ff_ref, fb_ref, r_ref, g_ref, o_ref, acc_ref, s_ref):
    C = GLA_CHUNK
    n_chunks = LP // C
    row = lax.broadcasted_iota(jnp.int32, (C, C), 0)
    col = lax.broadcasted_iota(jnp.int32, (C, C), 1)
    ones = jnp.ones((C, LANE), f32)
    hi = lax.Precision.HIGHEST

    def chunk(n, f_ref, cum_mask, keep_mask, total_row, accumulate):
        r0 = pl.multiple_of(n * C, C)
        rows = pl.ds(r0, C)
        lf = f_ref[rows, :]
        cum = jnp.dot(cum_mask.astype(f32), lf, precision=hi, preferred_element_type=f32)
        tot = cum[total_row:total_row + 1, :]
        tot_col = _dot_tn(lf, ones, precision=hi)
        q = q_ref[rows, :].astype(f32) * (GLA_DK ** -0.5)
        k = k_ref[rows, :].astype(f32)
        v = v_ref[rows, :]
        q_in = (q * jnp.exp(cum)).astype(bf16)
        k_in = (k * jnp.exp(-cum)).astype(bf16)
        k_st = (k * jnp.exp(tot - cum)).astype(bf16)
        a = jnp.where(keep_mask, _dot_nt(q_in, k_in), 0.0)
        o = _dot(a.astype(bf16), v) + _dot(q_in, s_ref[...].astype(bf16))
        if accumulate:
            acc_ref[rows, :] += o
        else:
            acc_ref[rows, :] = o
        dec = jnp.exp(tot_col)
        dec = jnp.concatenate([dec] * (GLA_DV // LANE), axis=1)
        s_ref[...] = s_ref[...] * dec + _dot_tn(k_st, v)

    s_ref[...] = jnp.zeros_like(s_ref)

    def fwd(n, carry):
        chunk(n, ff_ref, col <= row, col <= row, C - 1, False)
        return carry

    lax.fori_loop(0, n_chunks, fwd, 0)

    s_ref[...] = jnp.zeros_like(s_ref)

    def bwd(n, carry):
        chunk(n_chunks - 1 - n, fb_ref, col >= row, col > row, 0, True)
        return carry

    lax.fori_loop(0, n_chunks, bwd, 0)

    def fin(n, carry):
        rows = pl.ds(pl.multiple_of(n * LANE, LANE), LANE)
        o = _rms(acc_ref[rows, :], g_ref[...])
        o_ref[rows, :] = (o * jax.nn.silu(r_ref[rows, :])).astype(o_ref.dtype)
        return carry

    lax.fori_loop(0, LP // LANE, fin, 0)


def _gla(proj_a, proj_b, logf_f, logf_b, g_norm):
    kq = GLA_QK // GLA_DK
    kv = (2 * GLA_QK) // GLA_DV
    return pl.pallas_call(
        _gla_kernel,
        out_shape=jax.ShapeDtypeStruct((TP, GLA_V), bf16),
        grid=(BATCH, GLA_HEADS),
        in_specs=[pl.BlockSpec((LP, GLA_DK), lambda b, h: (b, h)),
                  pl.BlockSpec((LP, GLA_DK), lambda b, h: (b, kq + h)),
                  pl.BlockSpec((LP, GLA_DV), lambda b, h: (b, kv + h)),
                  pl.BlockSpec((LP, GLA_DK), lambda b, h: (b, h)),
                  pl.BlockSpec((LP, GLA_DK), lambda b, h: (b, h)),
                  pl.BlockSpec((LP, GLA_DV), lambda b, h: (b, h)),
                  pl.BlockSpec((1, GLA_DV), lambda b, h: (0, 0))],
        out_specs=pl.BlockSpec((LP, GLA_DV), lambda b, h: (b, h)),
        scratch_shapes=[pltpu.VMEM((LP, GLA_DV), f32), pltpu.VMEM((GLA_DK, GLA_DV), f32)],
        compiler_params=_params("parallel", "parallel"),
        name="gla_mixer",
    )(proj_a, proj_a, proj_a, logf_f, logf_b, proj_b, g_norm)


def _rope(x, tab):
    return (x * tab[0]
            + pltpu.roll(x, LANE - ROPE_HALF, axis=1) * tab[1]
            + pltpu.roll(x, ROPE_HALF, axis=1) * tab[2])


def _diff_kernel(lam_init, q_ref, k_ref, v_ref, tq_ref, tk_ref, lq1_ref, lk1_ref, lq2_ref, lk2_ref,
                 g_ref, o_ref, kr_ref):
    @pl.when(pl.program_id(2) == 0)
    def _():
        tk = tk_ref[...]
        for m in range(2):
            kx = k_ref[:, m * DIFF_DH:(m + 1) * DIFF_DH].astype(f32)
            kr_ref[m] = _rope(kx, tk).astype(bf16)

    lam = (jnp.exp(jnp.sum(lq1_ref[...] * lk1_ref[...], axis=1, keepdims=True))
           - jnp.exp(jnp.sum(lq2_ref[...] * lk2_ref[...], axis=1, keepdims=True)) + lam_init)
    tq = tq_ref[...]
    key_ok = lax.broadcasted_iota(jnp.int32, (DIFF_TQ, LP), 1) < L_REAL
    probs = []
    for m in range(2):
        qx = q_ref[:, m * DIFF_DH:(m + 1) * DIFF_DH].astype(f32)
        qr = _rope(qx, tq).astype(bf16)
        s = _dot_nt(qr, kr_ref[m]) * (DIFF_DH ** -0.5)
        s = jnp.where(key_ok, s, NEG_BIG)
        e = jnp.exp(s - jnp.max(s, axis=1, keepdims=True))
        probs.append(e * (1.0 / jnp.sum(e, axis=1, keepdims=True)))
    w = (probs[0] - lam * probs[1]).astype(bf16)
    o = _dot(w, v_ref[...])
    o_ref[...] = (_rms(o, g_ref[...]) * (1.0 - lam_init)).astype(o_ref.dtype)


def _diff_attention(proj_a, rope_tab, lq1, lk1, lq2, lk2, g_norm, lam_init):
    nq = LP // DIFF_TQ
    cq = (2 * GLA_QK + GLA_V) // DIFF_DV
    ck = cq + DIFF_HEADS
    cv = ck + DIFF_HEADS
    vec = pl.BlockSpec((1, DIFF_DH), lambda b, h, i: (0, 0))
    return pl.pallas_call(
        functools.partial(_diff_kernel, lam_init),
        out_shape=jax.ShapeDtypeStruct((TP, DIFF_V), bf16),
        grid=(BATCH, DIFF_HEADS, nq),
        in_specs=[pl.BlockSpec((DIFF_TQ, DIFF_DV), lambda b, h, i: (b * nq + i, cq + h)),
                  pl.BlockSpec((LP, DIFF_DV), lambda b, h, i: (b, ck + h)),
                  pl.BlockSpec((LP, DIFF_DV), lambda b, h, i: (b, cv + h)),
                  pl.BlockSpec((3, DIFF_TQ, DIFF_DH), lambda b, h, i: (0, i, 0)),
                  pl.BlockSpec((3, LP, DIFF_DH), lambda b, h, i: (0, 0, 0)),
                  vec, vec, vec, vec,
                  pl.BlockSpec((1, DIFF_DV), lambda b, h, i: (0, 0))],
        out_specs=pl.BlockSpec((DIFF_TQ, DIFF_DV), lambda b, h, i: (b * nq + i, h)),
        scratch_shapes=[pltpu.VMEM((2, LP, DIFF_DH), bf16)],
        compiler_params=_params("parallel", "parallel", "arbitrary"),
        name="diff_attention",
    )(proj_a, proj_a, proj_a, rope_tab, rope_tab, lq1, lk1, lq2, lk2, g_norm)


def _merge_kernel(og_ref, od_ref, za_ref, zb_ref, hs_ref, wa_ref, wb_ref, wo_ref, g_ref, hs2_ref, hn_ref, hnt_ref):
    y = (jax.nn.sigmoid(za_ref[...]) * _dot(og_ref[...], wa_ref[...])
         + jax.nn.sigmoid(zb_ref[...]) * _dot(od_ref[...], wb_ref[...]))
    hs2 = hs_ref[...] + _dot(y.astype(bf16), wo_ref[...])
    hs2_ref[...] = hs2
    hn = _rms(hs2, g_ref[...])
    hn_ref[...] = hn.astype(hn_ref.dtype)
    hnt_ref[...] = hn.T.astype(hnt_ref.dtype)


def _merge(o_gla, o_diff, proj_b, hs, wa, wb, wo, g_ffn):
    tm = 256
    row = lambda i: (i, 0)
    fixed = lambda i: (0, 0)
    wspec = pl.BlockSpec((D_MODEL, D_MODEL), fixed, pipeline_mode=pl.Buffered(1))
    return pl.pallas_call(
        _merge_kernel,
        out_shape=(jax.ShapeDtypeStruct((TP, D_MODEL), f32), jax.ShapeDtypeStruct((TP, D_MODEL), bf16),
                   jax.ShapeDtypeStruct((D_MODEL, TP), bf16)),
        grid=(TP // tm,),
        in_specs=[pl.BlockSpec((tm, GLA_V), row),
                  pl.BlockSpec((tm, DIFF_V), row),
                  pl.BlockSpec((tm, D_MODEL), lambda i: (i, 1)),
                  pl.BlockSpec((tm, D_MODEL), lambda i: (i, 2)),
                  pl.BlockSpec((tm, D_MODEL), row),
                  wspec, wspec, wspec,
                  pl.BlockSpec((1, D_MODEL), fixed)],
        out_specs=(pl.BlockSpec((tm, D_MODEL), row), pl.BlockSpec((tm, D_MODEL), row),
                   pl.BlockSpec((D_MODEL, tm), lambda i: (0, i))),
        compiler_params=_params("parallel"),
        name="branch_merge",
    )(o_gla, o_diff, proj_b, proj_b, hs, wa, wb, wo, g_ffn)


def _top16(s, iota):
    rank = jnp.full(s.shape, float(PEER_TOPK), f32)
    vals = []
    for r in range(PEER_TOPK):
        m = jnp.max(s, axis=0, keepdims=True)
        first = jnp.min(jnp.where(s == m, iota, float(PEER_NKEYS)), axis=0, keepdims=True)
        hit = iota == first
        rank = jnp.where(hit, float(r), rank)
        s = jnp.where(hit, -jnp.inf, s)
        vals.append(m)
    return vals, rank


def _peer_select_kernel(hn_ref, wq_ref, keys_ref, rank1_ref, cnt0_ref, e0_ref, e1_ref, q_sc):
    tb = SEL_TB
    K = PEER_TOPK
    q_sc[...] = _dot(hn_ref[...], wq_ref[...]).astype(bf16)
    iota = lax.broadcasted_iota(jnp.int32, (PEER_NKEYS, tb), 0).astype(f32)
    i16 = lax.broadcasted_iota(jnp.int32, (K, tb), 0).astype(f32)
    i8 = lax.broadcasted_iota(jnp.int32, (8, tb), 0).astype(f32)
    ids = jnp.concatenate([i16] + [a * float(K) + i8 for a in range(1, 8)] + [(i8 + 8.0) * float(K)], axis=0)
    n_cand = ids.shape[0]

    def head(h, carry):
        c0 = pl.multiple_of(h * (2 * PEER_DKEY), 2 * PEER_DKEY)
        s0 = _dot_nt(keys_ref[h, 0], q_sc[:, pl.ds(c0, PEER_DKEY)])
        s1 = _dot_nt(keys_ref[h, 1], q_sc[:, pl.ds(c0 + PEER_DKEY, PEER_DKEY)])
        v0, rank0 = _top16(s0, iota)
        v1, rank1 = _top16(s1, iota)
        sa = jnp.concatenate(v0, axis=0)
        sb = jnp.concatenate(v1, axis=0)
        cand = jnp.concatenate([sa[0:1] + sb]
                               + [sa[a:a + 1] + sb[0:8] for a in range(1, 8)]
                               + [sa[8:16] + sb[0:1]], axis=0)
        top = cand[0:1]
        taken = jnp.zeros((n_cand, tb), f32)
        z = jnp.zeros((1, tb), f32)
        for _ in range(K):
            m = jnp.max(cand, axis=0, keepdims=True)
            first = jnp.min(jnp.where(cand == m, ids, 1e9), axis=0, keepdims=True)
            hit = ids == first
            taken = jnp.where(hit, 1.0, taken)
            cand = jnp.where(hit, -jnp.inf, cand)
            z = z + jnp.exp(m - top)
        cnt = ([jnp.sum(taken[0:16], axis=0, keepdims=True)]
               + [jnp.sum(taken[8 + 8 * a:16 + 8 * a], axis=0, keepdims=True) for a in range(1, 8)]
               + [taken[72 + a:73 + a] for a in range(8)])
        cnt0 = jnp.zeros((PEER_NKEYS, tb), f32)
        for a in range(K):
            cnt0 = jnp.where(rank0 == float(a), cnt[a], cnt0)
        rank1_ref[h] = rank1
        cnt0_ref[h] = cnt0
        e0_ref[h] = jnp.exp(s0 - v0[0]) * (1.0 / z)
        e1_ref[h] = jnp.exp(s1 - v1[0])
        return carry

    lax.fori_loop(0, PEER_HEADS, head, 0)


def _peer_select(hn, wq, keys):
    tb = SEL_TB
    sel = jax.ShapeDtypeStruct((PEER_HEADS, PEER_NKEYS, TP), f32)
    sel_spec = pl.BlockSpec((PEER_HEADS, PEER_NKEYS, tb), lambda i: (0, 0, i))
    return pl.pallas_call(
        _peer_select_kernel,
        out_shape=(sel,) * 4,
        grid=(TP // tb,),
        in_specs=[pl.BlockSpec((tb, D_MODEL), lambda i: (i, 0)),
                  pl.BlockSpec((D_MODEL, PEER_HEADS * 2 * PEER_DKEY), lambda i: (0, 0)),
                  pl.BlockSpec((PEER_HEADS, 2, PEER_NKEYS, PEER_DKEY), lambda i: (0, 0, 0, 0))],
        out_specs=(sel_spec,) * 4,
        scratch_shapes=[pltpu.VMEM((tb, PEER_HEADS * 2 * PEER_DKEY), bf16)],
        compiler_params=_params("parallel"),
        name="peer_select",
    )(hn, wq, keys)


def _peer_kernel(hnt_ref, u0_ref, un_ref, vt_ref, rank1_ref, cnt0_ref, e0_ref, e1_ref, hs_ref, g_ref, o_ref,
                 acc_ref, a_even_ref, a_odd_ref, p_ref):
    j = pl.program_id(1)
    groups = PEER_EB // PEER_NKEYS

    @pl.when(j == 0)
    def _():
        acc_ref[...] = jnp.zeros_like(acc_ref)
        a_even_ref[...] = _dot(u0_ref[...], hnt_ref[...])

    def step(a_cur_ref, a_next_ref):
        a_next_ref[...] = _dot(un_ref[...], hnt_ref[...])
        for gi in range(groups):
            i = j * groups + gi
            rows = slice(gi * PEER_NKEYS, (gi + 1) * PEER_NKEYS)
            cnts = [cnt0_ref[h, pl.ds(i, 1), :] for h in range(PEER_HEADS)]
            e0s = [e0_ref[h, pl.ds(i, 1), :] for h in range(PEER_HEADS)]
            for c in range(PEER_TB // LANE):
                cols = slice(c * LANE, (c + 1) * LANE)
                a = a_cur_ref[rows, cols]
                act = 0.5 * a * (1.0 + lax.erf(a * (2.0 ** -0.5)))
                w = jnp.zeros((PEER_NKEYS, LANE), f32)
                for h in range(PEER_HEADS):
                    w = w + jnp.where(rank1_ref[h, :, cols] < cnts[h][:, cols], e1_ref[h, :, cols], 0.0) * e0s[h][:, cols]
                p_ref[rows, cols] = (w * act).astype(bf16)
        acc_ref[...] += _dot(vt_ref[0], p_ref[...])

    @pl.when(j % 2 == 0)
    def _():
        step(a_even_ref, a_odd_ref)

    @pl.when(j % 2 == 1)
    def _():
        step(a_odd_ref, a_even_ref)

    @pl.when(j == pl.num_programs(1) - 1)
    def _():
        hs3 = hs_ref[...] + acc_ref[...].T
        o_ref[...] = _rms(hs3, g_ref[...])


def _peer(hnt, u, vt, rank1, cnt0, e0, e1, hs2, g_final):
    tb, eb = PEER_TB, PEER_EB
    n_blocks = PEER_N // eb
    sel_spec = pl.BlockSpec((PEER_HEADS, PEER_NKEYS, tb), lambda i, j: (0, 0, i))
    once = pl.Buffered(1)
    return pl.pallas_call(
        _peer_kernel,
        out_shape=jax.ShapeDtypeStruct((TP, D_MODEL), f32),
        grid=(TP // tb, n_blocks),
        in_specs=[pl.BlockSpec((D_MODEL, tb), lambda i, j: (0, i)),
                  pl.BlockSpec((eb, D_MODEL), lambda i, j: (0, 0), pipeline_mode=once),
                  pl.BlockSpec((eb, D_MODEL), lambda i, j: (jnp.minimum(j + 1, n_blocks - 1), 0)),
                  pl.BlockSpec((1, D_MODEL, eb), lambda i, j: (j, 0, 0)),
                  sel_spec, sel_spec, sel_spec, sel_spec,
                  pl.BlockSpec((tb, D_MODEL), lambda i, j: (i, 0), pipeline_mode=once),
                  pl.BlockSpec((1, D_MODEL), lambda i, j: (0, 0))],
        out_specs=pl.BlockSpec((tb, D_MODEL), lambda i, j: (i, 0)),
        scratch_shapes=[pltpu.VMEM((D_MODEL, tb), f32), pltpu.VMEM((eb, tb), f32), pltpu.VMEM((eb, tb), f32),
                        pltpu.VMEM((eb, tb), bf16)],
        compiler_params=_params("parallel", "arbitrary"),
        name="peer_experts",
    )(hnt, u, u, vt, rank1, cnt0, e0, e1, hs2, g_final)


def _rope_tables():
    inv = 1.0 / (ROPE_THETA ** (jnp.arange(ROPE_HALF, dtype=f32) / ROPE_HALF))
    ang = jnp.arange(LP, dtype=jnp.int32).astype(f32)[:, None] * inv[None, :]
    cos, sin = jnp.cos(ang), jnp.sin(ang)
    rest = DIFF_DH - ROPE_DIMS
    zero, zrest = jnp.zeros_like(sin), jnp.zeros((LP, rest), f32)
    return jnp.stack([
        jnp.concatenate([cos, cos, jnp.ones((LP, rest), f32)], axis=1),
        jnp.concatenate([-sin, zero, zrest], axis=1),
        jnp.concatenate([zero, sin, zrest], axis=1)])


def _split_cols(w):
    parts, start = [], 0
    for n in IN_SIZES:
        parts.append(w[:, start:start + n])
        start += n
    return parts


def kernel(x, meta_tokens, g_mix, w_in, gla_w2_fwd, gla_b_fwd, gla_w2_bwd, gla_b_bwd, gla_g_norm, diff_lq1, diff_lk1, diff_lq2, diff_lk2, diff_g_norm, w_branch_gla, w_branch_diff, w_out, g_ffn, peer_w_q, peer_sub_keys, peer_u, peer_v, g_final):
    assert w_in.shape[0] == 1, "single-layer block only"
    l = 0
    lam_init = 0.8 - 0.6 * math.exp(-0.3 * l)
    meta = jnp.broadcast_to(meta_tokens[None].astype(x.dtype), (BATCH, N_META, D_MODEL))
    pad = jnp.zeros((BATCH, LP - L_REAL, D_MODEL), x.dtype)
    hs = jnp.concatenate([meta, x, pad], axis=1).reshape(TP, D_MODEL)
    rope_tab = _rope_tables()

    gq, gk, gv, gr, glr, dq, dk, dv, za, zb = _split_cols(w_in[l])
    w_a = jnp.concatenate([gq, gk, gv, dq, dk, dv], axis=1).astype(bf16)
    w_b = jnp.concatenate([gr, za, zb], axis=1).astype(bf16)
    w_lr = jnp.pad(glr, ((0, 0), (0, LANE - 2 * GLA_LOWRANK))).astype(bf16)
    w2f = jnp.pad(gla_w2_fwd[l], ((0, LANE - GLA_LOWRANK), (0, 0))).astype(bf16)
    w2b = jnp.pad(gla_w2_bwd[l], ((GLA_LOWRANK, LANE - 2 * GLA_LOWRANK), (0, 0))).astype(bf16)

    h = _norm_rows(hs, g_mix[l][None])
    proj_a = _project(h, w_a, bf16, "in_proj_qkv")
    proj_b = _project(h, w_b, f32, "in_proj_gates")
    logf_f, logf_b = _decay(h, w_lr, w2f, w2b, gla_b_fwd[l][None], gla_b_bwd[l][None])
    o_gla = _gla(proj_a, proj_b, logf_f, logf_b, gla_g_norm[l][None])
    o_diff = _diff_attention(proj_a, rope_tab, diff_lq1[l][None], diff_lk1[l][None], diff_lq2[l][None],
                             diff_lk2[l][None], diff_g_norm[l][None], lam_init)
    hs2, hn, hnt = _merge(o_gla, o_diff, proj_b, hs, w_branch_gla[l].astype(bf16), w_branch_diff[l].astype(bf16),
                     w_out[l].astype(bf16), g_ffn[l][None])
    rank1, cnt0, e0, e1 = _peer_select(hn, peer_w_q[l].astype(bf16), peer_sub_keys[l].astype(bf16))
    v_blocks = peer_v[l].reshape(PEER_N // PEER_EB, PEER_EB, D_MODEL).transpose(0, 2, 1).astype(bf16)
    out = _peer(hnt, peer_u[l].astype(bf16), v_blocks, rank1, cnt0, e0, e1, hs2, g_final[None])
    return out.reshape(BATCH, LP, D_MODEL)[:, N_META:L_REAL]
```

```python
import functools
import math

import jax
import jax.numpy as jnp
from jax import lax
from jax.experimental import pallas as pl
from jax.experimental.pallas import tpu as pltpu

f32 = jnp.float32
bf16 = jnp.bfloat16

D_MODEL = 2048
BATCH = 4
SEQ = 2048
N_META = 16
EPS = 1e-6
L_REAL = SEQ + N_META
LANE = 128
BF16_ROWS = 16
LP = -(-L_REAL // LANE) * LANE
TP = BATCH * LP

GLA_HEADS = 4
GLA_DK = 256
GLA_DV = 512
GLA_QK = GLA_HEADS * GLA_DK
GLA_V = GLA_HEADS * GLA_DV
GLA_LOWRANK = 16
GLA_TAU = 16.0
GLA_CHUNK = 64

DIFF_HEADS = 8
DIFF_DH = 128
DIFF_DV = 256
DIFF_QK = DIFF_HEADS * 2 * DIFF_DH
DIFF_V = DIFF_HEADS * DIFF_DV
ROPE_THETA = 500000.0
ROPE_DIMS = DIFF_DH // 4
ROPE_HALF = ROPE_DIMS // 2

PEER_HEADS = 8
PEER_NKEYS = 128
PEER_N = PEER_NKEYS * PEER_NKEYS
PEER_DKEY = 128
PEER_TOPK = 16

IN_SIZES = (GLA_QK, GLA_QK, GLA_V, GLA_V, 2 * GLA_LOWRANK, DIFF_QK, DIFF_QK, DIFF_V, D_MODEL, D_MODEL)

VMEM_LIMIT = 56 * 1024 * 1024
NEG_BIG = -1e30

ROW_BLOCK = 512
COL_BLOCK = 2048
DIFF_TQ = LP // 4
SEL_TB = 256
PEER_TB = 512
PEER_EB = 512


def _params(*sem):
    return pltpu.CompilerParams(dimension_semantics=sem, vmem_limit_bytes=VMEM_LIMIT)


def _rms(x, g):
    return x * lax.rsqrt(jnp.mean(x * x, axis=-1, keepdims=True) + EPS) * g


def _dot(a, b):
    return jnp.dot(a, b, preferred_element_type=f32)


def _dot_nt(a, b):
    return lax.dot_general(a, b, (((1,), (1,)), ((), ())), preferred_element_type=f32)


def _dot_tn(a, b, precision=None):
    return lax.dot_general(a, b, (((0,), (0,)), ((), ())), preferred_element_type=f32, precision=precision)


def _norm_kernel(x_ref, g_ref, o_ref):
    o_ref[...] = _rms(x_ref[...], g_ref[...]).astype(o_ref.dtype)


def _norm_rows(x, g):
    return pl.pallas_call(
        _norm_kernel,
        out_shape=jax.ShapeDtypeStruct((TP, D_MODEL), bf16),
        grid=(TP // ROW_BLOCK,),
        in_specs=[pl.BlockSpec((ROW_BLOCK, D_MODEL), lambda i: (i, 0)),
                  pl.BlockSpec((1, D_MODEL), lambda i: (0, 0))],
        out_specs=pl.BlockSpec((ROW_BLOCK, D_MODEL), lambda i: (i, 0)),
        compiler_params=_params("parallel"),
        name="mix_norm",
    )(x, g)


def _mm_kernel(a_ref, w_ref, o_ref):
    o_ref[...] = _dot(a_ref[...], w_ref[...]).astype(o_ref.dtype)


def _project(h, w, out_dtype, name):
    n = w.shape[1]
    return pl.pallas_call(
        _mm_kernel,
        out_shape=jax.ShapeDtypeStruct((TP, n), out_dtype),
        grid=(n // COL_BLOCK, TP // ROW_BLOCK),
        in_specs=[pl.BlockSpec((ROW_BLOCK, D_MODEL), lambda j, i: (i, 0)),
                  pl.BlockSpec((D_MODEL, COL_BLOCK), lambda j, i: (0, j))],
        out_specs=pl.BlockSpec((ROW_BLOCK, COL_BLOCK), lambda j, i: (i, j)),
        compiler_params=_params("parallel", "parallel"),
        name=name,
    )(h, w)


def _decay_kernel(h_ref, wlr_ref, w2f_ref, w2b_ref, bf_ref, bb_ref, of_ref, ob_ref):
    lr = _dot(h_ref[...], wlr_ref[...]).astype(bf16)
    zf = _dot(lr, w2f_ref[...]) + bf_ref[...]
    zb = _dot(lr, w2b_ref[...]) + bb_ref[...]
    of_ref[...] = jax.nn.log_sigmoid(zf) * (1.0 / GLA_TAU)
    ob_ref[...] = jax.nn.log_sigmoid(zb) * (1.0 / GLA_TAU)


def _decay(h, wlr, w2f, w2b, b_f, b_b):
    row = lambda i: (i, 0)
    fixed = lambda i: (0, 0)
    return pl.pallas_call(
        _decay_kernel,
        out_shape=(jax.ShapeDtypeStruct((TP, GLA_QK), f32),) * 2,
        grid=(TP // ROW_BLOCK,),
        in_specs=[pl.BlockSpec((ROW_BLOCK, D_MODEL), row),
                  pl.BlockSpec((D_MODEL, LANE), fixed),
                  pl.BlockSpec((LANE, GLA_QK), fixed),
                  pl.BlockSpec((LANE, GLA_QK), fixed),
                  pl.BlockSpec((1, GLA_QK), fixed),
                  pl.BlockSpec((1, GLA_QK), fixed)],
        out_specs=(pl.BlockSpec((ROW_BLOCK, GLA_QK), row),) * 2,
        compiler_params=_params("parallel"),
        name="gla_decay",
    )(h, wlr, w2f, w2b, b_f, b_b)


def _gla_kernel(q_ref, k_ref, v_ref, ff_ref, fb_ref, r_ref, g_ref, o_ref, accf_ref, accb_ref, sf_ref, sb_ref):
    C = GLA_CHUNK
    n_chunks = LP // C
    row = lax.broadcasted_iota(jnp.int32, (C, C), 0)
    col = lax.broadcasted_iota(jnp.int32, (C, C), 1)
    ones = jnp.ones((C, LANE), f32)
    hi = lax.Precision.HIGHEST

    def chunk(n, f_ref, cum_mask, keep_mask, total_row, acc_ref, s_ref):
        r0 = pl.multiple_of(n * C, C)
        rows = pl.ds(r0, C)
        lf = f_ref[rows, :]
        cum = jnp.dot(cum_mask.astype(f32), lf, precision=hi, preferred_element_type=f32)
        tot = cum[total_row:total_row + 1, :]
        tot_col = _dot_tn(lf, ones, precision=hi)
        q = q_ref[rows, :].astype(f32) * (GLA_DK ** -0.5)
        k = k_ref[rows, :].astype(f32)
        v = v_ref[rows, :]
        q_in = (q * jnp.exp(cum)).astype(bf16)
        k_in = (k * jnp.exp(-cum)).astype(bf16)
        k_st = (k * jnp.exp(tot - cum)).astype(bf16)
        a = jnp.where(keep_mask, _dot_nt(q_in, k_in), 0.0)
        acc_ref[rows, :] = _dot(a.astype(bf16), v) + _dot(q_in, s_ref[...].astype(bf16))
        dec = jnp.exp(tot_col)
        dec = jnp.concatenate([dec] * (GLA_DV // LANE), axis=1)
        s_ref[...] = s_ref[...] * dec + _dot_tn(k_st, v)

    sf_ref[...] = jnp.zeros_like(sf_ref)
    sb_ref[...] = jnp.zeros_like(sb_ref)

    def both(n, carry):
        chunk(n, ff_ref, col <= row, col <= row, C - 1, accf_ref, sf_ref)
        chunk(n_chunks - 1 - n, fb_ref, col >= row, col > row, 0, accb_ref, sb_ref)
        return carry

    lax.fori_loop(0, n_chunks, both, 0)

    def fin(n, carry):
        rows = pl.ds(pl.multiple_of(n * LANE, LANE), LANE)
        o = _rms(accf_ref[rows, :] + accb_ref[rows, :], g_ref[...])
        o_ref[rows, :] = (o * jax.nn.silu(r_ref[rows, :])).astype(o_ref.dtype)
        return carry

    lax.fori_loop(0, LP // LANE, fin, 0)


def _gla(proj_a, proj_b, logf_f, logf_b, g_norm):
    kq = GLA_QK // GLA_DK
    kv = (2 * GLA_QK) // GLA_DV
    return pl.pallas_call(
        _gla_kernel,
        out_shape=jax.ShapeDtypeStruct((TP, GLA_V), bf16),
        grid=(BATCH, GLA_HEADS),
        in_specs=[pl.BlockSpec((LP, GLA_DK), lambda b, h: (b, h)),
                  pl.BlockSpec((LP, GLA_DK), lambda b, h: (b, kq + h)),
                  pl.BlockSpec((LP, GLA_DV), lambda b, h: (b, kv + h)),
                  pl.BlockSpec((LP, GLA_DK), lambda b, h: (b, h)),
                  pl.BlockSpec((LP, GLA_DK), lambda b, h: (b, h)),
                  pl.BlockSpec((LP, GLA_DV), lambda b, h: (b, h)),
                  pl.BlockSpec((1, GLA_DV), lambda b, h: (0, 0))],
        out_specs=pl.BlockSpec((LP, GLA_DV), lambda b, h: (b, h)),
        scratch_shapes=[pltpu.VMEM((LP, GLA_DV), f32), pltpu.VMEM((LP, GLA_DV), f32),
                        pltpu.VMEM((GLA_DK, GLA_DV), f32), pltpu.VMEM((GLA_DK, GLA_DV), f32)],
        compiler_params=_params("parallel", "parallel"),
        name="gla_mixer",
    )(proj_a, proj_a, proj_a, logf_f, logf_b, proj_b, g_norm)


def _rope(x, tab):
    return (x * tab[0]
            + pltpu.roll(x, LANE - ROPE_HALF, axis=1) * tab[1]
            + pltpu.roll(x, ROPE_HALF, axis=1) * tab[2])


def _diff_kernel(lam_init, q_ref, k_ref, v_ref, tq_ref, tk_ref, lq1_ref, lk1_ref, lq2_ref, lk2_ref,
                 g_ref, o_ref, kr_ref):
    @pl.when(pl.program_id(2) == 0)
    def _():
        tk = tk_ref[...]
        for m in range(2):
            kx = k_ref[:, m * DIFF_DH:(m + 1) * DIFF_DH].astype(f32)
            kr_ref[m] = _rope(kx, tk).astype(bf16)

    lam = (jnp.exp(jnp.sum(lq1_ref[...] * lk1_ref[...], axis=1, keepdims=True))
           - jnp.exp(jnp.sum(lq2_ref[...] * lk2_ref[...], axis=1, keepdims=True)) + lam_init)
    tq = tq_ref[...]
    key_ok = lax.broadcasted_iota(jnp.int32, (DIFF_TQ, LP), 1) < L_REAL
    probs = []
    for m in range(2):
        qx = q_ref[:, m * DIFF_DH:(m + 1) * DIFF_DH].astype(f32)
        qr = _rope(qx, tq).astype(bf16)
        s = _dot_nt(qr, kr_ref[m]) * (DIFF_DH ** -0.5)
        s = jnp.where(key_ok, s, NEG_BIG)
        e = jnp.exp(s - jnp.max(s, axis=1, keepdims=True))
        probs.append(e * (1.0 / jnp.sum(e, axis=1, keepdims=True)))
    w = (probs[0] - lam * probs[1]).astype(bf16)
    o = _dot(w, v_ref[...])
    o_ref[...] = (_rms(o, g_ref[...]) * (1.0 - lam_init)).astype(o_ref.dtype)


def _diff_attention(proj_a, rope_tab, lq1, lk1, lq2, lk2, g_norm, lam_init):
    nq = LP // DIFF_TQ
    cq = (2 * GLA_QK + GLA_V) // DIFF_DV
    ck = cq + DIFF_HEADS
    cv = ck + DIFF_HEADS
    vec = pl.BlockSpec((1, DIFF_DH), lambda b, h, i: (0, 0))
    return pl.pallas_call(
        functools.partial(_diff_kernel, lam_init),
        out_shape=jax.ShapeDtypeStruct((TP, DIFF_V), bf16),
        grid=(BATCH, DIFF_HEADS, nq),
        in_specs=[pl.BlockSpec((DIFF_TQ, DIFF_DV), lambda b, h, i: (b * nq + i, cq + h)),
                  pl.BlockSpec((LP, DIFF_DV), lambda b, h, i: (b, ck + h)),
                  pl.BlockSpec((LP, DIFF_DV), lambda b, h, i: (b, cv + h)),
                  pl.BlockSpec((3, DIFF_TQ, DIFF_DH), lambda b, h, i: (0, i, 0)),
                  pl.BlockSpec((3, LP, DIFF_DH), lambda b, h, i: (0, 0, 0)),
                  vec, vec, vec, vec,
                  pl.BlockSpec((1, DIFF_DV), lambda b, h, i: (0, 0))],
        out_specs=pl.BlockSpec((DIFF_TQ, DIFF_DV), lambda b, h, i: (b * nq + i, h)),
        scratch_shapes=[pltpu.VMEM((2, LP, DIFF_DH), bf16)],
        compiler_params=_params("parallel", "parallel", "arbitrary"),
        name="diff_attention",
    )(proj_a, proj_a, proj_a, rope_tab, rope_tab, lq1, lk1, lq2, lk2, g_norm)


def _merge_kernel(og_ref, od_ref, za_ref, zb_ref, hs_ref, wa_ref, wb_ref, wo_ref, g_ref, hs2_ref, hn_ref, hnt_ref):
    y = (jax.nn.sigmoid(za_ref[...]) * _dot(og_ref[...], wa_ref[...])
         + jax.nn.sigmoid(zb_ref[...]) * _dot(od_ref[...], wb_ref[...]))
    hs2 = hs_ref[...] + _dot(y.astype(bf16), wo_ref[...])
    hs2_ref[...] = hs2
    hn = _rms(hs2, g_ref[...])
    hn_ref[...] = hn.astype(hn_ref.dtype)
    hnt_ref[...] = hn.T.astype(hnt_ref.dtype)


def _merge(o_gla, o_diff, proj_b, hs, wa, wb, wo, g_ffn):
    tm = 256
    row = lambda i: (i, 0)
    fixed = lambda i: (0, 0)
    wspec = pl.BlockSpec((D_MODEL, D_MODEL), fixed, pipeline_mode=pl.Buffered(1))
    return pl.pallas_call(
        _merge_kernel,
        out_shape=(jax.ShapeDtypeStruct((TP, D_MODEL), f32), jax.ShapeDtypeStruct((TP, D_MODEL), bf16),
                   jax.ShapeDtypeStruct((D_MODEL, TP), bf16)),
        grid=(TP // tm,),
        in_specs=[pl.BlockSpec((tm, GLA_V), row),
                  pl.BlockSpec((tm, DIFF_V), row),
                  pl.BlockSpec((tm, D_MODEL), lambda i: (i, 1)),
                  pl.BlockSpec((tm, D_MODEL), lambda i: (i, 2)),
                  pl.BlockSpec((tm, D_MODEL), row),
                  wspec, wspec, wspec,
                  pl.BlockSpec((1, D_MODEL), fixed)],
        out_specs=(pl.BlockSpec((tm, D_MODEL), row), pl.BlockSpec((tm, D_MODEL), row),
                   pl.BlockSpec((D_MODEL, tm), lambda i: (0, i))),
        compiler_params=_params("parallel"),
        name="branch_merge",
    )(o_gla, o_diff, proj_b, proj_b, hs, wa, wb, wo, g_ffn)


def _top16(s, iota):
    rank = jnp.full(s.shape, float(PEER_TOPK), f32)
    vals = []
    for r in range(PEER_TOPK):
        m = jnp.max(s, axis=0, keepdims=True)
        first = jnp.min(jnp.where(s == m, iota, float(PEER_NKEYS)), axis=0, keepdims=True)
        hit = iota == first
        rank = jnp.where(hit, float(r), rank)
        s = jnp.where(hit, -jnp.inf, s)
        vals.append(m)
    return vals, rank


def _peer_select_kernel(hn_ref, wq_ref, keys_ref, rank1_ref, cnt0_ref, e0_ref, e1_ref, q_sc):
    tb = SEL_TB
    K = PEER_TOPK
    q_sc[...] = _dot(hn_ref[...], wq_ref[...]).astype(bf16)
    iota = lax.broadcasted_iota(jnp.int32, (PEER_NKEYS, tb), 0).astype(f32)
    i16 = lax.broadcasted_iota(jnp.int32, (K, tb), 0).astype(f32)
    i8 = lax.broadcasted_iota(jnp.int32, (8, tb), 0).astype(f32)
    ids = jnp.concatenate([i16] + [a * float(K) + i8 for a in range(1, 8)] + [(i8 + 8.0) * float(K)], axis=0)
    n_cand = ids.shape[0]

    def head(h, carry):
        c0 = pl.multiple_of(h * (2 * PEER_DKEY), 2 * PEER_DKEY)
        s0 = _dot_nt(keys_ref[h, 0], q_sc[:, pl.ds(c0, PEER_DKEY)])
        s1 = _dot_nt(keys_ref[h, 1], q_sc[:, pl.ds(c0 + PEER_DKEY, PEER_DKEY)])
        v0, rank0 = _top16(s0, iota)
        v1, rank1 = _top16(s1, iota)
        sa = jnp.concatenate(v0, axis=0)
        sb = jnp.concatenate(v1, axis=0)
        cand = jnp.concatenate([sa[0:1] + sb]
                               + [sa[a:a + 1] + sb[0:8] for a in range(1, 8)]
                               + [sa[8:16] + sb[0:1]], axis=0)
        top = cand[0:1]
        taken = jnp.zeros((n_cand, tb), f32)
        z = jnp.zeros((1, tb), f32)
        for _ in range(K):
            m = jnp.max(cand, axis=0, keepdims=True)
            first = jnp.min(jnp.where(cand == m, ids, 1e9), axis=0, keepdims=True)
            hit = ids == first
            taken = jnp.where(hit, 1.0, taken)
            cand = jnp.where(hit, -jnp.inf, cand)
            z = z + jnp.exp(m - top)
        cnt = ([jnp.sum(taken[0:16], axis=0, keepdims=True)]
               + [jnp.sum(taken[8 + 8 * a:16 + 8 * a], axis=0, keepdims=True) for a in range(1, 8)]
               + [taken[72 + a:73 + a] for a in range(8)])
        cnt0 = jnp.zeros((PEER_NKEYS, tb), f32)
        for a in range(K):
            cnt0 = jnp.where(rank0 == float(a), cnt[a], cnt0)
        rank1_ref[h] = rank1.astype(bf16)
        cnt0_ref[h] = cnt0
        e0_ref[h] = jnp.exp(s0 - v0[0]) * (1.0 / z)
        e1_ref[h] = jnp.exp(s1 - v1[0]).astype(bf16)
        return carry

    lax.fori_loop(0, PEER_HEADS, head, 0)


def _peer_select(hn, wq, keys):
    tb = SEL_TB
    sel = lambda dt: jax.ShapeDtypeStruct((PEER_HEADS, PEER_NKEYS, TP), dt)
    sel_spec = pl.BlockSpec((PEER_HEADS, PEER_NKEYS, tb), lambda i: (0, 0, i))
    return pl.pallas_call(
        _peer_select_kernel,
        out_shape=(sel(bf16), sel(f32), sel(f32), sel(bf16)),
        grid=(TP // tb,),
        in_specs=[pl.BlockSpec((tb, D_MODEL), lambda i: (i, 0)),
                  pl.BlockSpec((D_MODEL, PEER_HEADS * 2 * PEER_DKEY), lambda i: (0, 0)),
                  pl.BlockSpec((PEER_HEADS, 2, PEER_NKEYS, PEER_DKEY), lambda i: (0, 0, 0, 0))],
        out_specs=(sel_spec,) * 4,
        scratch_shapes=[pltpu.VMEM((tb, PEER_HEADS * 2 * PEER_DKEY), bf16)],
        compiler_params=_params("parallel"),
        name="peer_select",
    )(hn, wq, keys)


def _peer_kernel(hnt_ref, u0_ref, un_ref, vt_ref, rank1_ref, cnt0_ref, e0_ref, e1_ref, hs_ref, g_ref, o_ref,
                 acc_ref, a_even_ref, a_odd_ref, p_ref):
    j = pl.program_id(1)
    groups = PEER_EB // PEER_NKEYS

    @pl.when(j == 0)
    def _():
        acc_ref[...] = jnp.zeros_like(acc_ref)
        a_even_ref[...] = _dot(u0_ref[...], hnt_ref[...])

    def step(a_cur_ref, a_next_ref):
        a_next_ref[...] = _dot(un_ref[...], hnt_ref[...])
        for gi in range(groups):
            i = j * groups + gi
            cnts = [jnp.broadcast_to(cnt0_ref[h, pl.ds(i, 1), :], (BF16_ROWS, PEER_TB)).astype(bf16)
                    for h in range(PEER_HEADS)]
            e0s = [jnp.broadcast_to(e0_ref[h, pl.ds(i, 1), :], (BF16_ROWS, PEER_TB)).astype(bf16)
                   for h in range(PEER_HEADS)]
            for r in range(PEER_NKEYS // BF16_ROWS):
                keys = slice(r * BF16_ROWS, (r + 1) * BF16_ROWS)
                rows = slice(gi * PEER_NKEYS + r * BF16_ROWS, gi * PEER_NKEYS + (r + 1) * BF16_ROWS)
                a = a_cur_ref[rows, :]
                act = 0.5 * a * (1.0 + lax.erf(a * (2.0 ** -0.5)))
                w = jnp.zeros((BF16_ROWS, PEER_TB), bf16)
                for h in range(PEER_HEADS):
                    w = w + jnp.where(rank1_ref[h, keys, :] < cnts[h], e1_ref[h, keys, :], jnp.zeros((), bf16)) * e0s[h]
                p_ref[rows, :] = w * act.astype(bf16)
        acc_ref[...] += _dot(vt_ref[0], p_ref[...])

    @pl.when(j % 2 == 0)
    def _():
        step(a_even_ref, a_odd_ref)

    @pl.when(j % 2 == 1)
    def _():
        step(a_odd_ref, a_even_ref)

    @pl.when(j == pl.num_programs(1) - 1)
    def _():
        hs3 = hs_ref[...] + acc_ref[...].T
        o_ref[...] = _rms(hs3, g_ref[...])


def _peer(hnt, u, vt, rank1, cnt0, e0, e1, hs2, g_final):
    tb, eb = PEER_TB, PEER_EB
    n_blocks = PEER_N // eb
    sel_spec = pl.BlockSpec((PEER_HEADS, PEER_NKEYS, tb), lambda i, j: (0, 0, i))
    once = pl.Buffered(1)
    return pl.pallas_call(
        _peer_kernel,
        out_shape=jax.ShapeDtypeStruct((TP, D_MODEL), f32),
        grid=(TP // tb, n_blocks),
        in_specs=[pl.BlockSpec((D_MODEL, tb), lambda i, j: (0, i)),
                  pl.BlockSpec((eb, D_MODEL), lambda i, j: (0, 0), pipeline_mode=once),
                  pl.BlockSpec((eb, D_MODEL), lambda i, j: (jnp.minimum(j + 1, n_blocks - 1), 0)),
                  pl.BlockSpec((1, D_MODEL, eb), lambda i, j: (j, 0, 0)),
                  sel_spec, sel_spec, sel_spec, sel_spec,
                  pl.BlockSpec((tb, D_MODEL), lambda i, j: (i, 0), pipeline_mode=once),
                  pl.BlockSpec((1, D_MODEL), lambda i, j: (0, 0))],
        out_specs=pl.BlockSpec((tb, D_MODEL), lambda i, j: (i, 0)),
        scratch_shapes=[pltpu.VMEM((D_MODEL, tb), f32), pltpu.VMEM((eb, tb), f32), pltpu.VMEM((eb, tb), f32),
                        pltpu.VMEM((eb, tb), bf16)],
        compiler_params=_params("parallel", "arbitrary"),
        name="peer_experts",
    )(hnt, u, u, vt, rank1, cnt0, e0, e1, hs2, g_final)


def _rope_tables():
    inv = 1.0 / (ROPE_THETA ** (jnp.arange(ROPE_HALF, dtype=f32) / ROPE_HALF))
    ang = jnp.arange(LP, dtype=jnp.int32).astype(f32)[:, None] * inv[None, :]
    cos, sin = jnp.cos(ang), jnp.sin(ang)
    rest = DIFF_DH - ROPE_DIMS
    zero, zrest = jnp.zeros_like(sin), jnp.zeros((LP, rest), f32)
    return jnp.stack([
        jnp.concatenate([cos, cos, jnp.ones((LP, rest), f32)], axis=1),
        jnp.concatenate([-sin, zero, zrest], axis=1),
        jnp.concatenate([zero, sin, zrest], axis=1)])


def _split_cols(w):
    parts, start = [], 0
    for n in IN_SIZES:
        parts.append(w[:, start:start + n])
        start += n
    return parts


def kernel(x, meta_tokens, g_mix, w_in, gla_w2_fwd, gla_b_fwd, gla_w2_bwd, gla_b_bwd, gla_g_norm, diff_lq1, diff_lk1, diff_lq2, diff_lk2, diff_g_norm, w_branch_gla, w_branch_diff, w_out, g_ffn, peer_w_q, peer_sub_keys, peer_u, peer_v, g_final):
    assert w_in.shape[0] == 1, "single-layer block only"
    l = 0
    lam_init = 0.8 - 0.6 * math.exp(-0.3 * l)
    meta = jnp.broadcast_to(meta_tokens[None].astype(x.dtype), (BATCH, N_META, D_MODEL))
    pad = jnp.zeros((BATCH, LP - L_REAL, D_MODEL), x.dtype)
    hs = jnp.concatenate([meta, x, pad], axis=1).reshape(TP, D_MODEL)
    rope_tab = _rope_tables()

    gq, gk, gv, gr, glr, dq, dk, dv, za, zb = _split_cols(w_in[l])
    w_a = jnp.concatenate([gq, gk, gv, dq, dk, dv], axis=1).astype(bf16)
    w_b = jnp.concatenate([gr, za, zb], axis=1).astype(bf16)
    w_lr = jnp.pad(glr, ((0, 0), (0, LANE - 2 * GLA_LOWRANK))).astype(bf16)
    w2f = jnp.pad(gla_w2_fwd[l], ((0, LANE - GLA_LOWRANK), (0, 0))).astype(bf16)
    w2b = jnp.pad(gla_w2_bwd[l], ((GLA_LOWRANK, LANE - 2 * GLA_LOWRANK), (0, 0))).astype(bf16)

    h = _norm_rows(hs, g_mix[l][None])
    proj_a = _project(h, w_a, bf16, "in_proj_qkv")
    proj_b = _project(h, w_b, f32, "in_proj_gates")
    logf_f, logf_b = _decay(h, w_lr, w2f, w2b, gla_b_fwd[l][None], gla_b_bwd[l][None])
    o_gla = _gla(proj_a, proj_b, logf_f, logf_b, gla_g_norm[l][None])
    o_diff = _diff_attention(proj_a, rope_tab, diff_lq1[l][None], diff_lk1[l][None], diff_lq2[l][None],
                             diff_lk2[l][None], diff_g_norm[l][None], lam_init)
    hs2, hn, hnt = _merge(o_gla, o_diff, proj_b, hs, w_branch_gla[l].astype(bf16), w_branch_diff[l].astype(bf16),
                     w_out[l].astype(bf16), g_ffn[l][None])
    rank1, cnt0, e0, e1 = _peer_select(hn, peer_w_q[l].astype(bf16), peer_sub_keys[l].astype(bf16))
    v_blocks = peer_v[l].reshape(PEER_N // PEER_EB, PEER_EB, D_MODEL).transpose(0, 2, 1).astype(bf16)
    out = _peer(hnt, peer_u[l].astype(bf16), v_blocks, rank1, cnt0, e0, e1, hs2, g_final[None])
    return out.reshape(BATCH, LP, D_MODEL)[:, N_META:L_REAL]
```

```python
import functools
import math

import jax
import jax.numpy as jnp
from jax import lax
from jax.experimental import pallas as pl
from jax.experimental.pallas import tpu as pltpu

f32 = jnp.float32
bf16 = jnp.bfloat16

D_MODEL = 2048
BATCH = 4
SEQ = 2048
N_META = 16
EPS = 1e-6
L_REAL = SEQ + N_META
LANE = 128
BF16_ROWS = 16
LP = -(-L_REAL // LANE) * LANE
TP = BATCH * LP

GLA_HEADS = 4
GLA_DK = 256
GLA_DV = 512
GLA_QK = GLA_HEADS * GLA_DK
GLA_V = GLA_HEADS * GLA_DV
GLA_LOWRANK = 16
GLA_TAU = 16.0
GLA_CHUNK = 64

DIFF_HEADS = 8
DIFF_DH = 128
DIFF_DV = 256
DIFF_QK = DIFF_HEADS * 2 * DIFF_DH
DIFF_V = DIFF_HEADS * DIFF_DV
ROPE_THETA = 500000.0
ROPE_DIMS = DIFF_DH // 4
ROPE_HALF = ROPE_DIMS // 2

PEER_HEADS = 8
PEER_NKEYS = 128
PEER_N = PEER_NKEYS * PEER_NKEYS
PEER_DKEY = 128
PEER_TOPK = 16

IN_SIZES = (GLA_QK, GLA_QK, GLA_V, GLA_V, 2 * GLA_LOWRANK, DIFF_QK, DIFF_QK, DIFF_V, D_MODEL, D_MODEL)

VMEM_LIMIT = 56 * 1024 * 1024
NEG_BIG = -1e30

ROW_BLOCK = 512
COL_BLOCK = 2048
DIFF_TQ = LP // 4
SEL_TB = 256
PEER_TB = 512
PEER_EB = 512


def _params(*sem):
    return pltpu.CompilerParams(dimension_semantics=sem, vmem_limit_bytes=VMEM_LIMIT)


def _rms(x, g):
    return x * lax.rsqrt(jnp.mean(x * x, axis=-1, keepdims=True) + EPS) * g


def _dot(a, b):
    return jnp.dot(a, b, preferred_element_type=f32)


def _dot_nt(a, b):
    return lax.dot_general(a, b, (((1,), (1,)), ((), ())), preferred_element_type=f32)


def _dot_tn(a, b, precision=None):
    return lax.dot_general(a, b, (((0,), (0,)), ((), ())), preferred_element_type=f32, precision=precision)


def _norm_kernel(x_ref, g_ref, o_ref):
    o_ref[...] = _rms(x_ref[...], g_ref[...]).astype(o_ref.dtype)


def _norm_rows(x, g):
    return pl.pallas_call(
        _norm_kernel,
        out_shape=jax.ShapeDtypeStruct((TP, D_MODEL), bf16),
        grid=(TP // ROW_BLOCK,),
        in_specs=[pl.BlockSpec((ROW_BLOCK, D_MODEL), lambda i: (i, 0)),
                  pl.BlockSpec((1, D_MODEL), lambda i: (0, 0))],
        out_specs=pl.BlockSpec((ROW_BLOCK, D_MODEL), lambda i: (i, 0)),
        compiler_params=_params("parallel"),
        name="mix_norm",
    )(x, g)


def _mm_kernel(a_ref, w_ref, o_ref):
    o_ref[...] = _dot(a_ref[...], w_ref[...]).astype(o_ref.dtype)


def _project(h, w, out_dtype, name):
    n = w.shape[1]
    return pl.pallas_call(
        _mm_kernel,
        out_shape=jax.ShapeDtypeStruct((TP, n), out_dtype),
        grid=(n // COL_BLOCK, TP // ROW_BLOCK),
        in_specs=[pl.BlockSpec((ROW_BLOCK, D_MODEL), lambda j, i: (i, 0)),
                  pl.BlockSpec((D_MODEL, COL_BLOCK), lambda j, i: (0, j))],
        out_specs=pl.BlockSpec((ROW_BLOCK, COL_BLOCK), lambda j, i: (i, j)),
        compiler_params=_params("parallel", "parallel"),
        name=name,
    )(h, w)


def _decay_kernel(h_ref, wlr_ref, w2f_ref, w2b_ref, bf_ref, bb_ref, of_ref, ob_ref):
    lr = _dot(h_ref[...], wlr_ref[...]).astype(bf16)
    zf = _dot(lr, w2f_ref[...]) + bf_ref[...]
    zb = _dot(lr, w2b_ref[...]) + bb_ref[...]
    of_ref[...] = jax.nn.log_sigmoid(zf) * (1.0 / GLA_TAU)
    ob_ref[...] = jax.nn.log_sigmoid(zb) * (1.0 / GLA_TAU)


def _decay(h, wlr, w2f, w2b, b_f, b_b):
    row = lambda i: (i, 0)
    fixed = lambda i: (0, 0)
    return pl.pallas_call(
        _decay_kernel,
        out_shape=(jax.ShapeDtypeStruct((TP, GLA_QK), f32),) * 2,
        grid=(TP // ROW_BLOCK,),
        in_specs=[pl.BlockSpec((ROW_BLOCK, D_MODEL), row),
                  pl.BlockSpec((D_MODEL, LANE), fixed),
                  pl.BlockSpec((LANE, GLA_QK), fixed),
                  pl.BlockSpec((LANE, GLA_QK), fixed),
                  pl.BlockSpec((1, GLA_QK), fixed),
                  pl.BlockSpec((1, GLA_QK), fixed)],
        out_specs=(pl.BlockSpec((ROW_BLOCK, GLA_QK), row),) * 2,
        compiler_params=_params("parallel"),
        name="gla_decay",
    )(h, wlr, w2f, w2b, b_f, b_b)


def _gla_kernel(q_ref, k_ref, v_ref, ff_ref, fb_ref, r_ref, g_ref, o_ref, accf_ref, accb_ref, sf_ref, sb_ref):
    C = GLA_CHUNK
    n_chunks = LP // C
    row = lax.broadcasted_iota(jnp.int32, (C, C), 0)
    col = lax.broadcasted_iota(jnp.int32, (C, C), 1)
    ones = jnp.ones((C, LANE), f32)
    hi = lax.Precision.HIGHEST

    def chunk(n, f_ref, cum_mask, keep_mask, total_row, acc_ref, s_ref):
        r0 = pl.multiple_of(n * C, C)
        rows = pl.ds(r0, C)
        lf = f_ref[rows, :]
        cum = jnp.dot(cum_mask.astype(f32), lf, precision=hi, preferred_element_type=f32)
        tot = cum[total_row:total_row + 1, :]
        tot_col = _dot_tn(lf, ones, precision=hi)
        q = q_ref[rows, :].astype(f32) * (GLA_DK ** -0.5)
        k = k_ref[rows, :].astype(f32)
        v = v_ref[rows, :]
        q_in = (q * jnp.exp(cum)).astype(bf16)
        k_in = (k * jnp.exp(-cum)).astype(bf16)
        k_st = (k * jnp.exp(tot - cum)).astype(bf16)
        a = jnp.where(keep_mask, _dot_nt(q_in, k_in), 0.0)
        acc_ref[rows, :] = _dot(a.astype(bf16), v) + _dot(q_in, s_ref[...].astype(bf16))
        dec = jnp.exp(tot_col)
        dec = jnp.concatenate([dec] * (GLA_DV // LANE), axis=1)
        s_ref[...] = s_ref[...] * dec + _dot_tn(k_st, v)

    sf_ref[...] = jnp.zeros_like(sf_ref)
    sb_ref[...] = jnp.zeros_like(sb_ref)

    def both(n, carry):
        chunk(n, ff_ref, col <= row, col <= row, C - 1, accf_ref, sf_ref)
        chunk(n_chunks - 1 - n, fb_ref, col >= row, col > row, 0, accb_ref, sb_ref)
        return carry

    lax.fori_loop(0, n_chunks, both, 0)

    def fin(n, carry):
        rows = pl.ds(pl.multiple_of(n * LANE, LANE), LANE)
        o = _rms(accf_ref[rows, :] + accb_ref[rows, :], g_ref[...])
        o_ref[rows, :] = (o * jax.nn.silu(r_ref[rows, :])).astype(o_ref.dtype)
        return carry

    lax.fori_loop(0, LP // LANE, fin, 0)


def _gla(proj_a, proj_b, logf_f, logf_b, g_norm):
    kq = GLA_QK // GLA_DK
    kv = (2 * GLA_QK) // GLA_DV
    return pl.pallas_call(
        _gla_kernel,
        out_shape=jax.ShapeDtypeStruct((TP, GLA_V), bf16),
        grid=(BATCH, GLA_HEADS),
        in_specs=[pl.BlockSpec((LP, GLA_DK), lambda b, h: (b, h)),
                  pl.BlockSpec((LP, GLA_DK), lambda b, h: (b, kq + h)),
                  pl.BlockSpec((LP, GLA_DV), lambda b, h: (b, kv + h)),
                  pl.BlockSpec((LP, GLA_DK), lambda b, h: (b, h)),
                  pl.BlockSpec((LP, GLA_DK), lambda b, h: (b, h)),
                  pl.BlockSpec((LP, GLA_DV), lambda b, h: (b, h)),
                  pl.BlockSpec((1, GLA_DV), lambda b, h: (0, 0))],
        out_specs=pl.BlockSpec((LP, GLA_DV), lambda b, h: (b, h)),
        scratch_shapes=[pltpu.VMEM((LP, GLA_DV), f32), pltpu.VMEM((LP, GLA_DV), f32),
                        pltpu.VMEM((GLA_DK, GLA_DV), f32), pltpu.VMEM((GLA_DK, GLA_DV), f32)],
        compiler_params=_params("parallel", "parallel"),
        name="gla_mixer",
    )(proj_a, proj_a, proj_a, logf_f, logf_b, proj_b, g_norm)


def _rope(x, tab):
    return (x * tab[0]
            + pltpu.roll(x, LANE - ROPE_HALF, axis=1) * tab[1]
            + pltpu.roll(x, ROPE_HALF, axis=1) * tab[2])


def _diff_kernel(lam_init, q_ref, k_ref, v_ref, tq_ref, tk_ref, lq1_ref, lk1_ref, lq2_ref, lk2_ref,
                 g_ref, o_ref, kr_ref):
    @pl.when(pl.program_id(2) == 0)
    def _():
        tk = tk_ref[...]
        for m in range(2):
            kx = k_ref[:, m * DIFF_DH:(m + 1) * DIFF_DH].astype(f32)
            kr_ref[m] = _rope(kx, tk).astype(bf16)

    lam = (jnp.exp(jnp.sum(lq1_ref[...] * lk1_ref[...], axis=1, keepdims=True))
           - jnp.exp(jnp.sum(lq2_ref[...] * lk2_ref[...], axis=1, keepdims=True)) + lam_init)
    tq = tq_ref[...]
    key_ok = lax.broadcasted_iota(jnp.int32, (DIFF_TQ, LP), 1) < L_REAL
    exp2_scale = (DIFF_DH ** -0.5) * math.log2(math.e)
    outs = []
    for m in range(2):
        qx = q_ref[:, m * DIFF_DH:(m + 1) * DIFF_DH].astype(f32)
        qr = _rope(qx, tq).astype(bf16)
        s = jnp.where(key_ok, _dot_nt(qr, kr_ref[m]), NEG_BIG)
        e = jnp.exp2((s - jnp.max(s, axis=1, keepdims=True)) * exp2_scale)
        inv = 1.0 / jnp.sum(e, axis=1, keepdims=True)
        outs.append(_dot(e.astype(bf16), v_ref[...]) * inv)
    o = outs[0] - lam * outs[1]
    o_ref[...] = (_rms(o, g_ref[...]) * (1.0 - lam_init)).astype(o_ref.dtype)


def _diff_attention(proj_a, rope_tab, lq1, lk1, lq2, lk2, g_norm, lam_init):
    nq = LP // DIFF_TQ
    cq = (2 * GLA_QK + GLA_V) // DIFF_DV
    ck = cq + DIFF_HEADS
    cv = ck + DIFF_HEADS
    vec = pl.BlockSpec((1, DIFF_DH), lambda b, h, i: (0, 0))
    return pl.pallas_call(
        functools.partial(_diff_kernel, lam_init),
        out_shape=jax.ShapeDtypeStruct((TP, DIFF_V), bf16),
        grid=(BATCH, DIFF_HEADS, nq),
        in_specs=[pl.BlockSpec((DIFF_TQ, DIFF_DV), lambda b, h, i: (b * nq + i, cq + h)),
                  pl.BlockSpec((LP, DIFF_DV), lambda b, h, i: (b, ck + h)),
                  pl.BlockSpec((LP, DIFF_DV), lambda b, h, i: (b, cv + h)),
                  pl.BlockSpec((3, DIFF_TQ, DIFF_DH), lambda b, h, i: (0, i, 0)),
                  pl.BlockSpec((3, LP, DIFF_DH), lambda b, h, i: (0, 0, 0)),
                  vec, vec, vec, vec,
                  pl.BlockSpec((1, DIFF_DV), lambda b, h, i: (0, 0))],
        out_specs=pl.BlockSpec((DIFF_TQ, DIFF_DV), lambda b, h, i: (b * nq + i, h)),
        scratch_shapes=[pltpu.VMEM((2, LP, DIFF_DH), bf16)],
        compiler_params=_params("parallel", "parallel", "arbitrary"),
        name="diff_attention",
    )(proj_a, proj_a, proj_a, rope_tab, rope_tab, lq1, lk1, lq2, lk2, g_norm)


def _merge_kernel(og_ref, od_ref, za_ref, zb_ref, hs_ref, wa_ref, wb_ref, wo_ref, g_ref, hs2_ref, hn_ref, hnt_ref):
    y = (jax.nn.sigmoid(za_ref[...]) * _dot(og_ref[...], wa_ref[...])
         + jax.nn.sigmoid(zb_ref[...]) * _dot(od_ref[...], wb_ref[...]))
    hs2 = hs_ref[...] + _dot(y.astype(bf16), wo_ref[...])
    hs2_ref[...] = hs2
    hn = _rms(hs2, g_ref[...])
    hn_ref[...] = hn.astype(hn_ref.dtype)
    hnt_ref[...] = hn.T.astype(hnt_ref.dtype)


def _merge(o_gla, o_diff, proj_b, hs, wa, wb, wo, g_ffn):
    tm = 256
    row = lambda i: (i, 0)
    fixed = lambda i: (0, 0)
    wspec = pl.BlockSpec((D_MODEL, D_MODEL), fixed, pipeline_mode=pl.Buffered(1))
    return pl.pallas_call(
        _merge_kernel,
        out_shape=(jax.ShapeDtypeStruct((TP, D_MODEL), f32), jax.ShapeDtypeStruct((TP, D_MODEL), bf16),
                   jax.ShapeDtypeStruct((D_MODEL, TP), bf16)),
        grid=(TP // tm,),
        in_specs=[pl.BlockSpec((tm, GLA_V), row),
                  pl.BlockSpec((tm, DIFF_V), row),
                  pl.BlockSpec((tm, D_MODEL), lambda i: (i, 1)),
                  pl.BlockSpec((tm, D_MODEL), lambda i: (i, 2)),
                  pl.BlockSpec((tm, D_MODEL), row),
                  wspec, wspec, wspec,
                  pl.BlockSpec((1, D_MODEL), fixed)],
        out_specs=(pl.BlockSpec((tm, D_MODEL), row), pl.BlockSpec((tm, D_MODEL), row),
                   pl.BlockSpec((D_MODEL, tm), lambda i: (0, i))),
        compiler_params=_params("parallel"),
        name="branch_merge",
    )(o_gla, o_diff, proj_b, proj_b, hs, wa, wb, wo, g_ffn)


def _top16(s, iota):
    rank = jnp.full(s.shape, float(PEER_TOPK), f32)
    vals = []
    for r in range(PEER_TOPK):
        m = jnp.max(s, axis=0, keepdims=True)
        first = jnp.min(jnp.where(s == m, iota, float(PEER_NKEYS)), axis=0, keepdims=True)
        hit = iota == first
        rank = jnp.where(hit, float(r), rank)
        s = jnp.where(hit, -jnp.inf, s)
        vals.append(m)
    return vals, rank


def _peer_select_kernel(hn_ref, wq_ref, keys_ref, rank1_ref, cnt0_ref, e0_ref, e1_ref, q_sc):
    tb = SEL_TB
    K = PEER_TOPK
    q_sc[...] = _dot(hn_ref[...], wq_ref[...]).astype(bf16)
    iota = lax.broadcasted_iota(jnp.int32, (PEER_NKEYS, tb), 0).astype(f32)
    i16 = lax.broadcasted_iota(jnp.int32, (K, tb), 0).astype(f32)
    i8 = lax.broadcasted_iota(jnp.int32, (8, tb), 0).astype(f32)
    ids = jnp.concatenate([i16] + [a * float(K) + i8 for a in range(1, 8)] + [(i8 + 8.0) * float(K)], axis=0)
    n_cand = ids.shape[0]

    def head(h, carry):
        c0 = pl.multiple_of(h * (2 * PEER_DKEY), 2 * PEER_DKEY)
        s0 = _dot_nt(keys_ref[h, 0], q_sc[:, pl.ds(c0, PEER_DKEY)])
        s1 = _dot_nt(keys_ref[h, 1], q_sc[:, pl.ds(c0 + PEER_DKEY, PEER_DKEY)])
        v0, rank0 = _top16(s0, iota)
        v1, rank1 = _top16(s1, iota)
        sa = jnp.concatenate(v0, axis=0)
        sb = jnp.concatenate(v1, axis=0)
        cand = jnp.concatenate([sa[0:1] + sb]
                               + [sa[a:a + 1] + sb[0:8] for a in range(1, 8)]
                               + [sa[8:16] + sb[0:1]], axis=0)
        top = cand[0:1]
        taken = jnp.zeros((n_cand, tb), f32)
        z = jnp.zeros((1, tb), f32)
        for _ in range(K):
            m = jnp.max(cand, axis=0, keepdims=True)
            first = jnp.min(jnp.where(cand == m, ids, 1e9), axis=0, keepdims=True)
            hit = ids == first
            taken = jnp.where(hit, 1.0, taken)
            cand = jnp.where(hit, -jnp.inf, cand)
            z = z + jnp.exp(m - top)
        cnt = ([jnp.sum(taken[0:16], axis=0, keepdims=True)]
               + [jnp.sum(taken[8 + 8 * a:16 + 8 * a], axis=0, keepdims=True) for a in range(1, 8)]
               + [taken[72 + a:73 + a] for a in range(8)])
        cnt0 = jnp.zeros((PEER_NKEYS, tb), f32)
        for a in range(K):
            cnt0 = jnp.where(rank0 == float(a), cnt[a], cnt0)
        rank1_ref[h] = rank1.astype(bf16)
        cnt0_ref[h] = cnt0
        e0_ref[h] = jnp.exp(s0 - v0[0]) * (1.0 / z)
        e1_ref[h] = jnp.exp(s1 - v1[0]).astype(bf16)
        return carry

    lax.fori_loop(0, PEER_HEADS, head, 0)


def _peer_select(hn, wq, keys):
    tb = SEL_TB
    sel = lambda dt: jax.ShapeDtypeStruct((PEER_HEADS, PEER_NKEYS, TP), dt)
    sel_spec = pl.BlockSpec((PEER_HEADS, PEER_NKEYS, tb), lambda i: (0, 0, i))
    return pl.pallas_call(
        _peer_select_kernel,
        out_shape=(sel(bf16), sel(f32), sel(f32), sel(bf16)),
        grid=(TP // tb,),
        in_specs=[pl.BlockSpec((tb, D_MODEL), lambda i: (i, 0)),
                  pl.BlockSpec((D_MODEL, PEER_HEADS * 2 * PEER_DKEY), lambda i: (0, 0)),
                  pl.BlockSpec((PEER_HEADS, 2, PEER_NKEYS, PEER_DKEY), lambda i: (0, 0, 0, 0))],
        out_specs=(sel_spec,) * 4,
        scratch_shapes=[pltpu.VMEM((tb, PEER_HEADS * 2 * PEER_DKEY), bf16)],
        compiler_params=_params("parallel"),
        name="peer_select",
    )(hn, wq, keys)


def _peer_kernel(hnt_ref, u0_ref, un_ref, vt_ref, rank1_ref, cnt0_ref, e0_ref, e1_ref, hs_ref, g_ref, o_ref,
                 acc_ref, a_even_ref, a_odd_ref, p_ref):
    j = pl.program_id(1)
    groups = PEER_EB // PEER_NKEYS

    @pl.when(j == 0)
    def _():
        acc_ref[...] = jnp.zeros_like(acc_ref)
        a_even_ref[...] = _dot(u0_ref[...], hnt_ref[...])

    def step(a_cur_ref, a_next_ref):
        a_next_ref[...] = _dot(un_ref[...], hnt_ref[...])
        for gi in range(groups):
            i = j * groups + gi
            cnts = [jnp.broadcast_to(cnt0_ref[h, pl.ds(i, 1), :], (BF16_ROWS, PEER_TB)).astype(bf16)
                    for h in range(PEER_HEADS)]
            e0s = [jnp.broadcast_to(e0_ref[h, pl.ds(i, 1), :], (BF16_ROWS, PEER_TB)).astype(bf16)
                   for h in range(PEER_HEADS)]
            for r in range(PEER_NKEYS // BF16_ROWS):
                keys = slice(r * BF16_ROWS, (r + 1) * BF16_ROWS)
                rows = slice(gi * PEER_NKEYS + r * BF16_ROWS, gi * PEER_NKEYS + (r + 1) * BF16_ROWS)
                a = a_cur_ref[rows, :]
                act = 0.5 * a * (1.0 + lax.erf(a * (2.0 ** -0.5)))
                w = jnp.zeros((BF16_ROWS, PEER_TB), bf16)
                for h in range(PEER_HEADS):
                    w = w + jnp.where(rank1_ref[h, keys, :] < cnts[h], e1_ref[h, keys, :], jnp.zeros((), bf16)) * e0s[h]
                p_ref[rows, :] = w * act.astype(bf16)
        acc_ref[...] += _dot(vt_ref[0], p_ref[...])

    @pl.when(j % 2 == 0)
    def _():
        step(a_even_ref, a_odd_ref)

    @pl.when(j % 2 == 1)
    def _():
        step(a_odd_ref, a_even_ref)

    @pl.when(j == pl.num_programs(1) - 1)
    def _():
        hs3 = hs_ref[...] + acc_ref[...].T
        o_ref[...] = _rms(hs3, g_ref[...])


def _peer(hnt, u, vt, rank1, cnt0, e0, e1, hs2, g_final):
    tb, eb = PEER_TB, PEER_EB
    n_blocks = PEER_N // eb
    sel_spec = pl.BlockSpec((PEER_HEADS, PEER_NKEYS, tb), lambda i, j: (0, 0, i))
    once = pl.Buffered(1)
    return pl.pallas_call(
        _peer_kernel,
        out_shape=jax.ShapeDtypeStruct((TP, D_MODEL), f32),
        grid=(TP // tb, n_blocks),
        in_specs=[pl.BlockSpec((D_MODEL, tb), lambda i, j: (0, i)),
                  pl.BlockSpec((eb, D_MODEL), lambda i, j: (0, 0), pipeline_mode=once),
                  pl.BlockSpec((eb, D_MODEL), lambda i, j: (jnp.minimum(j + 1, n_blocks - 1), 0)),
                  pl.BlockSpec((1, D_MODEL, eb), lambda i, j: (j, 0, 0)),
                  sel_spec, sel_spec, sel_spec, sel_spec,
                  pl.BlockSpec((tb, D_MODEL), lambda i, j: (i, 0), pipeline_mode=once),
                  pl.BlockSpec((1, D_MODEL), lambda i, j: (0, 0))],
        out_specs=pl.BlockSpec((tb, D_MODEL), lambda i, j: (i, 0)),
        scratch_shapes=[pltpu.VMEM((D_MODEL, tb), f32), pltpu.VMEM((eb, tb), f32), pltpu.VMEM((eb, tb), f32),
                        pltpu.VMEM((eb, tb), bf16)],
        compiler_params=_params("parallel", "arbitrary"),
        name="peer_experts",
    )(hnt, u, u, vt, rank1, cnt0, e0, e1, hs2, g_final)


def _rope_tables():
    inv = 1.0 / (ROPE_THETA ** (jnp.arange(ROPE_HALF, dtype=f32) / ROPE_HALF))
    ang = jnp.arange(LP, dtype=jnp.int32).astype(f32)[:, None] * inv[None, :]
    cos, sin = jnp.cos(ang), jnp.sin(ang)
    rest = DIFF_DH - ROPE_DIMS
    zero, zrest = jnp.zeros_like(sin), jnp.zeros((LP, rest), f32)
    return jnp.stack([
        jnp.concatenate([cos, cos, jnp.ones((LP, rest), f32)], axis=1),
        jnp.concatenate([-sin, zero, zrest], axis=1),
        jnp.concatenate([zero, sin, zrest], axis=1)])


def _split_cols(w):
    parts, start = [], 0
    for n in IN_SIZES:
        parts.append(w[:, start:start + n])
        start += n
    return parts


def kernel(x, meta_tokens, g_mix, w_in, gla_w2_fwd, gla_b_fwd, gla_w2_bwd, gla_b_bwd, gla_g_norm, diff_lq1, diff_lk1, diff_lq2, diff_lk2, diff_g_norm, w_branch_gla, w_branch_diff, w_out, g_ffn, peer_w_q, peer_sub_keys, peer_u, peer_v, g_final):
    assert w_in.shape[0] == 1, "single-layer block only"
    l = 0
    lam_init = 0.8 - 0.6 * math.exp(-0.3 * l)
    meta = jnp.broadcast_to(meta_tokens[None].astype(x.dtype), (BATCH, N_META, D_MODEL))
    pad = jnp.zeros((BATCH, LP - L_REAL, D_MODEL), x.dtype)
    hs = jnp.concatenate([meta, x, pad], axis=1).reshape(TP, D_MODEL)
    rope_tab = _rope_tables()

    gq, gk, gv, gr, glr, dq, dk, dv, za, zb = _split_cols(w_in[l])
    w_a = jnp.concatenate([gq, gk, gv, dq, dk, dv], axis=1).astype(bf16)
    w_b = jnp.concatenate([gr, za, zb], axis=1).astype(bf16)
    w_lr = jnp.pad(glr, ((0, 0), (0, LANE - 2 * GLA_LOWRANK))).astype(bf16)
    w2f = jnp.pad(gla_w2_fwd[l], ((0, LANE - GLA_LOWRANK), (0, 0))).astype(bf16)
    w2b = jnp.pad(gla_w2_bwd[l], ((GLA_LOWRANK, LANE - 2 * GLA_LOWRANK), (0, 0))).astype(bf16)

    h = _norm_rows(hs, g_mix[l][None])
    proj_a = _project(h, w_a, bf16, "in_proj_qkv")
    proj_b = _project(h, w_b, f32, "in_proj_gates")
    logf_f, logf_b = _decay(h, w_lr, w2f, w2b, gla_b_fwd[l][None], gla_b_bwd[l][None])
    o_gla = _gla(proj_a, proj_b, logf_f, logf_b, gla_g_norm[l][None])
    o_diff = _diff_attention(proj_a, rope_tab, diff_lq1[l][None], diff_lk1[l][None], diff_lq2[l][None],
                             diff_lk2[l][None], diff_g_norm[l][None], lam_init)
    hs2, hn, hnt = _merge(o_gla, o_diff, proj_b, hs, w_branch_gla[l].astype(bf16), w_branch_diff[l].astype(bf16),
                     w_out[l].astype(bf16), g_ffn[l][None])
    rank1, cnt0, e0, e1 = _peer_select(hn, peer_w_q[l].astype(bf16), peer_sub_keys[l].astype(bf16))
    v_blocks = peer_v[l].reshape(PEER_N // PEER_EB, PEER_EB, D_MODEL).transpose(0, 2, 1).astype(bf16)
    out = _peer(hnt, peer_u[l].astype(bf16), v_blocks, rank1, cnt0, e0, e1, hs2, g_final[None])
    return out.reshape(BATCH, LP, D_MODEL)[:, N_META:L_REAL]
```

```python
import functools
import math

import jax
import jax.numpy as jnp
from jax import lax
from jax.experimental import pallas as pl
from jax.experimental.pallas import tpu as pltpu

f32 = jnp.float32
bf16 = jnp.bfloat16

D_MODEL = 2048
BATCH = 4
SEQ = 2048
N_META = 16
EPS = 1e-6
L_REAL = SEQ + N_META
LANE = 128
BF16_ROWS = 16
LP = -(-L_REAL // LANE) * LANE
TP = BATCH * LP

GLA_HEADS = 4
GLA_DK = 256
GLA_DV = 512
GLA_QK = GLA_HEADS * GLA_DK
GLA_V = GLA_HEADS * GLA_DV
GLA_LOWRANK = 16
GLA_TAU = 16.0
GLA_CHUNK = 64

DIFF_HEADS = 8
DIFF_DH = 128
DIFF_DV = 256
DIFF_QK = DIFF_HEADS * 2 * DIFF_DH
DIFF_V = DIFF_HEADS * DIFF_DV
ROPE_THETA = 500000.0
ROPE_DIMS = DIFF_DH // 4
ROPE_HALF = ROPE_DIMS // 2

PEER_HEADS = 8
PEER_NKEYS = 128
PEER_N = PEER_NKEYS * PEER_NKEYS
PEER_DKEY = 128
PEER_TOPK = 16

IN_SIZES = (GLA_QK, GLA_QK, GLA_V, GLA_V, 2 * GLA_LOWRANK, DIFF_QK, DIFF_QK, DIFF_V, D_MODEL, D_MODEL)

VMEM_LIMIT = 56 * 1024 * 1024
NEG_BIG = -1e30

ROW_BLOCK = 512
COL_BLOCK = 2048
DIFF_TQ = LP // 4
SEL_TB = 256
PEER_TB = 512
PEER_EB = 512


def _params(*sem):
    return pltpu.CompilerParams(dimension_semantics=sem, vmem_limit_bytes=VMEM_LIMIT)


def _rms(x, g):
    return x * lax.rsqrt(jnp.mean(x * x, axis=-1, keepdims=True) + EPS) * g


def _dot(a, b):
    return jnp.dot(a, b, preferred_element_type=f32)


def _dot_nt(a, b):
    return lax.dot_general(a, b, (((1,), (1,)), ((), ())), preferred_element_type=f32)


def _dot_tn(a, b, precision=None):
    return lax.dot_general(a, b, (((0,), (0,)), ((), ())), preferred_element_type=f32, precision=precision)


def _norm_kernel(x_ref, g_ref, o_ref):
    o_ref[...] = _rms(x_ref[...], g_ref[...]).astype(o_ref.dtype)


def _norm_rows(x, g):
    return pl.pallas_call(
        _norm_kernel,
        out_shape=jax.ShapeDtypeStruct((TP, D_MODEL), bf16),
        grid=(TP // ROW_BLOCK,),
        in_specs=[pl.BlockSpec((ROW_BLOCK, D_MODEL), lambda i: (i, 0)),
                  pl.BlockSpec((1, D_MODEL), lambda i: (0, 0))],
        out_specs=pl.BlockSpec((ROW_BLOCK, D_MODEL), lambda i: (i, 0)),
        compiler_params=_params("parallel"),
        name="mix_norm",
    )(x, g)


def _mm_kernel(a_ref, w_ref, o_ref):
    o_ref[...] = _dot(a_ref[...], w_ref[...]).astype(o_ref.dtype)


def _project(h, w, out_dtype, name):
    n = w.shape[1]
    return pl.pallas_call(
        _mm_kernel,
        out_shape=jax.ShapeDtypeStruct((TP, n), out_dtype),
        grid=(n // COL_BLOCK, TP // ROW_BLOCK),
        in_specs=[pl.BlockSpec((ROW_BLOCK, D_MODEL), lambda j, i: (i, 0)),
                  pl.BlockSpec((D_MODEL, COL_BLOCK), lambda j, i: (0, j))],
        out_specs=pl.BlockSpec((ROW_BLOCK, COL_BLOCK), lambda j, i: (i, j)),
        compiler_params=_params("parallel", "parallel"),
        name=name,
    )(h, w)


def _decay_kernel(h_ref, wlr_ref, w2f_ref, w2b_ref, bf_ref, bb_ref, of_ref, ob_ref):
    lr = _dot(h_ref[...], wlr_ref[...]).astype(bf16)
    zf = _dot(lr, w2f_ref[...]) + bf_ref[...]
    zb = _dot(lr, w2b_ref[...]) + bb_ref[...]
    of_ref[...] = jax.nn.log_sigmoid(zf) * (1.0 / GLA_TAU)
    ob_ref[...] = jax.nn.log_sigmoid(zb) * (1.0 / GLA_TAU)


def _decay(h, wlr, w2f, w2b, b_f, b_b):
    row = lambda i: (i, 0)
    fixed = lambda i: (0, 0)
    return pl.pallas_call(
        _decay_kernel,
        out_shape=(jax.ShapeDtypeStruct((TP, GLA_QK), f32),) * 2,
        grid=(TP // ROW_BLOCK,),
        in_specs=[pl.BlockSpec((ROW_BLOCK, D_MODEL), row),
                  pl.BlockSpec((D_MODEL, LANE), fixed),
                  pl.BlockSpec((LANE, GLA_QK), fixed),
                  pl.BlockSpec((LANE, GLA_QK), fixed),
                  pl.BlockSpec((1, GLA_QK), fixed),
                  pl.BlockSpec((1, GLA_QK), fixed)],
        out_specs=(pl.BlockSpec((ROW_BLOCK, GLA_QK), row),) * 2,
        compiler_params=_params("parallel"),
        name="gla_decay",
    )(h, wlr, w2f, w2b, b_f, b_b)


def _gla_kernel(q_ref, k_ref, v_ref, ff_ref, fb_ref, r_ref, g_ref, o_ref, accf_ref, accb_ref, sf_ref, sb_ref):
    C = GLA_CHUNK
    n_chunks = LP // C
    row = lax.broadcasted_iota(jnp.int32, (C, C), 0)
    col = lax.broadcasted_iota(jnp.int32, (C, C), 1)
    ones = jnp.ones((C, LANE), f32)
    hi = lax.Precision.HIGHEST

    def chunk(n, f_ref, cum_mask, keep_mask, total_row, acc_ref, s_ref):
        r0 = pl.multiple_of(n * C, C)
        rows = pl.ds(r0, C)
        lf = f_ref[rows, :]
        cum = jnp.dot(cum_mask.astype(f32), lf, precision=hi, preferred_element_type=f32)
        tot = cum[total_row:total_row + 1, :]
        tot_col = _dot_tn(lf, ones, precision=hi)
        q = q_ref[rows, :].astype(f32) * (GLA_DK ** -0.5)
        k = k_ref[rows, :].astype(f32)
        v = v_ref[rows, :]
        q_in = (q * jnp.exp(cum)).astype(bf16)
        k_in = (k * jnp.exp(-cum)).astype(bf16)
        k_st = (k * jnp.exp(tot - cum)).astype(bf16)
        a = jnp.where(keep_mask, _dot_nt(q_in, k_in), 0.0)
        acc_ref[rows, :] = _dot(a.astype(bf16), v) + _dot(q_in, s_ref[...].astype(bf16))
        dec = jnp.exp(tot_col)
        dec = jnp.concatenate([dec] * (GLA_DV // LANE), axis=1)
        s_ref[...] = s_ref[...] * dec + _dot_tn(k_st, v)

    sf_ref[...] = jnp.zeros_like(sf_ref)
    sb_ref[...] = jnp.zeros_like(sb_ref)

    def both(n, carry):
        chunk(n, ff_ref, col <= row, col <= row, C - 1, accf_ref, sf_ref)
        chunk(n_chunks - 1 - n, fb_ref, col >= row, col > row, 0, accb_ref, sb_ref)
        return carry

    lax.fori_loop(0, n_chunks, both, 0)

    def fin(n, carry):
        rows = pl.ds(pl.multiple_of(n * LANE, LANE), LANE)
        o = _rms(accf_ref[rows, :] + accb_ref[rows, :], g_ref[...])
        o_ref[rows, :] = (o * jax.nn.silu(r_ref[rows, :])).astype(o_ref.dtype)
        return carry

    lax.fori_loop(0, LP // LANE, fin, 0)


def _gla(proj_a, proj_b, logf_f, logf_b, g_norm):
    kq = GLA_QK // GLA_DK
    kv = (2 * GLA_QK) // GLA_DV
    return pl.pallas_call(
        _gla_kernel,
        out_shape=jax.ShapeDtypeStruct((TP, GLA_V), bf16),
        grid=(BATCH, GLA_HEADS),
        in_specs=[pl.BlockSpec((LP, GLA_DK), lambda b, h: (b, h)),
                  pl.BlockSpec((LP, GLA_DK), lambda b, h: (b, kq + h)),
                  pl.BlockSpec((LP, GLA_DV), lambda b, h: (b, kv + h)),
                  pl.BlockSpec((LP, GLA_DK), lambda b, h: (b, h)),
                  pl.BlockSpec((LP, GLA_DK), lambda b, h: (b, h)),
                  pl.BlockSpec((LP, GLA_DV), lambda b, h: (b, h)),
                  pl.BlockSpec((1, GLA_DV), lambda b, h: (0, 0))],
        out_specs=pl.BlockSpec((LP, GLA_DV), lambda b, h: (b, h)),
        scratch_shapes=[pltpu.VMEM((LP, GLA_DV), f32), pltpu.VMEM((LP, GLA_DV), f32),
                        pltpu.VMEM((GLA_DK, GLA_DV), f32), pltpu.VMEM((GLA_DK, GLA_DV), f32)],
        compiler_params=_params("parallel", "parallel"),
        name="gla_mixer",
    )(proj_a, proj_a, proj_a, logf_f, logf_b, proj_b, g_norm)


def _rope(x, tab):
    return (x * tab[0]
            + pltpu.roll(x, LANE - ROPE_HALF, axis=1) * tab[1]
            + pltpu.roll(x, ROPE_HALF, axis=1) * tab[2])


def _diff_kernel(lam_init, q_ref, k_ref, v_ref, tq_ref, tk_ref, lq1_ref, lk1_ref, lq2_ref, lk2_ref,
                 g_ref, o_ref, kr_ref):
    @pl.when(pl.program_id(2) == 0)
    def _():
        tk = tk_ref[...]
        for m in range(2):
            kx = k_ref[:, m * DIFF_DH:(m + 1) * DIFF_DH].astype(f32)
            kr_ref[m] = _rope(kx, tk).astype(bf16)

    lam = (jnp.exp(jnp.sum(lq1_ref[...] * lk1_ref[...], axis=1, keepdims=True))
           - jnp.exp(jnp.sum(lq2_ref[...] * lk2_ref[...], axis=1, keepdims=True)) + lam_init)
    tq = tq_ref[...]
    key_ok = lax.broadcasted_iota(jnp.int32, (DIFF_TQ, LP), 1) < L_REAL
    exp2_scale = (DIFF_DH ** -0.5) * math.log2(math.e)
    outs = []
    for m in range(2):
        qx = q_ref[:, m * DIFF_DH:(m + 1) * DIFF_DH].astype(f32)
        qr = _rope(qx, tq).astype(bf16)
        s = jnp.where(key_ok, _dot_nt(qr, kr_ref[m]), NEG_BIG)
        e = jnp.exp2((s - jnp.max(s, axis=1, keepdims=True)) * exp2_scale)
        inv = 1.0 / jnp.sum(e, axis=1, keepdims=True)
        outs.append(_dot(e.astype(bf16), v_ref[...]) * inv)
    o = outs[0] - lam * outs[1]
    o_ref[...] = (_rms(o, g_ref[...]) * (1.0 - lam_init)).astype(o_ref.dtype)


def _diff_attention(proj_a, rope_tab, lq1, lk1, lq2, lk2, g_norm, lam_init):
    nq = LP // DIFF_TQ
    cq = (2 * GLA_QK + GLA_V) // DIFF_DV
    ck = cq + DIFF_HEADS
    cv = ck + DIFF_HEADS
    vec = pl.BlockSpec((1, DIFF_DH), lambda b, h, i: (0, 0))
    return pl.pallas_call(
        functools.partial(_diff_kernel, lam_init),
        out_shape=jax.ShapeDtypeStruct((TP, DIFF_V), bf16),
        grid=(BATCH, DIFF_HEADS, nq),
        in_specs=[pl.BlockSpec((DIFF_TQ, DIFF_DV), lambda b, h, i: (b * nq + i, cq + h)),
                  pl.BlockSpec((LP, DIFF_DV), lambda b, h, i: (b, ck + h)),
                  pl.BlockSpec((LP, DIFF_DV), lambda b, h, i: (b, cv + h)),
                  pl.BlockSpec((3, DIFF_TQ, DIFF_DH), lambda b, h, i: (0, i, 0)),
                  pl.BlockSpec((3, LP, DIFF_DH), lambda b, h, i: (0, 0, 0)),
                  vec, vec, vec, vec,
                  pl.BlockSpec((1, DIFF_DV), lambda b, h, i: (0, 0))],
        out_specs=pl.BlockSpec((DIFF_TQ, DIFF_DV), lambda b, h, i: (b * nq + i, h)),
        scratch_shapes=[pltpu.VMEM((2, LP, DIFF_DH), bf16)],
        compiler_params=_params("parallel", "parallel", "arbitrary"),
        name="diff_attention",
    )(proj_a, proj_a, proj_a, rope_tab, rope_tab, lq1, lk1, lq2, lk2, g_norm)


def _merge_kernel(og_ref, od_ref, za_ref, zb_ref, hs_ref, wa_ref, wb_ref, wo_ref, g_ref, hs2_ref, hn_ref, hnt_ref):
    y = (jax.nn.sigmoid(za_ref[...]) * _dot(og_ref[...], wa_ref[...])
         + jax.nn.sigmoid(zb_ref[...]) * _dot(od_ref[...], wb_ref[...]))
    hs2 = hs_ref[...] + _dot(y.astype(bf16), wo_ref[...])
    hs2_ref[...] = hs2
    hn = _rms(hs2, g_ref[...])
    hn_ref[...] = hn.astype(hn_ref.dtype)
    hnt_ref[...] = hn.T.astype(hnt_ref.dtype)


def _merge(o_gla, o_diff, proj_b, hs, wa, wb, wo, g_ffn):
    tm = 256
    row = lambda i: (i, 0)
    fixed = lambda i: (0, 0)
    wspec = pl.BlockSpec((D_MODEL, D_MODEL), fixed, pipeline_mode=pl.Buffered(1))
    return pl.pallas_call(
        _merge_kernel,
        out_shape=(jax.ShapeDtypeStruct((TP, D_MODEL), f32), jax.ShapeDtypeStruct((TP, D_MODEL), bf16),
                   jax.ShapeDtypeStruct((D_MODEL, TP), bf16)),
        grid=(TP // tm,),
        in_specs=[pl.BlockSpec((tm, GLA_V), row),
                  pl.BlockSpec((tm, DIFF_V), row),
                  pl.BlockSpec((tm, D_MODEL), lambda i: (i, 1)),
                  pl.BlockSpec((tm, D_MODEL), lambda i: (i, 2)),
                  pl.BlockSpec((tm, D_MODEL), row),
                  wspec, wspec, wspec,
                  pl.BlockSpec((1, D_MODEL), fixed)],
        out_specs=(pl.BlockSpec((tm, D_MODEL), row), pl.BlockSpec((tm, D_MODEL), row),
                   pl.BlockSpec((D_MODEL, tm), lambda i: (0, i))),
        compiler_params=_params("parallel"),
        name="branch_merge",
    )(o_gla, o_diff, proj_b, proj_b, hs, wa, wb, wo, g_ffn)


def _top16(s, iota, break_ties):
    rank = jnp.full(s.shape, float(PEER_TOPK), f32)
    vals = []
    for r in range(PEER_TOPK):
        m = jnp.max(s, axis=0, keepdims=True)
        hit = s == m
        if break_ties:
            first = jnp.min(jnp.where(hit, iota, float(PEER_NKEYS)), axis=0, keepdims=True)
            hit = iota == first
        rank = jnp.where(hit, float(r), rank)
        s = jnp.where(hit, -jnp.inf, s)
        vals.append(m)
    return vals, rank


def _select_head(s0, s1, iota, ids, break_ties):
    K = PEER_TOPK
    tb = s0.shape[1]
    v0, rank0 = _top16(s0, iota, break_ties)
    v1, rank1 = _top16(s1, iota, break_ties)
    sa = jnp.concatenate(v0, axis=0)
    sb = jnp.concatenate(v1, axis=0)
    cand = jnp.concatenate([sa[0:1] + sb]
                           + [sa[a:a + 1] + sb[0:8] for a in range(1, 8)]
                           + [sa[8:16] + sb[0:1]], axis=0)
    top = cand[0:1]
    taken = jnp.zeros(cand.shape, f32)
    z = jnp.zeros((1, tb), f32)
    for _ in range(K):
        m = jnp.max(cand, axis=0, keepdims=True)
        hit = cand == m
        if break_ties:
            first = jnp.min(jnp.where(hit, ids, 1e9), axis=0, keepdims=True)
            hit = ids == first
        taken = jnp.where(hit, 1.0, taken)
        cand = jnp.where(hit, -jnp.inf, cand)
        z = z + jnp.exp(m - top)
    cnt = ([jnp.sum(taken[0:16], axis=0, keepdims=True)]
           + [jnp.sum(taken[8 + 8 * a:16 + 8 * a], axis=0, keepdims=True) for a in range(1, 8)]
           + [taken[72 + a:73 + a] for a in range(8)])
    cnt0 = jnp.zeros((PEER_NKEYS, tb), f32)
    for a in range(K):
        cnt0 = jnp.where(rank0 == float(a), cnt[a], cnt0)
    e0 = jnp.exp(s0 - v0[0]) * (1.0 / z)
    e1 = jnp.exp(s1 - v1[0])
    marked = (jnp.sum((rank0 < float(K)).astype(f32), axis=0, keepdims=True),
              jnp.sum((rank1 < float(K)).astype(f32), axis=0, keepdims=True),
              jnp.sum(taken, axis=0, keepdims=True))
    excess = jnp.max(sum(jnp.abs(n - float(K)) for n in marked))
    return (rank1, cnt0, e0, e1), excess


def _peer_select_kernel(hn_ref, wq_ref, keys_ref, rank1_ref, cnt0_ref, e0_ref, e1_ref, q_sc):
    tb = SEL_TB
    K = PEER_TOPK
    q_sc[...] = _dot(hn_ref[...], wq_ref[...]).astype(bf16)
    iota = lax.broadcasted_iota(jnp.int32, (PEER_NKEYS, tb), 0).astype(f32)
    i16 = lax.broadcasted_iota(jnp.int32, (K, tb), 0).astype(f32)
    i8 = lax.broadcasted_iota(jnp.int32, (8, tb), 0).astype(f32)
    ids = jnp.concatenate([i16] + [a * float(K) + i8 for a in range(1, 8)] + [(i8 + 8.0) * float(K)], axis=0)

    def head(h, carry):
        c0 = pl.multiple_of(h * (2 * PEER_DKEY), 2 * PEER_DKEY)
        s0 = _dot_nt(keys_ref[h, 0], q_sc[:, pl.ds(c0, PEER_DKEY)])
        s1 = _dot_nt(keys_ref[h, 1], q_sc[:, pl.ds(c0 + PEER_DKEY, PEER_DKEY)])

        def store(rank1, cnt0, e0, e1):
            rank1_ref[h] = rank1.astype(bf16)
            cnt0_ref[h] = cnt0
            e0_ref[h] = e0
            e1_ref[h] = e1.astype(bf16)

        tables, excess = _select_head(s0, s1, iota, ids, False)
        store(*tables)

        @pl.when(excess > 0.0)
        def _():
            store(*_select_head(s0, s1, iota, ids, True)[0])

        return carry

    lax.fori_loop(0, PEER_HEADS, head, 0)


def _peer_select(hn, wq, keys):
    tb = SEL_TB
    sel = lambda dt: jax.ShapeDtypeStruct((PEER_HEADS, PEER_NKEYS, TP), dt)
    sel_spec = pl.BlockSpec((PEER_HEADS, PEER_NKEYS, tb), lambda i: (0, 0, i))
    return pl.pallas_call(
        _peer_select_kernel,
        out_shape=(sel(bf16), sel(f32), sel(f32), sel(bf16)),
        grid=(TP // tb,),
        in_specs=[pl.BlockSpec((tb, D_MODEL), lambda i: (i, 0)),
                  pl.BlockSpec((D_MODEL, PEER_HEADS * 2 * PEER_DKEY), lambda i: (0, 0)),
                  pl.BlockSpec((PEER_HEADS, 2, PEER_NKEYS, PEER_DKEY), lambda i: (0, 0, 0, 0))],
        out_specs=(sel_spec,) * 4,
        scratch_shapes=[pltpu.VMEM((tb, PEER_HEADS * 2 * PEER_DKEY), bf16)],
        compiler_params=_params("parallel"),
        name="peer_select",
    )(hn, wq, keys)


def _peer_kernel(hnt_ref, u0_ref, un_ref, vt_ref, rank1_ref, cnt0_ref, e0_ref, e1_ref, hs_ref, g_ref, o_ref,
                 acc_ref, a_even_ref, a_odd_ref, p_ref):
    j = pl.program_id(1)
    groups = PEER_EB // PEER_NKEYS

    @pl.when(j == 0)
    def _():
        acc_ref[...] = jnp.zeros_like(acc_ref)
        a_even_ref[...] = _dot(u0_ref[...], hnt_ref[...])

    def step(a_cur_ref, a_next_ref):
        a_next_ref[...] = _dot(un_ref[...], hnt_ref[...])
        for gi in range(groups):
            i = j * groups + gi
            cnts = [jnp.broadcast_to(cnt0_ref[h, pl.ds(i, 1), :], (BF16_ROWS, PEER_TB)).astype(bf16)
                    for h in range(PEER_HEADS)]
            e0s = [jnp.broadcast_to(e0_ref[h, pl.ds(i, 1), :], (BF16_ROWS, PEER_TB)).astype(bf16)
                   for h in range(PEER_HEADS)]
            for r in range(PEER_NKEYS // BF16_ROWS):
                keys = slice(r * BF16_ROWS, (r + 1) * BF16_ROWS)
                rows = slice(gi * PEER_NKEYS + r * BF16_ROWS, gi * PEER_NKEYS + (r + 1) * BF16_ROWS)
                a = a_cur_ref[rows, :]
                act = 0.5 * a * (1.0 + lax.erf(a * (2.0 ** -0.5)))
                w = jnp.zeros((BF16_ROWS, PEER_TB), bf16)
                for h in range(PEER_HEADS):
                    w = w + jnp.where(rank1_ref[h, keys, :] < cnts[h], e1_ref[h, keys, :], jnp.zeros((), bf16)) * e0s[h]
                p_ref[rows, :] = w * act.astype(bf16)
        acc_ref[...] += _dot(vt_ref[0], p_ref[...])

    @pl.when(j % 2 == 0)
    def _():
        step(a_even_ref, a_odd_ref)

    @pl.when(j % 2 == 1)
    def _():
        step(a_odd_ref, a_even_ref)

    @pl.when(j == pl.num_programs(1) - 1)
    def _():
        hs3 = hs_ref[...] + acc_ref[...].T
        o_ref[...] = _rms(hs3, g_ref[...])


def _peer(hnt, u, vt, rank1, cnt0, e0, e1, hs2, g_final):
    tb, eb = PEER_TB, PEER_EB
    n_blocks = PEER_N // eb
    sel_spec = pl.BlockSpec((PEER_HEADS, PEER_NKEYS, tb), lambda i, j: (0, 0, i))
    once = pl.Buffered(1)
    return pl.pallas_call(
        _peer_kernel,
        out_shape=jax.ShapeDtypeStruct((TP, D_MODEL), f32),
        grid=(TP // tb, n_blocks),
        in_specs=[pl.BlockSpec((D_MODEL, tb), lambda i, j: (0, i)),
                  pl.BlockSpec((eb, D_MODEL), lambda i, j: (0, 0), pipeline_mode=once),
                  pl.BlockSpec((eb, D_MODEL), lambda i, j: (jnp.minimum(j + 1, n_blocks - 1), 0)),
                  pl.BlockSpec((1, D_MODEL, eb), lambda i, j: (j, 0, 0)),
                  sel_spec, sel_spec, sel_spec, sel_spec,
                  pl.BlockSpec((tb, D_MODEL), lambda i, j: (i, 0), pipeline_mode=once),
                  pl.BlockSpec((1, D_MODEL), lambda i, j: (0, 0))],
        out_specs=pl.BlockSpec((tb, D_MODEL), lambda i, j: (i, 0)),
        scratch_shapes=[pltpu.VMEM((D_MODEL, tb), f32), pltpu.VMEM((eb, tb), f32), pltpu.VMEM((eb, tb), f32),
                        pltpu.VMEM((eb, tb), bf16)],
        compiler_params=_params("parallel", "arbitrary"),
        name="peer_experts",
    )(hnt, u, u, vt, rank1, cnt0, e0, e1, hs2, g_final)


def _rope_tables():
    inv = 1.0 / (ROPE_THETA ** (jnp.arange(ROPE_HALF, dtype=f32) / ROPE_HALF))
    ang = jnp.arange(LP, dtype=jnp.int32).astype(f32)[:, None] * inv[None, :]
    cos, sin = jnp.cos(ang), jnp.sin(ang)
    rest = DIFF_DH - ROPE_DIMS
    zero, zrest = jnp.zeros_like(sin), jnp.zeros((LP, rest), f32)
    return jnp.stack([
        jnp.concatenate([cos, cos, jnp.ones((LP, rest), f32)], axis=1),
        jnp.concatenate([-sin, zero, zrest], axis=1),
        jnp.concatenate([zero, sin, zrest], axis=1)])


def _split_cols(w):
    parts, start = [], 0
    for n in IN_SIZES:
        parts.append(w[:, start:start + n])
        start += n
    return parts


def kernel(x, meta_tokens, g_mix, w_in, gla_w2_fwd, gla_b_fwd, gla_w2_bwd, gla_b_bwd, gla_g_norm, diff_lq1, diff_lk1, diff_lq2, diff_lk2, diff_g_norm, w_branch_gla, w_branch_diff, w_out, g_ffn, peer_w_q, peer_sub_keys, peer_u, peer_v, g_final):
    assert w_in.shape[0] == 1, "single-layer block only"
    l = 0
    lam_init = 0.8 - 0.6 * math.exp(-0.3 * l)
    meta = jnp.broadcast_to(meta_tokens[None].astype(x.dtype), (BATCH, N_META, D_MODEL))
    pad = jnp.zeros((BATCH, LP - L_REAL, D_MODEL), x.dtype)
    hs = jnp.concatenate([meta, x, pad], axis=1).reshape(TP, D_MODEL)
    rope_tab = _rope_tables()

    gq, gk, gv, gr, glr, dq, dk, dv, za, zb = _split_cols(w_in[l])
    w_a = jnp.concatenate([gq, gk, gv, dq, dk, dv], axis=1).astype(bf16)
    w_b = jnp.concatenate([gr, za, zb], axis=1).astype(bf16)
    w_lr = jnp.pad(glr, ((0, 0), (0, LANE - 2 * GLA_LOWRANK))).astype(bf16)
    w2f = jnp.pad(gla_w2_fwd[l], ((0, LANE - GLA_LOWRANK), (0, 0))).astype(bf16)
    w2b = jnp.pad(gla_w2_bwd[l], ((GLA_LOWRANK, LANE - 2 * GLA_LOWRANK), (0, 0))).astype(bf16)

    h = _norm_rows(hs, g_mix[l][None])
    proj_a = _project(h, w_a, bf16, "in_proj_qkv")
    proj_b = _project(h, w_b, f32, "in_proj_gates")
    logf_f, logf_b = _decay(h, w_lr, w2f, w2b, gla_b_fwd[l][None], gla_b_bwd[l][None])
    o_gla = _gla(proj_a, proj_b, logf_f, logf_b, gla_g_norm[l][None])
    o_diff = _diff_attention(proj_a, rope_tab, diff_lq1[l][None], diff_lk1[l][None], diff_lq2[l][None],
                             diff_lk2[l][None], diff_g_norm[l][None], lam_init)
    hs2, hn, hnt = _merge(o_gla, o_diff, proj_b, hs, w_branch_gla[l].astype(bf16), w_branch_diff[l].astype(bf16),
                     w_out[l].astype(bf16), g_ffn[l][None])
    rank1, cnt0, e0, e1 = _peer_select(hn, peer_w_q[l].astype(bf16), peer_sub_keys[l].astype(bf16))
    v_blocks = peer_v[l].reshape(PEER_N // PEER_EB, PEER_EB, D_MODEL).transpose(0, 2, 1).astype(bf16)
    out = _peer(hnt, peer_u[l].astype(bf16), v_blocks, rank1, cnt0, e0, e1, hs2, g_final[None])
    return out.reshape(BATCH, LP, D_MODEL)[:, N_META:L_REAL]
```

```python
import functools
import math

import jax
import jax.numpy as jnp
from jax import lax
from jax.experimental import pallas as pl
from jax.experimental.pallas import tpu as pltpu

f32 = jnp.float32
bf16 = jnp.bfloat16

D_MODEL = 2048
BATCH = 4
SEQ = 2048
N_META = 16
EPS = 1e-6
L_REAL = SEQ + N_META
LANE = 128
BF16_ROWS = 16
LP = -(-L_REAL // LANE) * LANE
TP = BATCH * LP

GLA_HEADS = 4
GLA_DK = 256
GLA_DV = 512
GLA_QK = GLA_HEADS * GLA_DK
GLA_V = GLA_HEADS * GLA_DV
GLA_LOWRANK = 16
GLA_TAU = 16.0
GLA_CHUNK = 64
GLA_UNROLL = 17

DIFF_HEADS = 8
DIFF_DH = 128
DIFF_DV = 256
DIFF_QK = DIFF_HEADS * 2 * DIFF_DH
DIFF_V = DIFF_HEADS * DIFF_DV
ROPE_THETA = 500000.0
ROPE_DIMS = DIFF_DH // 4
ROPE_HALF = ROPE_DIMS // 2

PEER_HEADS = 8
PEER_NKEYS = 128
PEER_N = PEER_NKEYS * PEER_NKEYS
PEER_DKEY = 128
PEER_TOPK = 16

IN_SIZES = (GLA_QK, GLA_QK, GLA_V, GLA_V, 2 * GLA_LOWRANK, DIFF_QK, DIFF_QK, DIFF_V, D_MODEL, D_MODEL)

VMEM_LIMIT = 56 * 1024 * 1024
NEG_BIG = -1e30

ROW_BLOCK = 512
COL_BLOCK = 2048
DIFF_TQ = LP // 4
SEL_TB = 256
PEER_TB = 512
PEER_EB = 512


def _params(*sem):
    return pltpu.CompilerParams(dimension_semantics=sem, vmem_limit_bytes=VMEM_LIMIT)


def _rms(x, g):
    return x * lax.rsqrt(jnp.mean(x * x, axis=-1, keepdims=True) + EPS) * g


def _dot(a, b):
    return jnp.dot(a, b, preferred_element_type=f32)


def _dot_nt(a, b):
    return lax.dot_general(a, b, (((1,), (1,)), ((), ())), preferred_element_type=f32)


def _dot_tn(a, b, precision=None):
    return lax.dot_general(a, b, (((0,), (0,)), ((), ())), preferred_element_type=f32, precision=precision)


def _norm_kernel(x_ref, g_ref, o_ref):
    o_ref[...] = _rms(x_ref[...], g_ref[...]).astype(o_ref.dtype)


def _norm_rows(x, g):
    return pl.pallas_call(
        _norm_kernel,
        out_shape=jax.ShapeDtypeStruct((TP, D_MODEL), bf16),
        grid=(TP // ROW_BLOCK,),
        in_specs=[pl.BlockSpec((ROW_BLOCK, D_MODEL), lambda i: (i, 0)),
                  pl.BlockSpec((1, D_MODEL), lambda i: (0, 0))],
        out_specs=pl.BlockSpec((ROW_BLOCK, D_MODEL), lambda i: (i, 0)),
        compiler_params=_params("parallel"),
        name="mix_norm",
    )(x, g)


def _mm_kernel(a_ref, w_ref, o_ref):
    o_ref[...] = _dot(a_ref[...], w_ref[...]).astype(o_ref.dtype)


def _project(h, w, out_dtype, name):
    n = w.shape[1]
    return pl.pallas_call(
        _mm_kernel,
        out_shape=jax.ShapeDtypeStruct((TP, n), out_dtype),
        grid=(n // COL_BLOCK, TP // ROW_BLOCK),
        in_specs=[pl.BlockSpec((ROW_BLOCK, D_MODEL), lambda j, i: (i, 0)),
                  pl.BlockSpec((D_MODEL, COL_BLOCK), lambda j, i: (0, j))],
        out_specs=pl.BlockSpec((ROW_BLOCK, COL_BLOCK), lambda j, i: (i, j)),
        compiler_params=_params("parallel", "parallel"),
        name=name,
    )(h, w)


def _decay_kernel(h_ref, wlr_ref, w2f_ref, w2b_ref, bf_ref, bb_ref, of_ref, ob_ref):
    lr = _dot(h_ref[...], wlr_ref[...]).astype(bf16)
    zf = _dot(lr, w2f_ref[...]) + bf_ref[...]
    zb = _dot(lr, w2b_ref[...]) + bb_ref[...]
    of_ref[...] = jax.nn.log_sigmoid(zf) * (1.0 / GLA_TAU)
    ob_ref[...] = jax.nn.log_sigmoid(zb) * (1.0 / GLA_TAU)


def _decay(h, wlr, w2f, w2b, b_f, b_b):
    row = lambda i: (i, 0)
    fixed = lambda i: (0, 0)
    return pl.pallas_call(
        _decay_kernel,
        out_shape=(jax.ShapeDtypeStruct((TP, GLA_QK), f32),) * 2,
        grid=(TP // ROW_BLOCK,),
        in_specs=[pl.BlockSpec((ROW_BLOCK, D_MODEL), row),
                  pl.BlockSpec((D_MODEL, LANE), fixed),
                  pl.BlockSpec((LANE, GLA_QK), fixed),
                  pl.BlockSpec((LANE, GLA_QK), fixed),
                  pl.BlockSpec((1, GLA_QK), fixed),
                  pl.BlockSpec((1, GLA_QK), fixed)],
        out_specs=(pl.BlockSpec((ROW_BLOCK, GLA_QK), row),) * 2,
        compiler_params=_params("parallel"),
        name="gla_decay",
    )(h, wlr, w2f, w2b, b_f, b_b)


def _gla_kernel(q_ref, k_ref, v_ref, ff_ref, fb_ref, r_ref, g_ref, o_ref, accf_ref, accb_ref, sf_ref, sb_ref):
    C = GLA_CHUNK
    n_chunks = LP // C
    row = lax.broadcasted_iota(jnp.int32, (C, C), 0)
    col = lax.broadcasted_iota(jnp.int32, (C, C), 1)
    time = lax.broadcasted_iota(jnp.int32, (C, GLA_DK), 0)

    def chunk(n, f_ref, prefix, keep_mask, acc_ref, s_ref):
        r0 = pl.multiple_of(n * C, C)
        rows = pl.ds(r0, C)
        cum = f_ref[rows, :]
        shift = 1
        while shift < C:
            if prefix:
                cum = cum + jnp.where(time >= shift, pltpu.roll(cum, shift, axis=0), 0.0)
            else:
                cum = cum + jnp.where(time < C - shift, pltpu.roll(cum, C - shift, axis=0), 0.0)
            shift *= 2
        total_row = C - 1 if prefix else 0
        tot = cum[total_row:total_row + 1, :]
        tot_col = jnp.broadcast_to(tot, (8, GLA_DK)).T[:, 0:1]
        q = q_ref[rows, :].astype(f32) * (GLA_DK ** -0.5)
        k = k_ref[rows, :].astype(f32)
        v = v_ref[rows, :]
        q_in = (q * jnp.exp(cum)).astype(bf16)
        k_in = (k * jnp.exp(-cum)).astype(bf16)
        k_st = (k * jnp.exp(tot - cum)).astype(bf16)
        a = jnp.where(keep_mask, _dot_nt(q_in, k_in), 0.0)
        acc_ref[rows, :] = _dot(a.astype(bf16), v) + _dot(q_in, s_ref[...].astype(bf16))
        s_ref[...] = s_ref[...] * jnp.exp(tot_col) + _dot_tn(k_st, v)

    sf_ref[...] = jnp.zeros_like(sf_ref)
    sb_ref[...] = jnp.zeros_like(sb_ref)

    def both(m, carry):
        for u in range(GLA_UNROLL):
            n = m * GLA_UNROLL + u
            chunk(n, ff_ref, True, col <= row, accf_ref, sf_ref)
            chunk(n_chunks - 1 - n, fb_ref, False, col > row, accb_ref, sb_ref)
        return carry

    lax.fori_loop(0, n_chunks // GLA_UNROLL, both, 0)

    def fin(n, carry):
        rows = pl.ds(pl.multiple_of(n * LANE, LANE), LANE)
        o = _rms(accf_ref[rows, :] + accb_ref[rows, :], g_ref[...])
        o_ref[rows, :] = (o * jax.nn.silu(r_ref[rows, :])).astype(o_ref.dtype)
        return carry

    lax.fori_loop(0, LP // LANE, fin, 0)


def _gla(proj_a, proj_b, logf_f, logf_b, g_norm):
    kq = GLA_QK // GLA_DK
    kv = (2 * GLA_QK) // GLA_DV
    return pl.pallas_call(
        _gla_kernel,
        out_shape=jax.ShapeDtypeStruct((TP, GLA_V), bf16),
        grid=(BATCH, GLA_HEADS),
        in_specs=[pl.BlockSpec((LP, GLA_DK), lambda b, h: (b, h)),
                  pl.BlockSpec((LP, GLA_DK), lambda b, h: (b, kq + h)),
                  pl.BlockSpec((LP, GLA_DV), lambda b, h: (b, kv + h)),
                  pl.BlockSpec((LP, GLA_DK), lambda b, h: (b, h)),
                  pl.BlockSpec((LP, GLA_DK), lambda b, h: (b, h)),
                  pl.BlockSpec((LP, GLA_DV), lambda b, h: (b, h)),
                  pl.BlockSpec((1, GLA_DV), lambda b, h: (0, 0))],
        out_specs=pl.BlockSpec((LP, GLA_DV), lambda b, h: (b, h)),
        scratch_shapes=[pltpu.VMEM((LP, GLA_DV), f32), pltpu.VMEM((LP, GLA_DV), f32),
                        pltpu.VMEM((GLA_DK, GLA_DV), f32), pltpu.VMEM((GLA_DK, GLA_DV), f32)],
        compiler_params=_params("parallel", "parallel"),
        name="gla_mixer",
    )(proj_a, proj_a, proj_a, logf_f, logf_b, proj_b, g_norm)


def _rope(x, tab):
    return (x * tab[0]
            + pltpu.roll(x, LANE - ROPE_HALF, axis=1) * tab[1]
            + pltpu.roll(x, ROPE_HALF, axis=1) * tab[2])


def _diff_kernel(lam_init, q_ref, k_ref, v_ref, tq_ref, tk_ref, lq1_ref, lk1_ref, lq2_ref, lk2_ref,
                 g_ref, o_ref, kr_ref):
    @pl.when(pl.program_id(2) == 0)
    def _():
        tk = tk_ref[...]
        for m in range(2):
            kx = k_ref[:, m * DIFF_DH:(m + 1) * DIFF_DH].astype(f32)
            kr_ref[m] = _rope(kx, tk).astype(bf16)

    lam = (jnp.exp(jnp.sum(lq1_ref[...] * lk1_ref[...], axis=1, keepdims=True))
           - jnp.exp(jnp.sum(lq2_ref[...] * lk2_ref[...], axis=1, keepdims=True)) + lam_init)
    tq = tq_ref[...]
    key_ok = lax.broadcasted_iota(jnp.int32, (DIFF_TQ, LP), 1) < L_REAL
    exp2_scale = (DIFF_DH ** -0.5) * math.log2(math.e)
    outs = []
    for m in range(2):
        qx = q_ref[:, m * DIFF_DH:(m + 1) * DIFF_DH].astype(f32)
        qr = _rope(qx, tq).astype(bf16)
        s = jnp.where(key_ok, _dot_nt(qr, kr_ref[m]), NEG_BIG)
        e = jnp.exp2((s - jnp.max(s, axis=1, keepdims=True)) * exp2_scale)
        inv = 1.0 / jnp.sum(e, axis=1, keepdims=True)
        outs.append(_dot(e.astype(bf16), v_ref[...]) * inv)
    o = outs[0] - lam * outs[1]
    o_ref[...] = (_rms(o, g_ref[...]) * (1.0 - lam_init)).astype(o_ref.dtype)


def _diff_attention(proj_a, rope_tab, lq1, lk1, lq2, lk2, g_norm, lam_init):
    nq = LP // DIFF_TQ
    cq = (2 * GLA_QK + GLA_V) // DIFF_DV
    ck = cq + DIFF_HEADS
    cv = ck + DIFF_HEADS
    vec = pl.BlockSpec((1, DIFF_DH), lambda b, h, i: (0, 0))
    return pl.pallas_call(
        functools.partial(_diff_kernel, lam_init),
        out_shape=jax.ShapeDtypeStruct((TP, DIFF_V), bf16),
        grid=(BATCH, DIFF_HEADS, nq),
        in_specs=[pl.BlockSpec((DIFF_TQ, DIFF_DV), lambda b, h, i: (b * nq + i, cq + h)),
                  pl.BlockSpec((LP, DIFF_DV), lambda b, h, i: (b, ck + h)),
                  pl.BlockSpec((LP, DIFF_DV), lambda b, h, i: (b, cv + h)),
                  pl.BlockSpec((3, DIFF_TQ, DIFF_DH), lambda b, h, i: (0, i, 0)),
                  pl.BlockSpec((3, LP, DIFF_DH), lambda b, h, i: (0, 0, 0)),
                  vec, vec, vec, vec,
                  pl.BlockSpec((1, DIFF_DV), lambda b, h, i: (0, 0))],
        out_specs=pl.BlockSpec((DIFF_TQ, DIFF_DV), lambda b, h, i: (b * nq + i, h)),
        scratch_shapes=[pltpu.VMEM((2, LP, DIFF_DH), bf16)],
        compiler_params=_params("parallel", "parallel", "arbitrary"),
        name="diff_attention",
    )(proj_a, proj_a, proj_a, rope_tab, rope_tab, lq1, lk1, lq2, lk2, g_norm)


def _merge_kernel(og_ref, od_ref, za_ref, zb_ref, hs_ref, wa_ref, wb_ref, wo_ref, g_ref, hs2_ref, hn_ref, hnt_ref):
    y = (jax.nn.sigmoid(za_ref[...]) * _dot(og_ref[...], wa_ref[...])
         + jax.nn.sigmoid(zb_ref[...]) * _dot(od_ref[...], wb_ref[...]))
    hs2 = hs_ref[...] + _dot(y.astype(bf16), wo_ref[...])
    hs2_ref[...] = hs2
    hn = _rms(hs2, g_ref[...])
    hn_ref[...] = hn.astype(hn_ref.dtype)
    hnt_ref[...] = hn.T.astype(hnt_ref.dtype)


def _merge(o_gla, o_diff, proj_b, hs, wa, wb, wo, g_ffn):
    tm = 256
    row = lambda i: (i, 0)
    fixed = lambda i: (0, 0)
    wspec = pl.BlockSpec((D_MODEL, D_MODEL), fixed, pipeline_mode=pl.Buffered(1))
    return pl.pallas_call(
        _merge_kernel,
        out_shape=(jax.ShapeDtypeStruct((TP, D_MODEL), f32), jax.ShapeDtypeStruct((TP, D_MODEL), bf16),
                   jax.ShapeDtypeStruct((D_MODEL, TP), bf16)),
        grid=(TP // tm,),
        in_specs=[pl.BlockSpec((tm, GLA_V), row),
                  pl.BlockSpec((tm, DIFF_V), row),
                  pl.BlockSpec((tm, D_MODEL), lambda i: (i, 1)),
                  pl.BlockSpec((tm, D_MODEL), lambda i: (i, 2)),
                  pl.BlockSpec((tm, D_MODEL), row),
                  wspec, wspec, wspec,
                  pl.BlockSpec((1, D_MODEL), fixed)],
        out_specs=(pl.BlockSpec((tm, D_MODEL), row), pl.BlockSpec((tm, D_MODEL), row),
                   pl.BlockSpec((D_MODEL, tm), lambda i: (0, i))),
        compiler_params=_params("parallel"),
        name="branch_merge",
    )(o_gla, o_diff, proj_b, proj_b, hs, wa, wb, wo, g_ffn)


def _top16(s, iota, break_ties):
    rank = jnp.full(s.shape, float(PEER_TOPK), f32)
    vals = []
    for r in range(PEER_TOPK):
        m = jnp.max(s, axis=0, keepdims=True)
        hit = s == m
        if break_ties:
            first = jnp.min(jnp.where(hit, iota, float(PEER_NKEYS)), axis=0, keepdims=True)
            hit = iota == first
        rank = jnp.where(hit, float(r), rank)
        s = jnp.where(hit, -jnp.inf, s)
        vals.append(m)
    return vals, rank


def _select_head(s0, s1, iota, ids, break_ties):
    K = PEER_TOPK
    tb = s0.shape[1]
    v0, rank0 = _top16(s0, iota, break_ties)
    v1, rank1 = _top16(s1, iota, break_ties)
    sa = jnp.concatenate(v0, axis=0)
    sb = jnp.concatenate(v1, axis=0)
    cand = jnp.concatenate([sa[0:1] + sb]
                           + [sa[a:a + 1] + sb[0:8] for a in range(1, 8)]
                           + [sa[8:16] + sb[0:1]], axis=0)
    top = cand[0:1]
    taken = jnp.zeros(cand.shape, f32)
    z = jnp.zeros((1, tb), f32)
    for _ in range(K):
        m = jnp.max(cand, axis=0, keepdims=True)
        hit = cand == m
        if break_ties:
            first = jnp.min(jnp.where(hit, ids, 1e9), axis=0, keepdims=True)
            hit = ids == first
        taken = jnp.where(hit, 1.0, taken)
        cand = jnp.where(hit, -jnp.inf, cand)
        z = z + jnp.exp(m - top)
    cnt = ([jnp.sum(taken[0:16], axis=0, keepdims=True)]
           + [jnp.sum(taken[8 + 8 * a:16 + 8 * a], axis=0, keepdims=True) for a in range(1, 8)]
           + [taken[72 + a:73 + a] for a in range(8)])
    cnt0 = jnp.zeros((PEER_NKEYS, tb), f32)
    for a in range(K):
        cnt0 = jnp.where(rank0 == float(a), cnt[a], cnt0)
    e0 = jnp.exp(s0 - v0[0]) * (1.0 / z)
    e1 = jnp.exp(s1 - v1[0])
    marked = (jnp.sum((rank0 < float(K)).astype(f32), axis=0, keepdims=True),
              jnp.sum((rank1 < float(K)).astype(f32), axis=0, keepdims=True),
              jnp.sum(taken, axis=0, keepdims=True))
    excess = jnp.max(sum(jnp.abs(n - float(K)) for n in marked))
    return (rank1, cnt0, e0, e1), excess


def _peer_select_kernel(hn_ref, wq_ref, keys_ref, rank1_ref, cnt0_ref, e0_ref, e1_ref, q_sc):
    tb = SEL_TB
    K = PEER_TOPK
    q_sc[...] = _dot(hn_ref[...], wq_ref[...]).astype(bf16)
    iota = lax.broadcasted_iota(jnp.int32, (PEER_NKEYS, tb), 0).astype(f32)
    i16 = lax.broadcasted_iota(jnp.int32, (K, tb), 0).astype(f32)
    i8 = lax.broadcasted_iota(jnp.int32, (8, tb), 0).astype(f32)
    ids = jnp.concatenate([i16] + [a * float(K) + i8 for a in range(1, 8)] + [(i8 + 8.0) * float(K)], axis=0)

    def head(h, carry):
        c0 = pl.multiple_of(h * (2 * PEER_DKEY), 2 * PEER_DKEY)
        s0 = _dot_nt(keys_ref[h, 0], q_sc[:, pl.ds(c0, PEER_DKEY)])
        s1 = _dot_nt(keys_ref[h, 1], q_sc[:, pl.ds(c0 + PEER_DKEY, PEER_DKEY)])

        def store(rank1, cnt0, e0, e1):
            rank1_ref[h] = rank1.astype(bf16)
            cnt0_ref[h] = cnt0
            e0_ref[h] = e0
            e1_ref[h] = e1.astype(bf16)

        tables, excess = _select_head(s0, s1, iota, ids, False)
        store(*tables)

        @pl.when(excess > 0.0)
        def _():
            store(*_select_head(s0, s1, iota, ids, True)[0])

        return carry

    lax.fori_loop(0, PEER_HEADS, head, 0)


def _peer_select(hn, wq, keys):
    tb = SEL_TB
    sel = lambda dt: jax.ShapeDtypeStruct((PEER_HEADS, PEER_NKEYS, TP), dt)
    sel_spec = pl.BlockSpec((PEER_HEADS, PEER_NKEYS, tb), lambda i: (0, 0, i))
    return pl.pallas_call(
        _peer_select_kernel,
        out_shape=(sel(bf16), sel(f32), sel(f32), sel(bf16)),
        grid=(TP // tb,),
        in_specs=[pl.BlockSpec((tb, D_MODEL), lambda i: (i, 0)),
                  pl.BlockSpec((D_MODEL, PEER_HEADS * 2 * PEER_DKEY), lambda i: (0, 0)),
                  pl.BlockSpec((PEER_HEADS, 2, PEER_NKEYS, PEER_DKEY), lambda i: (0, 0, 0, 0))],
        out_specs=(sel_spec,) * 4,
        scratch_shapes=[pltpu.VMEM((tb, PEER_HEADS * 2 * PEER_DKEY), bf16)],
        compiler_params=_params("parallel"),
        name="peer_select",
    )(hn, wq, keys)


def _peer_kernel(hnt_ref, u0_ref, un_ref, vt_ref, rank1_ref, cnt0_ref, e0_ref, e1_ref, hs_ref, g_ref, o_ref,
                 acc_ref, a_even_ref, a_odd_ref, p_ref):
    j = pl.program_id(1)
    groups = PEER_EB // PEER_NKEYS

    @pl.when(j == 0)
    def _():
        acc_ref[...] = jnp.zeros_like(acc_ref)
        a_even_ref[...] = _dot(u0_ref[...], hnt_ref[...])

    def step(a_cur_ref, a_next_ref):
        a_next_ref[...] = _dot(un_ref[...], hnt_ref[...])
        for gi in range(groups):
            i = j * groups + gi
            cnts = [jnp.broadcast_to(cnt0_ref[h, pl.ds(i, 1), :], (BF16_ROWS, PEER_TB)).astype(bf16)
                    for h in range(PEER_HEADS)]
            e0s = [jnp.broadcast_to(e0_ref[h, pl.ds(i, 1), :], (BF16_ROWS, PEER_TB)).astype(bf16)
                   for h in range(PEER_HEADS)]
            for r in range(PEER_NKEYS // BF16_ROWS):
                keys = slice(r * BF16_ROWS, (r + 1) * BF16_ROWS)
                rows = slice(gi * PEER_NKEYS + r * BF16_ROWS, gi * PEER_NKEYS + (r + 1) * BF16_ROWS)
                a = a_cur_ref[rows, :]
                act = 0.5 * a * (1.0 + lax.erf(a * (2.0 ** -0.5)))
                w = jnp.zeros((BF16_ROWS, PEER_TB), bf16)
                for h in range(PEER_HEADS):
                    w = w + jnp.where(rank1_ref[h, keys, :] < cnts[h], e1_ref[h, keys, :], jnp.zeros((), bf16)) * e0s[h]
                p_ref[rows, :] = w * act.astype(bf16)
        acc_ref[...] += _dot(vt_ref[0], p_ref[...])

    @pl.when(j % 2 == 0)
    def _():
        step(a_even_ref, a_odd_ref)

    @pl.when(j % 2 == 1)
    def _():
        step(a_odd_ref, a_even_ref)

    @pl.when(j == pl.num_programs(1) - 1)
    def _():
        hs3 = hs_ref[...] + acc_ref[...].T
        o_ref[...] = _rms(hs3, g_ref[...])


def _peer(hnt, u, vt, rank1, cnt0, e0, e1, hs2, g_final):
    tb, eb = PEER_TB, PEER_EB
    n_blocks = PEER_N // eb
    sel_spec = pl.BlockSpec((PEER_HEADS, PEER_NKEYS, tb), lambda i, j: (0, 0, i))
    once = pl.Buffered(1)
    return pl.pallas_call(
        _peer_kernel,
        out_shape=jax.ShapeDtypeStruct((TP, D_MODEL), f32),
        grid=(TP // tb, n_blocks),
        in_specs=[pl.BlockSpec((D_MODEL, tb), lambda i, j: (0, i)),
                  pl.BlockSpec((eb, D_MODEL), lambda i, j: (0, 0), pipeline_mode=once),
                  pl.BlockSpec((eb, D_MODEL), lambda i, j: (jnp.minimum(j + 1, n_blocks - 1), 0)),
                  pl.BlockSpec((1, D_MODEL, eb), lambda i, j: (j, 0, 0)),
                  sel_spec, sel_spec, sel_spec, sel_spec,
                  pl.BlockSpec((tb, D_MODEL), lambda i, j: (i, 0), pipeline_mode=once),
                  pl.BlockSpec((1, D_MODEL), lambda i, j: (0, 0))],
        out_specs=pl.BlockSpec((tb, D_MODEL), lambda i, j: (i, 0)),
        scratch_shapes=[pltpu.VMEM((D_MODEL, tb), f32), pltpu.VMEM((eb, tb), f32), pltpu.VMEM((eb, tb), f32),
                        pltpu.VMEM((eb, tb), bf16)],
        compiler_params=_params("parallel", "arbitrary"),
        name="peer_experts",
    )(hnt, u, u, vt, rank1, cnt0, e0, e1, hs2, g_final)


def _rope_tables():
    inv = 1.0 / (ROPE_THETA ** (jnp.arange(ROPE_HALF, dtype=f32) / ROPE_HALF))
    ang = jnp.arange(LP, dtype=jnp.int32).astype(f32)[:, None] * inv[None, :]
    cos, sin = jnp.cos(ang), jnp.sin(ang)
    rest = DIFF_DH - ROPE_DIMS
    zero, zrest = jnp.zeros_like(sin), jnp.zeros((LP, rest), f32)
    return jnp.stack([
        jnp.concatenate([cos, cos, jnp.ones((LP, rest), f32)], axis=1),
        jnp.concatenate([-sin, zero, zrest], axis=1),
        jnp.concatenate([zero, sin, zrest], axis=1)])


def _split_cols(w):
    parts, start = [], 0
    for n in IN_SIZES:
        parts.append(w[:, start:start + n])
        start += n
    return parts


def kernel(x, meta_tokens, g_mix, w_in, gla_w2_fwd, gla_b_fwd, gla_w2_bwd, gla_b_bwd, gla_g_norm, diff_lq1, diff_lk1, diff_lq2, diff_lk2, diff_g_norm, w_branch_gla, w_branch_diff, w_out, g_ffn, peer_w_q, peer_sub_keys, peer_u, peer_v, g_final):
    assert w_in.shape[0] == 1, "single-layer block only"
    l = 0
    lam_init = 0.8 - 0.6 * math.exp(-0.3 * l)
    meta = jnp.broadcast_to(meta_tokens[None].astype(x.dtype), (BATCH, N_META, D_MODEL))
    pad = jnp.zeros((BATCH, LP - L_REAL, D_MODEL), x.dtype)
    hs = jnp.concatenate([meta, x, pad], axis=1).reshape(TP, D_MODEL)
    rope_tab = _rope_tables()

    gq, gk, gv, gr, glr, dq, dk, dv, za, zb = _split_cols(w_in[l])
    w_a = jnp.concatenate([gq, gk, gv, dq, dk, dv], axis=1).astype(bf16)
    w_b = jnp.concatenate([gr, za, zb], axis=1).astype(bf16)
    w_lr = jnp.pad(glr, ((0, 0), (0, LANE - 2 * GLA_LOWRANK))).astype(bf16)
    w2f = jnp.pad(gla_w2_fwd[l], ((0, LANE - GLA_LOWRANK), (0, 0))).astype(bf16)
    w2b = jnp.pad(gla_w2_bwd[l], ((GLA_LOWRANK, LANE - 2 * GLA_LOWRANK), (0, 0))).astype(bf16)

    h = _norm_rows(hs, g_mix[l][None])
    proj_a = _project(h, w_a, bf16, "in_proj_qkv")
    proj_b = _project(h, w_b, f32, "in_proj_gates")
    logf_f, logf_b = _decay(h, w_lr, w2f, w2b, gla_b_fwd[l][None], gla_b_bwd[l][None])
    o_gla = _gla(proj_a, proj_b, logf_f, logf_b, gla_g_norm[l][None])
    o_diff = _diff_attention(proj_a, rope_tab, diff_lq1[l][None], diff_lk1[l][None], diff_lq2[l][None],
                             diff_lk2[l][None], diff_g_norm[l][None], lam_init)
    hs2, hn, hnt = _merge(o_gla, o_diff, proj_b, hs, w_branch_gla[l].astype(bf16), w_branch_diff[l].astype(bf16),
                     w_out[l].astype(bf16), g_ffn[l][None])
    rank1, cnt0, e0, e1 = _peer_select(hn, peer_w_q[l].astype(bf16), peer_sub_keys[l].astype(bf16))
    v_blocks = peer_v[l].reshape(PEER_N // PEER_EB, PEER_EB, D_MODEL).transpose(0, 2, 1).astype(bf16)
    out = _peer(hnt, peer_u[l].astype(bf16), v_blocks, rank1, cnt0, e0, e1, hs2, g_final[None])
    return out.reshape(BATCH, LP, D_MODEL)[:, N_META:L_REAL]
```

```python
import functools
import math

import jax
import jax.numpy as jnp
from jax import lax
from jax.experimental import pallas as pl
from jax.experimental.pallas import tpu as pltpu

f32 = jnp.float32
bf16 = jnp.bfloat16

D_MODEL = 2048
BATCH = 4
SEQ = 2048
N_META = 16
EPS = 1e-6
L_REAL = SEQ + N_META
LANE = 128
BF16_ROWS = 16
LP = -(-L_REAL // LANE) * LANE
TP = BATCH * LP

GLA_HEADS = 4
GLA_DK = 256
GLA_DV = 512
GLA_QK = GLA_HEADS * GLA_DK
GLA_V = GLA_HEADS * GLA_DV
GLA_LOWRANK = 16
GLA_TAU = 16.0
GLA_CHUNK = 64
GLA_UNROLL = 17

DIFF_HEADS = 8
DIFF_DH = 128
DIFF_DV = 256
DIFF_QK = DIFF_HEADS * 2 * DIFF_DH
DIFF_V = DIFF_HEADS * DIFF_DV
ROPE_THETA = 500000.0
ROPE_DIMS = DIFF_DH // 4
ROPE_HALF = ROPE_DIMS // 2

PEER_HEADS = 8
PEER_NKEYS = 128
PEER_N = PEER_NKEYS * PEER_NKEYS
PEER_DKEY = 128
PEER_TOPK = 16

IN_SIZES = (GLA_QK, GLA_QK, GLA_V, GLA_V, 2 * GLA_LOWRANK, DIFF_QK, DIFF_QK, DIFF_V, D_MODEL, D_MODEL)

VMEM_LIMIT = 56 * 1024 * 1024
NEG_BIG = -1e30
RANK_MARK = 2.0 ** 100

ROW_BLOCK = 512
COL_BLOCK = 2048
DIFF_TQ = LP // 4
SEL_TB = 256
PEER_TB = 512
PEER_EB = 512


def _params(*sem):
    return pltpu.CompilerParams(dimension_semantics=sem, vmem_limit_bytes=VMEM_LIMIT)


def _rms(x, g):
    return x * lax.rsqrt(jnp.mean(x * x, axis=-1, keepdims=True) + EPS) * g


def _dot(a, b):
    return jnp.dot(a, b, preferred_element_type=f32)


def _dot_nt(a, b):
    return lax.dot_general(a, b, (((1,), (1,)), ((), ())), preferred_element_type=f32)


def _dot_tn(a, b, precision=None):
    return lax.dot_general(a, b, (((0,), (0,)), ((), ())), preferred_element_type=f32, precision=precision)


def _norm_kernel(x_ref, g_ref, o_ref):
    o_ref[...] = _rms(x_ref[...], g_ref[...]).astype(o_ref.dtype)


def _norm_rows(x, g):
    return pl.pallas_call(
        _norm_kernel,
        out_shape=jax.ShapeDtypeStruct((TP, D_MODEL), bf16),
        grid=(TP // ROW_BLOCK,),
        in_specs=[pl.BlockSpec((ROW_BLOCK, D_MODEL), lambda i: (i, 0)),
                  pl.BlockSpec((1, D_MODEL), lambda i: (0, 0))],
        out_specs=pl.BlockSpec((ROW_BLOCK, D_MODEL), lambda i: (i, 0)),
        compiler_params=_params("parallel"),
        name="mix_norm",
    )(x, g)


def _mm_kernel(a_ref, wt_ref, o_ref):
    o_ref[...] = _dot_nt(a_ref[...], wt_ref[...]).astype(o_ref.dtype)


def _project(h, w, out_dtype, name):
    n = w.shape[0]
    return pl.pallas_call(
        _mm_kernel,
        out_shape=jax.ShapeDtypeStruct((TP, n), out_dtype),
        grid=(n // COL_BLOCK, TP // ROW_BLOCK),
        in_specs=[pl.BlockSpec((ROW_BLOCK, D_MODEL), lambda j, i: (i, 0)),
                  pl.BlockSpec((COL_BLOCK, D_MODEL), lambda j, i: (j, 0))],
        out_specs=pl.BlockSpec((ROW_BLOCK, COL_BLOCK), lambda j, i: (i, j)),
        compiler_params=_params("parallel", "parallel"),
        name=name,
    )(h, w)


def _decay_kernel(h_ref, wlr_ref, w2f_ref, w2b_ref, bf_ref, bb_ref, of_ref, ob_ref):
    lr = _dot_nt(h_ref[...], wlr_ref[...]).astype(bf16)
    zf = _dot(lr, w2f_ref[...]) + bf_ref[...]
    zb = _dot(lr, w2b_ref[...]) + bb_ref[...]
    of_ref[...] = jax.nn.log_sigmoid(zf) * (1.0 / GLA_TAU)
    ob_ref[...] = jax.nn.log_sigmoid(zb) * (1.0 / GLA_TAU)


def _decay(h, wlr, w2f, w2b, b_f, b_b):
    row = lambda i: (i, 0)
    fixed = lambda i: (0, 0)
    return pl.pallas_call(
        _decay_kernel,
        out_shape=(jax.ShapeDtypeStruct((TP, GLA_QK), f32),) * 2,
        grid=(TP // ROW_BLOCK,),
        in_specs=[pl.BlockSpec((ROW_BLOCK, D_MODEL), row),
                  pl.BlockSpec((LANE, D_MODEL), fixed),
                  pl.BlockSpec((LANE, GLA_QK), fixed),
                  pl.BlockSpec((LANE, GLA_QK), fixed),
                  pl.BlockSpec((1, GLA_QK), fixed),
                  pl.BlockSpec((1, GLA_QK), fixed)],
        out_specs=(pl.BlockSpec((ROW_BLOCK, GLA_QK), row),) * 2,
        compiler_params=_params("parallel"),
        name="gla_decay",
    )(h, wlr, w2f, w2b, b_f, b_b)


def _gla_kernel(q_ref, k_ref, v_ref, ff_ref, fb_ref, r_ref, g_ref, o_ref, accf_ref, accb_ref, sf_ref, sb_ref):
    C = GLA_CHUNK
    n_chunks = LP // C
    row = lax.broadcasted_iota(jnp.int32, (C, C), 0)
    col = lax.broadcasted_iota(jnp.int32, (C, C), 1)
    time = lax.broadcasted_iota(jnp.int32, (C, GLA_DK), 0)

    def chunk(n, f_ref, prefix, keep_mask, acc_ref, s_ref):
        r0 = pl.multiple_of(n * C, C)
        rows = pl.ds(r0, C)
        cum = f_ref[rows, :]
        shift = 1
        while shift < C:
            if prefix:
                cum = cum + jnp.where(time >= shift, pltpu.roll(cum, shift, axis=0), 0.0)
            else:
                cum = cum + jnp.where(time < C - shift, pltpu.roll(cum, C - shift, axis=0), 0.0)
            shift *= 2
        total_row = C - 1 if prefix else 0
        tot = cum[total_row:total_row + 1, :]
        tot_col = jnp.broadcast_to(tot, (8, GLA_DK)).T[:, 0:1]
        q = q_ref[rows, :].astype(f32) * (GLA_DK ** -0.5)
        k = k_ref[rows, :].astype(f32)
        v = v_ref[rows, :]
        q_in = (q * jnp.exp(cum)).astype(bf16)
        k_in = (k * jnp.exp(-cum)).astype(bf16)
        k_st = (k * jnp.exp(tot - cum)).astype(bf16)
        a = jnp.where(keep_mask, _dot_nt(q_in, k_in), 0.0)
        acc_ref[rows, :] = _dot(a.astype(bf16), v) + _dot(q_in, s_ref[...].astype(bf16))
        s_ref[...] = s_ref[...] * jnp.exp(tot_col) + _dot_tn(k_st, v)

    sf_ref[...] = jnp.zeros_like(sf_ref)
    sb_ref[...] = jnp.zeros_like(sb_ref)

    def both(m, carry):
        for u in range(GLA_UNROLL):
            n = m * GLA_UNROLL + u
            chunk(n, ff_ref, True, col <= row, accf_ref, sf_ref)
            chunk(n_chunks - 1 - n, fb_ref, False, col > row, accb_ref, sb_ref)
        return carry

    lax.fori_loop(0, n_chunks // GLA_UNROLL, both, 0)

    def fin(n, carry):
        rows = pl.ds(pl.multiple_of(n * LANE, LANE), LANE)
        o = _rms(accf_ref[rows, :] + accb_ref[rows, :], g_ref[...])
        o_ref[rows, :] = (o * jax.nn.silu(r_ref[rows, :])).astype(o_ref.dtype)
        return carry

    lax.fori_loop(0, LP // LANE, fin, 0)


def _gla(proj_a, proj_b, logf_f, logf_b, g_norm):
    kq = GLA_QK // GLA_DK
    kv = (2 * GLA_QK) // GLA_DV
    return pl.pallas_call(
        _gla_kernel,
        out_shape=jax.ShapeDtypeStruct((TP, GLA_V), bf16),
        grid=(BATCH, GLA_HEADS),
        in_specs=[pl.BlockSpec((LP, GLA_DK), lambda b, h: (b, h)),
                  pl.BlockSpec((LP, GLA_DK), lambda b, h: (b, kq + h)),
                  pl.BlockSpec((LP, GLA_DV), lambda b, h: (b, kv + h)),
                  pl.BlockSpec((LP, GLA_DK), lambda b, h: (b, h)),
                  pl.BlockSpec((LP, GLA_DK), lambda b, h: (b, h)),
                  pl.BlockSpec((LP, GLA_DV), lambda b, h: (b, h)),
                  pl.BlockSpec((1, GLA_DV), lambda b, h: (0, 0))],
        out_specs=pl.BlockSpec((LP, GLA_DV), lambda b, h: (b, h)),
        scratch_shapes=[pltpu.VMEM((LP, GLA_DV), f32), pltpu.VMEM((LP, GLA_DV), f32),
                        pltpu.VMEM((GLA_DK, GLA_DV), f32), pltpu.VMEM((GLA_DK, GLA_DV), f32)],
        compiler_params=_params("parallel", "parallel"),
        name="gla_mixer",
    )(proj_a, proj_a, proj_a, logf_f, logf_b, proj_b, g_norm)


def _rope(x, tab):
    return (x * tab[0]
            + pltpu.roll(x, LANE - ROPE_HALF, axis=1) * tab[1]
            + pltpu.roll(x, ROPE_HALF, axis=1) * tab[2])


def _diff_kernel(lam_init, q_ref, k_ref, v_ref, tq_ref, tk_ref, lq1_ref, lk1_ref, lq2_ref, lk2_ref,
                 g_ref, o_ref, kr_ref):
    @pl.when(pl.program_id(2) == 0)
    def _():
        tk = tk_ref[...]
        for m in range(2):
            kx = k_ref[:, m * DIFF_DH:(m + 1) * DIFF_DH].astype(f32)
            kr_ref[m] = _rope(kx, tk).astype(bf16)

    lam = (jnp.exp(jnp.sum(lq1_ref[...] * lk1_ref[...], axis=1, keepdims=True))
           - jnp.exp(jnp.sum(lq2_ref[...] * lk2_ref[...], axis=1, keepdims=True)) + lam_init)
    tq = tq_ref[...]
    key_ok = lax.broadcasted_iota(jnp.int32, (DIFF_TQ, LP), 1) < L_REAL
    exp2_scale = (DIFF_DH ** -0.5) * math.log2(math.e)
    outs = []
    for m in range(2):
        qx = q_ref[:, m * DIFF_DH:(m + 1) * DIFF_DH].astype(f32)
        qr = _rope(qx, tq).astype(bf16)
        s = jnp.where(key_ok, _dot_nt(qr, kr_ref[m]), NEG_BIG)
        e = jnp.exp2((s - jnp.max(s, axis=1, keepdims=True)) * exp2_scale)
        inv = 1.0 / jnp.sum(e, axis=1, keepdims=True)
        outs.append(_dot(e.astype(bf16), v_ref[...]) * inv)
    o = outs[0] - lam * outs[1]
    o_ref[...] = (_rms(o, g_ref[...]) * (1.0 - lam_init)).astype(o_ref.dtype)


def _diff_attention(proj_a, rope_tab, lq1, lk1, lq2, lk2, g_norm, lam_init):
    nq = LP // DIFF_TQ
    cq = (2 * GLA_QK + GLA_V) // DIFF_DV
    ck = cq + DIFF_HEADS
    cv = ck + DIFF_HEADS
    vec = pl.BlockSpec((1, DIFF_DH), lambda b, h, i: (0, 0))
    return pl.pallas_call(
        functools.partial(_diff_kernel, lam_init),
        out_shape=jax.ShapeDtypeStruct((TP, DIFF_V), bf16),
        grid=(BATCH, DIFF_HEADS, nq),
        in_specs=[pl.BlockSpec((DIFF_TQ, DIFF_DV), lambda b, h, i: (b * nq + i, cq + h)),
                  pl.BlockSpec((LP, DIFF_DV), lambda b, h, i: (b, ck + h)),
                  pl.BlockSpec((LP, DIFF_DV), lambda b, h, i: (b, cv + h)),
                  pl.BlockSpec((3, DIFF_TQ, DIFF_DH), lambda b, h, i: (0, i, 0)),
                  pl.BlockSpec((3, LP, DIFF_DH), lambda b, h, i: (0, 0, 0)),
                  vec, vec, vec, vec,
                  pl.BlockSpec((1, DIFF_DV), lambda b, h, i: (0, 0))],
        out_specs=pl.BlockSpec((DIFF_TQ, DIFF_DV), lambda b, h, i: (b * nq + i, h)),
        scratch_shapes=[pltpu.VMEM((2, LP, DIFF_DH), bf16)],
        compiler_params=_params("parallel", "parallel", "arbitrary"),
        name="diff_attention",
    )(proj_a, proj_a, proj_a, rope_tab, rope_tab, lq1, lk1, lq2, lk2, g_norm)


def _merge_kernel(og_ref, od_ref, za_ref, zb_ref, hs_ref, wa_ref, wb_ref, wo_ref, g_ref, hs2_ref, hn_ref, hnt_ref):
    y = (jax.nn.sigmoid(za_ref[...]) * _dot(og_ref[...], wa_ref[...])
         + jax.nn.sigmoid(zb_ref[...]) * _dot(od_ref[...], wb_ref[...]))
    hs2 = hs_ref[...] + _dot(y.astype(bf16), wo_ref[...])
    hs2_ref[...] = hs2
    hn = _rms(hs2, g_ref[...])
    hn_ref[...] = hn.astype(hn_ref.dtype)
    hnt_ref[...] = hn.T.astype(hnt_ref.dtype)


def _merge(o_gla, o_diff, proj_b, hs, wa, wb, wo, g_ffn):
    tm = 256
    row = lambda i: (i, 0)
    fixed = lambda i: (0, 0)
    wspec = pl.BlockSpec((D_MODEL, D_MODEL), fixed, pipeline_mode=pl.Buffered(1))
    return pl.pallas_call(
        _merge_kernel,
        out_shape=(jax.ShapeDtypeStruct((TP, D_MODEL), f32), jax.ShapeDtypeStruct((TP, D_MODEL), bf16),
                   jax.ShapeDtypeStruct((D_MODEL, TP), bf16)),
        grid=(TP // tm,),
        in_specs=[pl.BlockSpec((tm, GLA_V), row),
                  pl.BlockSpec((tm, DIFF_V), row),
                  pl.BlockSpec((tm, D_MODEL), lambda i: (i, 1)),
                  pl.BlockSpec((tm, D_MODEL), lambda i: (i, 2)),
                  pl.BlockSpec((tm, D_MODEL), row),
                  wspec, wspec, wspec,
                  pl.BlockSpec((1, D_MODEL), fixed)],
        out_specs=(pl.BlockSpec((tm, D_MODEL), row), pl.BlockSpec((tm, D_MODEL), row),
                   pl.BlockSpec((D_MODEL, tm), lambda i: (0, i))),
        compiler_params=_params("parallel"),
        name="branch_merge",
    )(o_gla, o_diff, proj_b, proj_b, hs, wa, wb, wo, g_ffn)


def _top16(s, iota, break_ties):
    vals = []
    if not break_ties:
        for r in range(PEER_TOPK):
            m = jnp.max(s, axis=0, keepdims=True)
            s = jnp.where(s == m, -RANK_MARK * (r + 1), s)
            vals.append(m)
        rank = jnp.where(s <= -RANK_MARK, s * (-1.0 / RANK_MARK) - 1.0, float(PEER_TOPK))
        return vals, rank
    rank = jnp.full(s.shape, float(PEER_TOPK), f32)
    for r in range(PEER_TOPK):
        m = jnp.max(s, axis=0, keepdims=True)
        first = jnp.min(jnp.where(s == m, iota, float(PEER_NKEYS)), axis=0, keepdims=True)
        hit = iota == first
        rank = jnp.where(hit, float(r), rank)
        s = jnp.where(hit, -jnp.inf, s)
        vals.append(m)
    return vals, rank


def _select_head(s0, s1, iota, ids, break_ties):
    K = PEER_TOPK
    tb = s0.shape[1]
    v0, rank0 = _top16(s0, iota, break_ties)
    v1, rank1 = _top16(s1, iota, break_ties)
    sa = jnp.concatenate(v0, axis=0)
    sb = jnp.concatenate(v1, axis=0)
    cand = jnp.concatenate([sa[0:1] + sb]
                           + [sa[a:a + 1] + sb[0:8] for a in range(1, 8)]
                           + [sa[8:16] + sb[0:1]], axis=0)
    top = cand[0:1]
    z = jnp.zeros((1, tb), f32)
    for _ in range(K):
        m = jnp.max(cand, axis=0, keepdims=True)
        hit = cand == m
        if break_ties:
            first = jnp.min(jnp.where(hit, ids, 1e9), axis=0, keepdims=True)
            hit = ids == first
        cand = jnp.where(hit, -jnp.inf, cand)
        z = z + jnp.exp(m - top)
    taken = jnp.where(cand == -jnp.inf, 1.0, 0.0)
    cnt = ([jnp.sum(taken[0:16], axis=0, keepdims=True)]
           + [jnp.sum(taken[8 + 8 * a:16 + 8 * a], axis=0, keepdims=True) for a in range(1, 8)]
           + [taken[72 + a:73 + a] for a in range(8)])
    cnt0 = jnp.zeros((PEER_NKEYS, tb), f32)
    for a in range(K):
        cnt0 = jnp.where(rank0 == float(a), cnt[a], cnt0)
    e0 = jnp.exp(s0 - v0[0]) * (1.0 / z)
    e1 = jnp.exp(s1 - v1[0])
    marked = (jnp.sum((rank0 < float(K)).astype(f32), axis=0, keepdims=True),
              jnp.sum((rank1 < float(K)).astype(f32), axis=0, keepdims=True),
              jnp.sum(taken, axis=0, keepdims=True))
    excess = jnp.max(sum(jnp.abs(n - float(K)) for n in marked))
    return (rank1, cnt0, e0, e1), excess


def _peer_select_kernel(hn_ref, wq_ref, keys_ref, rank1_ref, cnt0_ref, e0_ref, e1_ref, q_sc):
    tb = SEL_TB
    K = PEER_TOPK
    q_sc[...] = _dot(hn_ref[...], wq_ref[...]).astype(bf16)
    iota = lax.broadcasted_iota(jnp.int32, (PEER_NKEYS, tb), 0).astype(f32)
    i16 = lax.broadcasted_iota(jnp.int32, (K, tb), 0).astype(f32)
    i8 = lax.broadcasted_iota(jnp.int32, (8, tb), 0).astype(f32)
    ids = jnp.concatenate([i16] + [a * float(K) + i8 for a in range(1, 8)] + [(i8 + 8.0) * float(K)], axis=0)

    def head(h, carry):
        c0 = pl.multiple_of(h * (2 * PEER_DKEY), 2 * PEER_DKEY)
        s0 = _dot_nt(keys_ref[h, 0], q_sc[:, pl.ds(c0, PEER_DKEY)])
        s1 = _dot_nt(keys_ref[h, 1], q_sc[:, pl.ds(c0 + PEER_DKEY, PEER_DKEY)])

        def store(rank1, cnt0, e0, e1):
            rank1_ref[h] = rank1.astype(bf16)
            cnt0_ref[h] = cnt0
            e0_ref[h] = e0
            e1_ref[h] = e1.astype(bf16)

        tables, excess = _select_head(s0, s1, iota, ids, False)
        store(*tables)

        @pl.when(excess > 0.0)
        def _():
            store(*_select_head(s0, s1, iota, ids, True)[0])

        return carry

    lax.fori_loop(0, PEER_HEADS, head, 0)


def _peer_select(hn, wq, keys):
    tb = SEL_TB
    sel = lambda dt: jax.ShapeDtypeStruct((PEER_HEADS, PEER_NKEYS, TP), dt)
    sel_spec = pl.BlockSpec((PEER_HEADS, PEER_NKEYS, tb), lambda i: (0, 0, i))
    return pl.pallas_call(
        _peer_select_kernel,
        out_shape=(sel(bf16), sel(f32), sel(f32), sel(bf16)),
        grid=(TP // tb,),
        in_specs=[pl.BlockSpec((tb, D_MODEL), lambda i: (i, 0)),
                  pl.BlockSpec((D_MODEL, PEER_HEADS * 2 * PEER_DKEY), lambda i: (0, 0)),
                  pl.BlockSpec((PEER_HEADS, 2, PEER_NKEYS, PEER_DKEY), lambda i: (0, 0, 0, 0))],
        out_specs=(sel_spec,) * 4,
        scratch_shapes=[pltpu.VMEM((tb, PEER_HEADS * 2 * PEER_DKEY), bf16)],
        compiler_params=_params("parallel"),
        name="peer_select",
    )(hn, wq, keys)


def _peer_kernel(hnt_ref, u0_ref, un_ref, v_ref, rank1_ref, cnt0_ref, e0_ref, e1_ref, hs_ref, g_ref, o_ref,
                 acc_ref, a_even_ref, a_odd_ref, p_ref):
    j = pl.program_id(1)
    groups = PEER_EB // PEER_NKEYS

    @pl.when(j == 0)
    def _():
        acc_ref[...] = jnp.zeros_like(acc_ref)
        a_even_ref[...] = _dot(u0_ref[...], hnt_ref[...])

    def step(a_cur_ref, a_next_ref):
        a_next_ref[...] = _dot(un_ref[...], hnt_ref[...])
        for gi in range(groups):
            i = j * groups + gi
            cnts = [jnp.broadcast_to(cnt0_ref[h, pl.ds(i, 1), :], (BF16_ROWS, PEER_TB)).astype(bf16)
                    for h in range(PEER_HEADS)]
            e0s = [jnp.broadcast_to(e0_ref[h, pl.ds(i, 1), :], (BF16_ROWS, PEER_TB)).astype(bf16)
                   for h in range(PEER_HEADS)]
            for r in range(PEER_NKEYS // BF16_ROWS):
                keys = slice(r * BF16_ROWS, (r + 1) * BF16_ROWS)
                rows = slice(gi * PEER_NKEYS + r * BF16_ROWS, gi * PEER_NKEYS + (r + 1) * BF16_ROWS)
                a = a_cur_ref[rows, :]
                act = 0.5 * a * (1.0 + lax.erf(a * (2.0 ** -0.5)))
                w = jnp.zeros((BF16_ROWS, PEER_TB), bf16)
                for h in range(PEER_HEADS):
                    w = w + jnp.where(rank1_ref[h, keys, :] < cnts[h], e1_ref[h, keys, :], jnp.zeros((), bf16)) * e0s[h]
                p_ref[rows, :] = w * act.astype(bf16)
        acc_ref[...] += _dot_tn(v_ref[...], p_ref[...])

    @pl.when(j % 2 == 0)
    def _():
        step(a_even_ref, a_odd_ref)

    @pl.when(j % 2 == 1)
    def _():
        step(a_odd_ref, a_even_ref)

    @pl.when(j == pl.num_programs(1) - 1)
    def _():
        hs3 = hs_ref[...] + acc_ref[...].T
        o_ref[...] = _rms(hs3, g_ref[...])


def _peer(hnt, u, v, rank1, cnt0, e0, e1, hs2, g_final):
    tb, eb = PEER_TB, PEER_EB
    n_blocks = PEER_N // eb
    sel_spec = pl.BlockSpec((PEER_HEADS, PEER_NKEYS, tb), lambda i, j: (0, 0, i))
    once = pl.Buffered(1)
    return pl.pallas_call(
        _peer_kernel,
        out_shape=jax.ShapeDtypeStruct((TP, D_MODEL), f32),
        grid=(TP // tb, n_blocks),
        in_specs=[pl.BlockSpec((D_MODEL, tb), lambda i, j: (0, i)),
                  pl.BlockSpec((eb, D_MODEL), lambda i, j: (0, 0), pipeline_mode=once),
                  pl.BlockSpec((eb, D_MODEL), lambda i, j: (jnp.minimum(j + 1, n_blocks - 1), 0)),
                  pl.BlockSpec((eb, D_MODEL), lambda i, j: (j, 0)),
                  sel_spec, sel_spec, sel_spec, sel_spec,
                  pl.BlockSpec((tb, D_MODEL), lambda i, j: (i, 0), pipeline_mode=once),
                  pl.BlockSpec((1, D_MODEL), lambda i, j: (0, 0))],
        out_specs=pl.BlockSpec((tb, D_MODEL), lambda i, j: (i, 0)),
        scratch_shapes=[pltpu.VMEM((D_MODEL, tb), f32), pltpu.VMEM((eb, tb), f32), pltpu.VMEM((eb, tb), f32),
                        pltpu.VMEM((eb, tb), bf16)],
        compiler_params=_params("parallel", "arbitrary"),
        name="peer_experts",
    )(hnt, u, u, v, rank1, cnt0, e0, e1, hs2, g_final)


def _rope_tables():
    inv = 1.0 / (ROPE_THETA ** (jnp.arange(ROPE_HALF, dtype=f32) / ROPE_HALF))
    ang = jnp.arange(LP, dtype=jnp.int32).astype(f32)[:, None] * inv[None, :]
    cos, sin = jnp.cos(ang), jnp.sin(ang)
    rest = DIFF_DH - ROPE_DIMS
    zero, zrest = jnp.zeros_like(sin), jnp.zeros((LP, rest), f32)
    return jnp.stack([
        jnp.concatenate([cos, cos, jnp.ones((LP, rest), f32)], axis=1),
        jnp.concatenate([-sin, zero, zrest], axis=1),
        jnp.concatenate([zero, sin, zrest], axis=1)])


def _split_rows(wt):
    parts, start = [], 0
    for n in IN_SIZES:
        parts.append(wt[start:start + n])
        start += n
    return parts


def kernel(x, meta_tokens, g_mix, w_in, gla_w2_fwd, gla_b_fwd, gla_w2_bwd, gla_b_bwd, gla_g_norm, diff_lq1, diff_lk1, diff_lq2, diff_lk2, diff_g_norm, w_branch_gla, w_branch_diff, w_out, g_ffn, peer_w_q, peer_sub_keys, peer_u, peer_v, g_final):
    assert w_in.shape[0] == 1, "single-layer block only"
    l = 0
    lam_init = 0.8 - 0.6 * math.exp(-0.3 * l)
    meta = jnp.broadcast_to(meta_tokens[None].astype(x.dtype), (BATCH, N_META, D_MODEL))
    pad = jnp.zeros((BATCH, LP - L_REAL, D_MODEL), x.dtype)
    hs = jnp.concatenate([meta, x, pad], axis=1).reshape(TP, D_MODEL)
    rope_tab = _rope_tables()

    gq, gk, gv, gr, glr, dq, dk, dv, za, zb = _split_rows(jnp.swapaxes(w_in[l], 0, 1))
    w_a = jnp.concatenate([gq, gk, gv, dq, dk, dv], axis=0).astype(bf16)
    w_b = jnp.concatenate([gr, za, zb], axis=0).astype(bf16)
    w_lr = jnp.pad(glr, ((0, LANE - 2 * GLA_LOWRANK), (0, 0))).astype(bf16)
    w2f = jnp.pad(gla_w2_fwd[l], ((0, LANE - GLA_LOWRANK), (0, 0))).astype(bf16)
    w2b = jnp.pad(gla_w2_bwd[l], ((GLA_LOWRANK, LANE - 2 * GLA_LOWRANK), (0, 0))).astype(bf16)

    h = _norm_rows(hs, g_mix[l][None])
    proj_a = _project(h, w_a, bf16, "in_proj_qkv")
    proj_b = _project(h, w_b, f32, "in_proj_gates")
    logf_f, logf_b = _decay(h, w_lr, w2f, w2b, gla_b_fwd[l][None], gla_b_bwd[l][None])
    o_gla = _gla(proj_a, proj_b, logf_f, logf_b, gla_g_norm[l][None])
    o_diff = _diff_attention(proj_a, rope_tab, diff_lq1[l][None], diff_lk1[l][None], diff_lq2[l][None],
                             diff_lk2[l][None], diff_g_norm[l][None], lam_init)
    hs2, hn, hnt = _merge(o_gla, o_diff, proj_b, hs, w_branch_gla[l].astype(bf16), w_branch_diff[l].astype(bf16),
                     w_out[l].astype(bf16), g_ffn[l][None])
    rank1, cnt0, e0, e1 = _peer_select(hn, peer_w_q[l].astype(bf16), peer_sub_keys[l].astype(bf16))
    out = _peer(hnt, peer_u[l].astype(bf16), peer_v[l].astype(bf16), rank1, cnt0, e0, e1, hs2, g_final[None])
    return out.reshape(BATCH, LP, D_MODEL)[:, N_META:L_REAL]
```

```python
import functools
import math

import jax
import jax.numpy as jnp
from jax import lax
from jax.experimental import pallas as pl
from jax.experimental.pallas import tpu as pltpu

f32 = jnp.float32
bf16 = jnp.bfloat16

D_MODEL = 2048
BATCH = 4
SEQ = 2048
N_META = 16
EPS = 1e-6
LANE = 128
BF16_ROWS = 16
EXTRA = LANE
META_PAD = EXTRA - N_META
TX = BATCH * SEQ
TP = TX + BATCH * EXTRA
L_REAL = SEQ + N_META
LK = SEQ + EXTRA

GLA_HEADS = 4
GLA_DK = 256
GLA_DV = 512
GLA_QK = GLA_HEADS * GLA_DK
GLA_V = GLA_HEADS * GLA_DV
GLA_LOWRANK = 16
GLA_TAU = 16.0
GLA_CHUNK = 64
GLA_UNROLL = 16

DIFF_HEADS = 8
DIFF_DH = 128
DIFF_DV = 256
DIFF_QK = DIFF_HEADS * 2 * DIFF_DH
DIFF_V = DIFF_HEADS * DIFF_DV
ROPE_THETA = 500000.0
ROPE_DIMS = DIFF_DH // 4
ROPE_HALF = ROPE_DIMS // 2

PEER_HEADS = 8
PEER_NKEYS = 128
PEER_N = PEER_NKEYS * PEER_NKEYS
PEER_DKEY = 128
PEER_TOPK = 16

IN_SIZES = (GLA_QK, GLA_QK, GLA_V, GLA_V, 2 * GLA_LOWRANK, DIFF_QK, DIFF_QK, DIFF_V, D_MODEL, D_MODEL)
IN_OFFSETS = tuple(sum(IN_SIZES[:i]) for i in range(len(IN_SIZES)))

VMEM_LIMIT = 56 * 1024 * 1024
NEG_BIG = -1e30
RANK_MARK = 2.0 ** 100

ROW_BLOCK = 512
COL_BLOCK = 2048
DIFF_TQ = 512
MERGE_TM = 256
SEL_TB = 256
PEER_TB = 512
PEER_EB = 512


def _params(*sem):
    return pltpu.CompilerParams(dimension_semantics=sem, vmem_limit_bytes=VMEM_LIMIT)


def _rms(x, g):
    return x * lax.rsqrt(jnp.mean(x * x, axis=-1, keepdims=True) + EPS) * g


def _dot(a, b):
    return jnp.dot(a, b, preferred_element_type=f32)


def _dot_nt(a, b):
    return lax.dot_general(a, b, (((1,), (1,)), ((), ())), preferred_element_type=f32)


def _dot_tn(a, b):
    return lax.dot_general(a, b, (((0,), (0,)), ((), ())), preferred_element_type=f32)


def _norm_kernel(x_ref, e_ref, g_ref, o_ref):
    is_seq = pl.program_id(0) < TX // ROW_BLOCK

    @pl.when(is_seq)
    def _():
        o_ref[...] = _rms(x_ref[...], g_ref[...]).astype(o_ref.dtype)

    @pl.when(jnp.logical_not(is_seq))
    def _():
        o_ref[...] = _rms(e_ref[...], g_ref[...]).astype(o_ref.dtype)


def _norm_rows(x, extras, g):
    assert BATCH * EXTRA == ROW_BLOCK
    last = TX // ROW_BLOCK - 1
    return pl.pallas_call(
        _norm_kernel,
        out_shape=jax.ShapeDtypeStruct((TP, D_MODEL), bf16),
        grid=(TP // ROW_BLOCK,),
        in_specs=[pl.BlockSpec((ROW_BLOCK, D_MODEL), lambda i: (jnp.minimum(i, last), 0)),
                  pl.BlockSpec((ROW_BLOCK, D_MODEL), lambda i: (0, 0)),
                  pl.BlockSpec((1, D_MODEL), lambda i: (0, 0))],
        out_specs=pl.BlockSpec((ROW_BLOCK, D_MODEL), lambda i: (i, 0)),
        compiler_params=_params("arbitrary"),
        name="mix_norm",
    )(x, extras, g)


def _mm_kernel(a_ref, wt_ref, o_ref):
    o_ref[...] = _dot_nt(a_ref[...], wt_ref[...]).astype(o_ref.dtype)


def _project(h, wt, r0, n, rows, out_dtype, name):
    return pl.pallas_call(
        _mm_kernel,
        out_shape=jax.ShapeDtypeStruct((rows, n), out_dtype),
        grid=(n // COL_BLOCK, rows // ROW_BLOCK),
        in_specs=[pl.BlockSpec((ROW_BLOCK, D_MODEL), lambda j, i: (i, 0)),
                  pl.BlockSpec((pl.Element(COL_BLOCK), pl.Element(D_MODEL)),
                               lambda j, i: (pl.multiple_of(r0 + j * COL_BLOCK, BF16_ROWS), 0))],
        out_specs=pl.BlockSpec((ROW_BLOCK, COL_BLOCK), lambda j, i: (i, j)),
        compiler_params=_params("parallel", "parallel"),
        name=name,
    )(h, wt)


def _decay_kernel(h_ref, wlr_ref, w2f_ref, w2b_ref, bf_ref, bb_ref, of_ref, ob_ref):
    lr = _dot_nt(h_ref[...], wlr_ref[...]).astype(bf16)
    zf = _dot(lr, w2f_ref[...]) + bf_ref[...]
    zb = _dot(lr, w2b_ref[...]) + bb_ref[...]
    of_ref[...] = jax.nn.log_sigmoid(zf) * (1.0 / GLA_TAU)
    ob_ref[...] = jax.nn.log_sigmoid(zb) * (1.0 / GLA_TAU)


def _decay(h, wlr, w2f, w2b, b_f, b_b):
    row = lambda i: (i, 0)
    fixed = lambda i: (0, 0)
    return pl.pallas_call(
        _decay_kernel,
        out_shape=(jax.ShapeDtypeStruct((TP, GLA_QK), f32),) * 2,
        grid=(TP // ROW_BLOCK,),
        in_specs=[pl.BlockSpec((ROW_BLOCK, D_MODEL), row),
                  pl.BlockSpec((LANE, D_MODEL), fixed),
                  pl.BlockSpec((LANE, GLA_QK), fixed),
                  pl.BlockSpec((LANE, GLA_QK), fixed),
                  pl.BlockSpec((1, GLA_QK), fixed),
                  pl.BlockSpec((1, GLA_QK), fixed)],
        out_specs=(pl.BlockSpec((ROW_BLOCK, GLA_QK), row),) * 2,
        compiler_params=_params("parallel"),
        name="gla_decay",
    )(h, wlr, w2f, w2b, b_f, b_b)


def _gla_kernel(q_ref, k_ref, v_ref, ke_ref, ve_ref, ff_ref, fb_ref, fe_ref, r_ref, g_ref, o_ref,
                accf_ref, accb_ref, sf_ref, sb_ref):
    C = GLA_CHUNK
    n_chunks = SEQ // C
    row = lax.broadcasted_iota(jnp.int32, (C, C), 0)
    col = lax.broadcasted_iota(jnp.int32, (C, C), 1)
    time = lax.broadcasted_iota(jnp.int32, (C, GLA_DK), 0)

    def decay_sums(lf, prefix):
        cum = lf
        shift = 1
        while shift < C:
            if prefix:
                cum = cum + jnp.where(time >= shift, pltpu.roll(cum, shift, axis=0), 0.0)
            else:
                cum = cum + jnp.where(time < C - shift, pltpu.roll(cum, C - shift, axis=0), 0.0)
            shift *= 2
        total_row = C - 1 if prefix else 0
        return cum, cum[total_row:total_row + 1, :]

    def advance(s_ref, k, v, cum, tot):
        tot_col = jnp.broadcast_to(tot, (8, GLA_DK)).T[:, 0:1]
        k_st = (k * jnp.exp(tot - cum)).astype(bf16)
        s_ref[...] = s_ref[...] * jnp.exp(tot_col) + _dot_tn(k_st, v)

    def chunk(n, f_ref, prefix, keep_mask, acc_ref, s_ref):
        rows = pl.ds(pl.multiple_of(n * C, C), C)
        cum, tot = decay_sums(f_ref[rows, :], prefix)
        q = q_ref[rows, :].astype(f32) * (GLA_DK ** -0.5)
        k = k_ref[rows, :].astype(f32)
        v = v_ref[rows, :]
        q_in = (q * jnp.exp(cum)).astype(bf16)
        k_in = (k * jnp.exp(-cum)).astype(bf16)
        a = jnp.where(keep_mask, _dot_nt(q_in, k_in), 0.0)
        acc_ref[rows, :] = _dot(a.astype(bf16), v) + _dot(q_in, s_ref[...].astype(bf16))
        advance(s_ref, k, v, cum, tot)

    sf_ref[...] = jnp.zeros_like(sf_ref)
    sb_ref[...] = jnp.zeros_like(sb_ref)

    meta_rows = slice(EXTRA - C, EXTRA)
    cum_e, tot_e = decay_sums(fe_ref[meta_rows, :], True)
    advance(sf_ref, ke_ref[meta_rows, :].astype(f32), ve_ref[meta_rows, :], cum_e, tot_e)

    def both(m, carry):
        for u in range(GLA_UNROLL):
            n = m * GLA_UNROLL + u
            chunk(n, ff_ref, True, col <= row, accf_ref, sf_ref)
            chunk(n_chunks - 1 - n, fb_ref, False, col > row, accb_ref, sb_ref)
        return carry

    lax.fori_loop(0, n_chunks // GLA_UNROLL, both, 0)

    def fin(n, carry):
        rows = pl.ds(pl.multiple_of(n * LANE, LANE), LANE)
        o = _rms(accf_ref[rows, :] + accb_ref[rows, :], g_ref[...])
        o_ref[rows, :] = (o * jax.nn.silu(r_ref[rows, :])).astype(o_ref.dtype)
        return carry

    lax.fori_loop(0, SEQ // LANE, fin, 0)


def _gla(proj, gate, logf_f, logf_b, g_norm):
    kq = GLA_QK // GLA_DK
    kv = (2 * GLA_QK) // GLA_DV
    ex = TX // EXTRA
    return pl.pallas_call(
        _gla_kernel,
        out_shape=jax.ShapeDtypeStruct((TX, GLA_V), bf16),
        grid=(BATCH, GLA_HEADS),
        in_specs=[pl.BlockSpec((SEQ, GLA_DK), lambda b, h: (b, h)),
                  pl.BlockSpec((SEQ, GLA_DK), lambda b, h: (b, kq + h)),
                  pl.BlockSpec((SEQ, GLA_DV), lambda b, h: (b, kv + h)),
                  pl.BlockSpec((EXTRA, GLA_DK), lambda b, h: (ex + b, kq + h)),
                  pl.BlockSpec((EXTRA, GLA_DV), lambda b, h: (ex + b, kv + h)),
                  pl.BlockSpec((SEQ, GLA_DK), lambda b, h: (b, h)),
                  pl.BlockSpec((SEQ, GLA_DK), lambda b, h: (b, h)),
                  pl.BlockSpec((EXTRA, GLA_DK), lambda b, h: (ex + b, h)),
                  pl.BlockSpec((SEQ, GLA_DV), lambda b, h: (b, h)),
                  pl.BlockSpec((1, GLA_DV), lambda b, h: (0, 0))],
        out_specs=pl.BlockSpec((SEQ, GLA_DV), lambda b, h: (b, h)),
        scratch_shapes=[pltpu.VMEM((SEQ, GLA_DV), f32), pltpu.VMEM((SEQ, GLA_DV), f32),
                        pltpu.VMEM((GLA_DK, GLA_DV), f32), pltpu.VMEM((GLA_DK, GLA_DV), f32)],
        compiler_params=_params("parallel", "parallel"),
        name="gla_mixer",
    )(proj, proj, proj, proj, proj, logf_f, logf_b, logf_f, gate, g_norm)


def _rope(x, tab):
    return (x * tab[0]
            + pltpu.roll(x, LANE - ROPE_HALF, axis=1) * tab[1]
            + pltpu.roll(x, ROPE_HALF, axis=1) * tab[2])


def _diff_kernel(lam_init, q_ref, k_ref, ke_ref, v_ref, ve_ref, tq_ref, tk_ref, lq1_ref, lk1_ref, lq2_ref, lk2_ref,
                 g_ref, o_ref, kr_ref, vr_ref):
    @pl.when(pl.program_id(2) == 0)
    def _():
        meta = slice(META_PAD, EXTRA)
        zeros = slice(0, META_PAD)
        for m in range(2):
            cols = slice(m * DIFF_DH, (m + 1) * DIFF_DH)
            kr_ref[m, 0:SEQ, :] = _rope(k_ref[:, cols].astype(f32), tk_ref[:, 0:SEQ, :]).astype(bf16)
            kr_ref[m, SEQ:L_REAL, :] = _rope(ke_ref[meta, cols].astype(f32), tk_ref[:, SEQ:L_REAL, :]).astype(bf16)
            kr_ref[m, L_REAL:LK, :] = ke_ref[zeros, cols]
        vr_ref[0:SEQ, :] = v_ref[...]
        vr_ref[SEQ:L_REAL, :] = ve_ref[meta, :]
        vr_ref[L_REAL:LK, :] = ve_ref[zeros, :]

    lam = (jnp.exp(jnp.sum(lq1_ref[...] * lk1_ref[...], axis=1, keepdims=True))
           - jnp.exp(jnp.sum(lq2_ref[...] * lk2_ref[...], axis=1, keepdims=True)) + lam_init)
    tq = tq_ref[...]
    key_ok = lax.broadcasted_iota(jnp.int32, (DIFF_TQ, LK), 1) < L_REAL
    exp2_scale = (DIFF_DH ** -0.5) * math.log2(math.e)
    outs = []
    for m in range(2):
        qx = q_ref[:, m * DIFF_DH:(m + 1) * DIFF_DH].astype(f32)
        qr = _rope(qx, tq).astype(bf16)
        s = jnp.where(key_ok, _dot_nt(qr, kr_ref[m]), NEG_BIG)
        e = jnp.exp2((s - jnp.max(s, axis=1, keepdims=True)) * exp2_scale)
        inv = 1.0 / jnp.sum(e, axis=1, keepdims=True)
        outs.append(_dot(e.astype(bf16), vr_ref[...]) * inv)
    o = outs[0] - lam * outs[1]
    o_ref[...] = (_rms(o, g_ref[...]) * (1.0 - lam_init)).astype(o_ref.dtype)


def _diff_attention(proj, rope_tab, lq1, lk1, lq2, lk2, g_norm, lam_init):
    nq = SEQ // DIFF_TQ
    ck = DIFF_QK // DIFF_DV
    cv = 2 * ck
    ex = TX // EXTRA
    vec = pl.BlockSpec((1, DIFF_DH), lambda b, h, i: (0, 0))
    return pl.pallas_call(
        functools.partial(_diff_kernel, lam_init),
        out_shape=jax.ShapeDtypeStruct((TX, DIFF_V), bf16),
        grid=(BATCH, DIFF_HEADS, nq),
        in_specs=[pl.BlockSpec((DIFF_TQ, DIFF_DV), lambda b, h, i: (b * nq + i, h)),
                  pl.BlockSpec((SEQ, DIFF_DV), lambda b, h, i: (b, ck + h)),
                  pl.BlockSpec((EXTRA, DIFF_DV), lambda b, h, i: (ex + b, ck + h)),
                  pl.BlockSpec((SEQ, DIFF_DV), lambda b, h, i: (b, cv + h)),
                  pl.BlockSpec((EXTRA, DIFF_DV), lambda b, h, i: (ex + b, cv + h)),
                  pl.BlockSpec((3, DIFF_TQ, DIFF_DH), lambda b, h, i: (0, i, 0)),
                  pl.BlockSpec((3, LK, DIFF_DH), lambda b, h, i: (0, 0, 0)),
                  vec, vec, vec, vec,
                  pl.BlockSpec((1, DIFF_DV), lambda b, h, i: (0, 0))],
        out_specs=pl.BlockSpec((DIFF_TQ, DIFF_DV), lambda b, h, i: (b * nq + i, h)),
        scratch_shapes=[pltpu.VMEM((2, LK, DIFF_DH), bf16), pltpu.VMEM((LK, DIFF_DV), bf16)],
        compiler_params=_params("parallel", "parallel", "arbitrary"),
        name="diff_attention",
    )(proj, proj, proj, proj, proj, rope_tab, rope_tab, lq1, lk1, lq2, lk2, g_norm)


def _merge_kernel(og_ref, od_ref, za_ref, zb_ref, hs_ref, wa_ref, wb_ref, wo_ref, g_ref, hs2_ref, hn_ref, hnt_ref):
    y = (jax.nn.sigmoid(za_ref[...]) * _dot(og_ref[...], wa_ref[...])
         + jax.nn.sigmoid(zb_ref[...]) * _dot(od_ref[...], wb_ref[...]))
    hs2 = hs_ref[...] + _dot(y.astype(bf16), wo_ref[...])
    hs2_ref[...] = hs2
    hn = _rms(hs2, g_ref[...])
    hn_ref[...] = hn.astype(hn_ref.dtype)
    hnt_ref[...] = hn.T.astype(hnt_ref.dtype)


def _merge(o_gla, o_diff, zab, hs, wa, wb, wo, g_ffn):
    tm = MERGE_TM
    row = lambda i: (i, 0)
    fixed = lambda i: (0, 0)
    wspec = pl.BlockSpec((D_MODEL, D_MODEL), fixed, pipeline_mode=pl.Buffered(1))
    return pl.pallas_call(
        _merge_kernel,
        out_shape=(jax.ShapeDtypeStruct((TX, D_MODEL), f32), jax.ShapeDtypeStruct((TX, D_MODEL), bf16),
                   jax.ShapeDtypeStruct((D_MODEL, TX), bf16)),
        grid=(TX // tm,),
        in_specs=[pl.BlockSpec((tm, GLA_V), row),
                  pl.BlockSpec((tm, DIFF_V), row),
                  pl.BlockSpec((tm, D_MODEL), lambda i: (i, 0)),
                  pl.BlockSpec((tm, D_MODEL), lambda i: (i, 1)),
                  pl.BlockSpec((tm, D_MODEL), row),
                  wspec, wspec, wspec,
                  pl.BlockSpec((1, D_MODEL), fixed)],
        out_specs=(pl.BlockSpec((tm, D_MODEL), row), pl.BlockSpec((tm, D_MODEL), row),
                   pl.BlockSpec((D_MODEL, tm), lambda i: (0, i))),
        compiler_params=_params("parallel"),
        name="branch_merge",
    )(o_gla, o_diff, zab, zab, hs, wa, wb, wo, g_ffn)


def _top16(s, iota, break_ties):
    vals = []
    if not break_ties:
        for r in range(PEER_TOPK):
            m = jnp.max(s, axis=0, keepdims=True)
            s = jnp.where(s == m, -RANK_MARK * (r + 1), s)
            vals.append(m)
        rank = jnp.where(s <= -RANK_MARK, s * (-1.0 / RANK_MARK) - 1.0, float(PEER_TOPK))
        return vals, rank
    rank = jnp.full(s.shape, float(PEER_TOPK), f32)
    for r in range(PEER_TOPK):
        m = jnp.max(s, axis=0, keepdims=True)
        first = jnp.min(jnp.where(s == m, iota, float(PEER_NKEYS)), axis=0, keepdims=True)
        hit = iota == first
        rank = jnp.where(hit, float(r), rank)
        s = jnp.where(hit, -jnp.inf, s)
        vals.append(m)
    return vals, rank


def _select_head(s0, s1, iota, ids, break_ties):
    K = PEER_TOPK
    tb = s0.shape[1]
    v0, rank0 = _top16(s0, iota, break_ties)
    v1, rank1 = _top16(s1, iota, break_ties)
    sa = jnp.concatenate(v0, axis=0)
    sb = jnp.concatenate(v1, axis=0)
    cand = jnp.concatenate([sa[0:1] + sb]
                           + [sa[a:a + 1] + sb[0:8] for a in range(1, 8)]
                           + [sa[8:16] + sb[0:1]], axis=0)
    top = cand[0:1]
    z = jnp.zeros((1, tb), f32)
    for _ in range(K):
        m = jnp.max(cand, axis=0, keepdims=True)
        hit = cand == m
        if break_ties:
            first = jnp.min(jnp.where(hit, ids, 1e9), axis=0, keepdims=True)
            hit = ids == first
        cand = jnp.where(hit, -jnp.inf, cand)
        z = z + jnp.exp(m - top)
    taken = jnp.where(cand == -jnp.inf, 1.0, 0.0)
    cnt = ([jnp.sum(taken[0:16], axis=0, keepdims=True)]
           + [jnp.sum(taken[8 + 8 * a:16 + 8 * a], axis=0, keepdims=True) for a in range(1, 8)]
           + [taken[72 + a:73 + a] for a in range(8)])
    cnt0 = jnp.zeros((PEER_NKEYS, tb), f32)
    for a in range(K):
        cnt0 = jnp.where(rank0 == float(a), cnt[a], cnt0)
    e0 = jnp.exp(s0 - v0[0]) * (1.0 / z)
    e1 = jnp.exp(s1 - v1[0])
    marked = (jnp.sum((rank0 < float(K)).astype(f32), axis=0, keepdims=True),
              jnp.sum((rank1 < float(K)).astype(f32), axis=0, keepdims=True),
              jnp.sum(taken, axis=0, keepdims=True))
    excess = jnp.max(sum(jnp.abs(n - float(K)) for n in marked))
    return (rank1, cnt0, e0, e1), excess


def _peer_select_kernel(hn_ref, wq_ref, keys_ref, rank1_ref, cnt0_ref, e0_ref, e1_ref, q_sc):
    tb = SEL_TB
    K = PEER_TOPK
    q_sc[...] = _dot(hn_ref[...], wq_ref[...]).astype(bf16)
    iota = lax.broadcasted_iota(jnp.int32, (PEER_NKEYS, tb), 0).astype(f32)
    i16 = lax.broadcasted_iota(jnp.int32, (K, tb), 0).astype(f32)
    i8 = lax.broadcasted_iota(jnp.int32, (8, tb), 0).astype(f32)
    ids = jnp.concatenate([i16] + [a * float(K) + i8 for a in range(1, 8)] + [(i8 + 8.0) * float(K)], axis=0)

    def head(h, carry):
        c0 = pl.multiple_of(h * (2 * PEER_DKEY), 2 * PEER_DKEY)
        s0 = _dot_nt(keys_ref[h, 0], q_sc[:, pl.ds(c0, PEER_DKEY)])
        s1 = _dot_nt(keys_ref[h, 1], q_sc[:, pl.ds(c0 + PEER_DKEY, PEER_DKEY)])

        def store(rank1, cnt0, e0, e1):
            rank1_ref[h] = rank1.astype(bf16)
            cnt0_ref[h] = cnt0
            e0_ref[h] = e0
            e1_ref[h] = e1.astype(bf16)

        tables, excess = _select_head(s0, s1, iota, ids, False)
        store(*tables)

        @pl.when(excess > 0.0)
        def _():
            store(*_select_head(s0, s1, iota, ids, True)[0])

        return carry

    lax.fori_loop(0, PEER_HEADS, head, 0)


def _peer_select(hn, wq, keys):
    tb = SEL_TB
    sel = lambda dt: jax.ShapeDtypeStruct((PEER_HEADS, PEER_NKEYS, TX), dt)
    sel_spec = pl.BlockSpec((PEER_HEADS, PEER_NKEYS, tb), lambda i: (0, 0, i))
    return pl.pallas_call(
        _peer_select_kernel,
        out_shape=(sel(bf16), sel(f32), sel(f32), sel(bf16)),
        grid=(TX // tb,),
        in_specs=[pl.BlockSpec((tb, D_MODEL), lambda i: (i, 0)),
                  pl.BlockSpec((D_MODEL, PEER_HEADS * 2 * PEER_DKEY), lambda i: (0, 0)),
                  pl.BlockSpec((PEER_HEADS, 2, PEER_NKEYS, PEER_DKEY), lambda i: (0, 0, 0, 0))],
        out_specs=(sel_spec,) * 4,
        scratch_shapes=[pltpu.VMEM((tb, PEER_HEADS * 2 * PEER_DKEY), bf16)],
        compiler_params=_params("parallel"),
        name="peer_select",
    )(hn, wq, keys)


def _peer_kernel(hnt_ref, u0_ref, un_ref, v_ref, rank1_ref, cnt0_ref, e0_ref, e1_ref, hs_ref, g_ref, o_ref,
                 acc_ref, a_even_ref, a_odd_ref, p_ref):
    j = pl.program_id(1)
    groups = PEER_EB // PEER_NKEYS

    @pl.when(j == 0)
    def _():
        acc_ref[...] = jnp.zeros_like(acc_ref)
        a_even_ref[...] = _dot(u0_ref[...], hnt_ref[...])

    def step(a_cur_ref, a_next_ref):
        a_next_ref[...] = _dot(un_ref[...], hnt_ref[...])
        for gi in range(groups):
            i = j * groups + gi
            cnts = [jnp.broadcast_to(cnt0_ref[h, pl.ds(i, 1), :], (BF16_ROWS, PEER_TB)).astype(bf16)
                    for h in range(PEER_HEADS)]
            e0s = [jnp.broadcast_to(e0_ref[h, pl.ds(i, 1), :], (BF16_ROWS, PEER_TB)).astype(bf16)
                   for h in range(PEER_HEADS)]
            for r in range(PEER_NKEYS // BF16_ROWS):
                keys = slice(r * BF16_ROWS, (r + 1) * BF16_ROWS)
                rows = slice(gi * PEER_NKEYS + r * BF16_ROWS, gi * PEER_NKEYS + (r + 1) * BF16_ROWS)
                a = a_cur_ref[rows, :]
                act = 0.5 * a * (1.0 + lax.erf(a * (2.0 ** -0.5)))
                w = jnp.zeros((BF16_ROWS, PEER_TB), bf16)
                for h in range(PEER_HEADS):
                    w = w + jnp.where(rank1_ref[h, keys, :] < cnts[h], e1_ref[h, keys, :], jnp.zeros((), bf16)) * e0s[h]
                p_ref[rows, :] = w * act.astype(bf16)
        acc_ref[...] += _dot_tn(v_ref[...], p_ref[...])

    @pl.when(j % 2 == 0)
    def _():
        step(a_even_ref, a_odd_ref)

    @pl.when(j % 2 == 1)
    def _():
        step(a_odd_ref, a_even_ref)

    @pl.when(j == pl.num_programs(1) - 1)
    def _():
        hs3 = hs_ref[...] + acc_ref[...].T
        o_ref[...] = _rms(hs3, g_ref[...])


def _peer(hnt, u, v, rank1, cnt0, e0, e1, hs2, g_final):
    tb, eb = PEER_TB, PEER_EB
    n_blocks = PEER_N // eb
    sel_spec = pl.BlockSpec((PEER_HEADS, PEER_NKEYS, tb), lambda i, j: (0, 0, i))
    once = pl.Buffered(1)
    return pl.pallas_call(
        _peer_kernel,
        out_shape=jax.ShapeDtypeStruct((TX, D_MODEL), f32),
        grid=(TX // tb, n_blocks),
        in_specs=[pl.BlockSpec((D_MODEL, tb), lambda i, j: (0, i)),
                  pl.BlockSpec((eb, D_MODEL), lambda i, j: (0, 0), pipeline_mode=once),
                  pl.BlockSpec((eb, D_MODEL), lambda i, j: (jnp.minimum(j + 1, n_blocks - 1), 0)),
                  pl.BlockSpec((eb, D_MODEL), lambda i, j: (j, 0)),
                  sel_spec, sel_spec, sel_spec, sel_spec,
                  pl.BlockSpec((tb, D_MODEL), lambda i, j: (i, 0), pipeline_mode=once),
                  pl.BlockSpec((1, D_MODEL), lambda i, j: (0, 0))],
        out_specs=pl.BlockSpec((tb, D_MODEL), lambda i, j: (i, 0)),
        scratch_shapes=[pltpu.VMEM((D_MODEL, tb), f32), pltpu.VMEM((eb, tb), f32), pltpu.VMEM((eb, tb), f32),
                        pltpu.VMEM((eb, tb), bf16)],
        compiler_params=_params("parallel", "arbitrary"),
        name="peer_experts",
    )(hnt, u, u, v, rank1, cnt0, e0, e1, hs2, g_final)


def _rope_tables():
    pos = jnp.concatenate([jnp.arange(N_META, N_META + SEQ, dtype=jnp.int32),
                           jnp.arange(N_META, dtype=jnp.int32),
                           jnp.zeros((META_PAD,), jnp.int32)])
    inv = 1.0 / (ROPE_THETA ** (jnp.arange(ROPE_HALF, dtype=f32) / ROPE_HALF))
    ang = pos.astype(f32)[:, None] * inv[None, :]
    cos, sin = jnp.cos(ang), jnp.sin(ang)
    rest = DIFF_DH - ROPE_DIMS
    zero, zrest = jnp.zeros_like(sin), jnp.zeros((LK, rest), f32)
    return jnp.stack([
        jnp.concatenate([cos, cos, jnp.ones((LK, rest), f32)], axis=1),
        jnp.concatenate([-sin, zero, zrest], axis=1),
        jnp.concatenate([zero, sin, zrest], axis=1)])


def kernel(x, meta_tokens, g_mix, w_in, gla_w2_fwd, gla_b_fwd, gla_w2_bwd, gla_b_bwd, gla_g_norm, diff_lq1, diff_lk1, diff_lq2, diff_lk2, diff_g_norm, w_branch_gla, w_branch_diff, w_out, g_ffn, peer_w_q, peer_sub_keys, peer_u, peer_v, g_final):
    assert w_in.shape[0] == 1, "single-layer block only"
    l = 0
    lam_init = 0.8 - 0.6 * math.exp(-0.3 * l)
    xs = x.reshape(TX, D_MODEL)
    extra = jnp.concatenate([jnp.zeros((META_PAD, D_MODEL), x.dtype), meta_tokens.astype(x.dtype)], axis=0)
    extras = jnp.tile(extra, (BATCH, 1))
    rope_tab = _rope_tables()

    wt = jnp.swapaxes(w_in[l], 0, 1).astype(bf16)
    o_gq, o_gk, o_gv, o_gr, o_lr, o_dq, o_dk, o_dv, o_za, o_zb = IN_OFFSETS
    w_lr = jnp.pad(wt[o_lr:o_lr + 2 * GLA_LOWRANK], ((0, LANE - 2 * GLA_LOWRANK), (0, 0)))
    w2f = jnp.pad(gla_w2_fwd[l], ((0, LANE - GLA_LOWRANK), (0, 0))).astype(bf16)
    w2b = jnp.pad(gla_w2_bwd[l], ((GLA_LOWRANK, LANE - 2 * GLA_LOWRANK), (0, 0))).astype(bf16)

    h = _norm_rows(xs, extras, g_mix[l][None])
    proj_gla = _project(h, wt, o_gq, 2 * GLA_QK + GLA_V, TP, bf16, "in_proj_gla")
    proj_diff = _project(h, wt, o_dq, 2 * DIFF_QK + DIFF_V, TP, bf16, "in_proj_diff")
    gate_gla = _project(h, wt, o_gr, GLA_V, TX, f32, "in_proj_gla_gate")
    gate_merge = _project(h, wt, o_za, 2 * D_MODEL, TX, f32, "in_proj_merge_gates")
    logf_f, logf_b = _decay(h, w_lr, w2f, w2b, gla_b_fwd[l][None], gla_b_bwd[l][None])
    o_gla = _gla(proj_gla, gate_gla, logf_f, logf_b, gla_g_norm[l][None])
    o_diff = _diff_attention(proj_diff, rope_tab, diff_lq1[l][None], diff_lk1[l][None], diff_lq2[l][None],
                             diff_lk2[l][None], diff_g_norm[l][None], lam_init)
    hs2, hn, hnt = _merge(o_gla, o_diff, gate_merge, xs, w_branch_gla[l].astype(bf16),
                          w_branch_diff[l].astype(bf16), w_out[l].astype(bf16), g_ffn[l][None])
    rank1, cnt0, e0, e1 = _peer_select(hn, peer_w_q[l].astype(bf16), peer_sub_keys[l].astype(bf16))
    out = _peer(hnt, peer_u[l].astype(bf16), peer_v[l].astype(bf16), rank1, cnt0, e0, e1, hs2, g_final[None])
    return out.reshape(BATCH, SEQ, D_MODEL)
```

```python
import functools
import math

import jax
import jax.numpy as jnp
from jax import lax
from jax.experimental import pallas as pl
from jax.experimental.pallas import tpu as pltpu

f32 = jnp.float32
bf16 = jnp.bfloat16

D_MODEL = 2048
BATCH = 4
SEQ = 2048
N_META = 16
EPS = 1e-6
LANE = 128
BF16_ROWS = 16
EXTRA = LANE
META_PAD = EXTRA - N_META
TX = BATCH * SEQ
TP = TX + BATCH * EXTRA
L_REAL = SEQ + N_META
LK = SEQ + EXTRA

GLA_HEADS = 4
GLA_DK = 256
GLA_DV = 512
GLA_QK = GLA_HEADS * GLA_DK
GLA_V = GLA_HEADS * GLA_DV
GLA_LOWRANK = 16
GLA_TAU = 16.0
GLA_CHUNK = 64
GLA_UNROLL = 16

DIFF_HEADS = 8
DIFF_DH = 128
DIFF_DV = 256
DIFF_QK = DIFF_HEADS * 2 * DIFF_DH
DIFF_V = DIFF_HEADS * DIFF_DV
ROPE_THETA = 500000.0
ROPE_DIMS = DIFF_DH // 4
ROPE_HALF = ROPE_DIMS // 2

PEER_HEADS = 8
PEER_NKEYS = 128
PEER_N = PEER_NKEYS * PEER_NKEYS
PEER_DKEY = 128
PEER_TOPK = 16

IN_SIZES = (GLA_QK, GLA_QK, GLA_V, GLA_V, 2 * GLA_LOWRANK, DIFF_QK, DIFF_QK, DIFF_V, D_MODEL, D_MODEL)
IN_OFFSETS = tuple(sum(IN_SIZES[:i]) for i in range(len(IN_SIZES)))

VMEM_LIMIT = 56 * 1024 * 1024
NEG_BIG = -1e30
RANK_MARK = 2.0 ** 100

ROW_BLOCK = 512
COL_BLOCK = 2048
DIFF_TQ = 512
MERGE_TM = 256
SEL_TB = 256
PEER_TB = 512
PEER_EB = 1024


def _params(*sem):
    return pltpu.CompilerParams(dimension_semantics=sem, vmem_limit_bytes=VMEM_LIMIT)


def _rms(x, g):
    return x * lax.rsqrt(jnp.mean(x * x, axis=-1, keepdims=True) + EPS) * g


def _dot(a, b):
    return jnp.dot(a, b, preferred_element_type=f32)


def _dot_nt(a, b):
    return lax.dot_general(a, b, (((1,), (1,)), ((), ())), preferred_element_type=f32)


def _dot_tn(a, b):
    return lax.dot_general(a, b, (((0,), (0,)), ((), ())), preferred_element_type=f32)


def _norm_kernel(x_ref, e_ref, g_ref, o_ref):
    is_seq = pl.program_id(0) < TX // ROW_BLOCK

    @pl.when(is_seq)
    def _():
        o_ref[...] = _rms(x_ref[...], g_ref[...]).astype(o_ref.dtype)

    @pl.when(jnp.logical_not(is_seq))
    def _():
        o_ref[...] = _rms(e_ref[...], g_ref[...]).astype(o_ref.dtype)


def _norm_rows(x, extras, g):
    assert BATCH * EXTRA == ROW_BLOCK
    last = TX // ROW_BLOCK - 1
    return pl.pallas_call(
        _norm_kernel,
        out_shape=jax.ShapeDtypeStruct((TP, D_MODEL), bf16),
        grid=(TP // ROW_BLOCK,),
        in_specs=[pl.BlockSpec((ROW_BLOCK, D_MODEL), lambda i: (jnp.minimum(i, last), 0)),
                  pl.BlockSpec((ROW_BLOCK, D_MODEL), lambda i: (0, 0)),
                  pl.BlockSpec((1, D_MODEL), lambda i: (0, 0))],
        out_specs=pl.BlockSpec((ROW_BLOCK, D_MODEL), lambda i: (i, 0)),
        compiler_params=_params("arbitrary"),
        name="mix_norm",
    )(x, extras, g)


def _mm_kernel(a_ref, wt_ref, o_ref):
    o_ref[...] = _dot_nt(a_ref[...], wt_ref[...]).astype(o_ref.dtype)


def _project(h, wt, r0, n, rows, out_dtype, name):
    assert r0 % BF16_ROWS == 0 and n % COL_BLOCK == 0 and rows % ROW_BLOCK == 0
    return pl.pallas_call(
        _mm_kernel,
        out_shape=jax.ShapeDtypeStruct((rows, n), out_dtype),
        grid=(n // COL_BLOCK, rows // ROW_BLOCK),
        in_specs=[pl.BlockSpec((ROW_BLOCK, D_MODEL), lambda j, i: (i, 0)),
                  pl.BlockSpec((pl.Element(COL_BLOCK), pl.Element(D_MODEL)),
                               lambda j, i: (pl.multiple_of(r0 + j * COL_BLOCK, BF16_ROWS), 0))],
        out_specs=pl.BlockSpec((ROW_BLOCK, COL_BLOCK), lambda j, i: (i, j)),
        compiler_params=_params("parallel", "parallel"),
        name=name,
    )(h, wt)


def _decay_kernel(h_ref, wlr_ref, w2f_ref, w2b_ref, bf_ref, bb_ref, of_ref, ob_ref):
    lr = _dot_nt(h_ref[...], wlr_ref[...]).astype(bf16)
    zf = _dot(lr, w2f_ref[...]) + bf_ref[...]
    zb = _dot(lr, w2b_ref[...]) + bb_ref[...]
    of_ref[...] = jax.nn.log_sigmoid(zf) * (1.0 / GLA_TAU)
    ob_ref[...] = jax.nn.log_sigmoid(zb) * (1.0 / GLA_TAU)


def _decay(h, wlr, w2f, w2b, b_f, b_b):
    row = lambda i: (i, 0)
    fixed = lambda i: (0, 0)
    return pl.pallas_call(
        _decay_kernel,
        out_shape=(jax.ShapeDtypeStruct((TP, GLA_QK), f32),) * 2,
        grid=(TP // ROW_BLOCK,),
        in_specs=[pl.BlockSpec((ROW_BLOCK, D_MODEL), row),
                  pl.BlockSpec((LANE, D_MODEL), fixed),
                  pl.BlockSpec((LANE, GLA_QK), fixed),
                  pl.BlockSpec((LANE, GLA_QK), fixed),
                  pl.BlockSpec((1, GLA_QK), fixed),
                  pl.BlockSpec((1, GLA_QK), fixed)],
        out_specs=(pl.BlockSpec((ROW_BLOCK, GLA_QK), row),) * 2,
        compiler_params=_params("parallel"),
        name="gla_decay",
    )(h, wlr, w2f, w2b, b_f, b_b)


def _gla_kernel(q_ref, k_ref, v_ref, ke_ref, ve_ref, ff_ref, fb_ref, fe_ref, r_ref, g_ref, o_ref,
                accf_ref, accb_ref, sf_ref, sb_ref):
    C = GLA_CHUNK
    n_chunks = SEQ // C
    row = lax.broadcasted_iota(jnp.int32, (C, C), 0)
    col = lax.broadcasted_iota(jnp.int32, (C, C), 1)
    time = lax.broadcasted_iota(jnp.int32, (C, GLA_DK), 0)

    def decay_sums(lf, prefix):
        cum = lf
        shift = 1
        while shift < C:
            if prefix:
                cum = cum + jnp.where(time >= shift, pltpu.roll(cum, shift, axis=0), 0.0)
            else:
                cum = cum + jnp.where(time < C - shift, pltpu.roll(cum, C - shift, axis=0), 0.0)
            shift *= 2
        total_row = C - 1 if prefix else 0
        return cum, cum[total_row:total_row + 1, :]

    def advance(s_ref, k, v, cum, tot):
        tot_col = jnp.broadcast_to(tot, (8, GLA_DK)).T[:, 0:1]
        k_st = (k * jnp.exp(tot - cum)).astype(bf16)
        s_ref[...] = s_ref[...] * jnp.exp(tot_col) + _dot_tn(k_st, v)

    def chunk(n, f_ref, prefix, keep_mask, acc_ref, s_ref):
        rows = pl.ds(pl.multiple_of(n * C, C), C)
        cum, tot = decay_sums(f_ref[rows, :], prefix)
        q = q_ref[rows, :].astype(f32) * (GLA_DK ** -0.5)
        k = k_ref[rows, :].astype(f32)
        v = v_ref[rows, :]
        q_in = (q * jnp.exp(cum)).astype(bf16)
        k_in = (k * jnp.exp(-cum)).astype(bf16)
        a = jnp.where(keep_mask, _dot_nt(q_in, k_in), 0.0)
        acc_ref[rows, :] = _dot(a.astype(bf16), v) + _dot(q_in, s_ref[...].astype(bf16))
        advance(s_ref, k, v, cum, tot)

    sf_ref[...] = jnp.zeros_like(sf_ref)
    sb_ref[...] = jnp.zeros_like(sb_ref)

    meta_rows = slice(EXTRA - C, EXTRA)
    cum_e, tot_e = decay_sums(fe_ref[meta_rows, :], True)
    advance(sf_ref, ke_ref[meta_rows, :].astype(f32), ve_ref[meta_rows, :], cum_e, tot_e)

    def both(m, carry):
        for u in range(GLA_UNROLL):
            n = m * GLA_UNROLL + u
            chunk(n, ff_ref, True, col <= row, accf_ref, sf_ref)
            chunk(n_chunks - 1 - n, fb_ref, False, col > row, accb_ref, sb_ref)
        return carry

    lax.fori_loop(0, n_chunks // GLA_UNROLL, both, 0)

    def fin(n, carry):
        rows = pl.ds(pl.multiple_of(n * LANE, LANE), LANE)
        o = _rms(accf_ref[rows, :] + accb_ref[rows, :], g_ref[...])
        o_ref[rows, :] = (o * jax.nn.silu(r_ref[rows, :])).astype(o_ref.dtype)
        return carry

    lax.fori_loop(0, SEQ // LANE, fin, 0)


def _gla(proj, gate, logf_f, logf_b, g_norm):
    kq = GLA_QK // GLA_DK
    kv = (2 * GLA_QK) // GLA_DV
    ex = TX // EXTRA
    return pl.pallas_call(
        _gla_kernel,
        out_shape=jax.ShapeDtypeStruct((TX, GLA_V), bf16),
        grid=(BATCH, GLA_HEADS),
        in_specs=[pl.BlockSpec((SEQ, GLA_DK), lambda b, h: (b, h)),
                  pl.BlockSpec((SEQ, GLA_DK), lambda b, h: (b, kq + h)),
                  pl.BlockSpec((SEQ, GLA_DV), lambda b, h: (b, kv + h)),
                  pl.BlockSpec((EXTRA, GLA_DK), lambda b, h: (ex + b, kq + h)),
                  pl.BlockSpec((EXTRA, GLA_DV), lambda b, h: (ex + b, kv + h)),
                  pl.BlockSpec((SEQ, GLA_DK), lambda b, h: (b, h)),
                  pl.BlockSpec((SEQ, GLA_DK), lambda b, h: (b, h)),
                  pl.BlockSpec((EXTRA, GLA_DK), lambda b, h: (ex + b, h)),
                  pl.BlockSpec((SEQ, GLA_DV), lambda b, h: (b, h)),
                  pl.BlockSpec((1, GLA_DV), lambda b, h: (0, 0))],
        out_specs=pl.BlockSpec((SEQ, GLA_DV), lambda b, h: (b, h)),
        scratch_shapes=[pltpu.VMEM((SEQ, GLA_DV), f32), pltpu.VMEM((SEQ, GLA_DV), f32),
                        pltpu.VMEM((GLA_DK, GLA_DV), f32), pltpu.VMEM((GLA_DK, GLA_DV), f32)],
        compiler_params=_params("parallel", "parallel"),
        name="gla_mixer",
    )(proj, proj, proj, proj, proj, logf_f, logf_b, logf_f, gate, g_norm)


def _rope(x, tab):
    return (x * tab[0]
            + pltpu.roll(x, LANE - ROPE_HALF, axis=1) * tab[1]
            + pltpu.roll(x, ROPE_HALF, axis=1) * tab[2])


def _diff_kernel(lam_init, q_ref, k_ref, ke_ref, v_ref, ve_ref, tq_ref, tk_ref, lq1_ref, lk1_ref, lq2_ref, lk2_ref,
                 g_ref, o_ref, kr_ref, vr_ref):
    @pl.when(pl.program_id(2) == 0)
    def _():
        meta = slice(META_PAD, EXTRA)
        zeros = slice(0, META_PAD)
        for m in range(2):
            cols = slice(m * DIFF_DH, (m + 1) * DIFF_DH)
            kr_ref[m, 0:SEQ, :] = _rope(k_ref[:, cols].astype(f32), tk_ref[:, 0:SEQ, :]).astype(bf16)
            kr_ref[m, SEQ:L_REAL, :] = _rope(ke_ref[meta, cols].astype(f32), tk_ref[:, SEQ:L_REAL, :]).astype(bf16)
            kr_ref[m, L_REAL:LK, :] = ke_ref[zeros, cols]
        vr_ref[0:SEQ, :] = v_ref[...]
        vr_ref[SEQ:L_REAL, :] = ve_ref[meta, :]
        vr_ref[L_REAL:LK, :] = ve_ref[zeros, :]

    lam = (jnp.exp(jnp.sum(lq1_ref[...] * lk1_ref[...], axis=1, keepdims=True))
           - jnp.exp(jnp.sum(lq2_ref[...] * lk2_ref[...], axis=1, keepdims=True)) + lam_init)
    tq = tq_ref[...]
    key_ok = lax.broadcasted_iota(jnp.int32, (DIFF_TQ, LK), 1) < L_REAL
    exp2_scale = (DIFF_DH ** -0.5) * math.log2(math.e)
    outs = []
    for m in range(2):
        qx = q_ref[:, m * DIFF_DH:(m + 1) * DIFF_DH].astype(f32)
        qr = _rope(qx, tq).astype(bf16)
        s = jnp.where(key_ok, _dot_nt(qr, kr_ref[m]), NEG_BIG)
        e = jnp.exp2((s - jnp.max(s, axis=1, keepdims=True)) * exp2_scale)
        inv = 1.0 / jnp.sum(e, axis=1, keepdims=True)
        outs.append(_dot(e.astype(bf16), vr_ref[...]) * inv)
    o = outs[0] - lam * outs[1]
    o_ref[...] = (_rms(o, g_ref[...]) * (1.0 - lam_init)).astype(o_ref.dtype)


def _diff_attention(proj, rope_tab, lq1, lk1, lq2, lk2, g_norm, lam_init):
    nq = SEQ // DIFF_TQ
    ck = DIFF_QK // DIFF_DV
    cv = 2 * ck
    ex = TX // EXTRA
    vec = pl.BlockSpec((1, DIFF_DH), lambda b, h, i: (0, 0))
    return pl.pallas_call(
        functools.partial(_diff_kernel, lam_init),
        out_shape=jax.ShapeDtypeStruct((TX, DIFF_V), bf16),
        grid=(BATCH, DIFF_HEADS, nq),
        in_specs=[pl.BlockSpec((DIFF_TQ, DIFF_DV), lambda b, h, i: (b * nq + i, h)),
                  pl.BlockSpec((SEQ, DIFF_DV), lambda b, h, i: (b, ck + h)),
                  pl.BlockSpec((EXTRA, DIFF_DV), lambda b, h, i: (ex + b, ck + h)),
                  pl.BlockSpec((SEQ, DIFF_DV), lambda b, h, i: (b, cv + h)),
                  pl.BlockSpec((EXTRA, DIFF_DV), lambda b, h, i: (ex + b, cv + h)),
                  pl.BlockSpec((3, DIFF_TQ, DIFF_DH), lambda b, h, i: (0, i, 0)),
                  pl.BlockSpec((3, LK, DIFF_DH), lambda b, h, i: (0, 0, 0)),
                  vec, vec, vec, vec,
                  pl.BlockSpec((1, DIFF_DV), lambda b, h, i: (0, 0))],
        out_specs=pl.BlockSpec((DIFF_TQ, DIFF_DV), lambda b, h, i: (b * nq + i, h)),
        scratch_shapes=[pltpu.VMEM((2, LK, DIFF_DH), bf16), pltpu.VMEM((LK, DIFF_DV), bf16)],
        compiler_params=_params("parallel", "parallel", "arbitrary"),
        name="diff_attention",
    )(proj, proj, proj, proj, proj, rope_tab, rope_tab, lq1, lk1, lq2, lk2, g_norm)


def _merge_kernel(og_ref, od_ref, za_ref, zb_ref, hs_ref, wa_ref, wb_ref, wo_ref, g_ref, hs2_ref, hn_ref, hnt_ref):
    y = (jax.nn.sigmoid(za_ref[...]) * _dot(og_ref[...], wa_ref[...])
         + jax.nn.sigmoid(zb_ref[...]) * _dot(od_ref[...], wb_ref[...]))
    hs2 = hs_ref[...] + _dot(y.astype(bf16), wo_ref[...])
    hs2_ref[...] = hs2
    hn = _rms(hs2, g_ref[...])
    hn_ref[...] = hn.astype(hn_ref.dtype)
    hnt_ref[...] = hn.T.astype(hnt_ref.dtype)


def _merge(o_gla, o_diff, zab, hs, wa, wb, wo, g_ffn):
    tm = MERGE_TM
    row = lambda i: (i, 0)
    fixed = lambda i: (0, 0)
    wspec = pl.BlockSpec((D_MODEL, D_MODEL), fixed, pipeline_mode=pl.Buffered(1))
    return pl.pallas_call(
        _merge_kernel,
        out_shape=(jax.ShapeDtypeStruct((TX, D_MODEL), f32), jax.ShapeDtypeStruct((TX, D_MODEL), bf16),
                   jax.ShapeDtypeStruct((D_MODEL, TX), bf16)),
        grid=(TX // tm,),
        in_specs=[pl.BlockSpec((tm, GLA_V), row),
                  pl.BlockSpec((tm, DIFF_V), row),
                  pl.BlockSpec((tm, D_MODEL), lambda i: (i, 0)),
                  pl.BlockSpec((tm, D_MODEL), lambda i: (i, 1)),
                  pl.BlockSpec((tm, D_MODEL), row),
                  wspec, wspec, wspec,
                  pl.BlockSpec((1, D_MODEL), fixed)],
        out_specs=(pl.BlockSpec((tm, D_MODEL), row), pl.BlockSpec((tm, D_MODEL), row),
                   pl.BlockSpec((D_MODEL, tm), lambda i: (0, i))),
        compiler_params=_params("parallel"),
        name="branch_merge",
    )(o_gla, o_diff, zab, zab, hs, wa, wb, wo, g_ffn)


def _top16(s, iota, break_ties):
    vals = []
    if not break_ties:
        for r in range(PEER_TOPK):
            m = jnp.max(s, axis=0, keepdims=True)
            s = jnp.where(s == m, -RANK_MARK * (r + 1), s)
            vals.append(m)
        rank = jnp.where(s <= -RANK_MARK, s * (-1.0 / RANK_MARK) - 1.0, float(PEER_TOPK))
        return vals, rank
    rank = jnp.full(s.shape, float(PEER_TOPK), f32)
    for r in range(PEER_TOPK):
        m = jnp.max(s, axis=0, keepdims=True)
        first = jnp.min(jnp.where(s == m, iota, float(PEER_NKEYS)), axis=0, keepdims=True)
        hit = iota == first
        rank = jnp.where(hit, float(r), rank)
        s = jnp.where(hit, -jnp.inf, s)
        vals.append(m)
    return vals, rank


def _select_head(s0, s1, iota, ids, break_ties):
    K = PEER_TOPK
    tb = s0.shape[1]
    v0, rank0 = _top16(s0, iota, break_ties)
    v1, rank1 = _top16(s1, iota, break_ties)
    sa = jnp.concatenate(v0, axis=0)
    sb = jnp.concatenate(v1, axis=0)
    cand = jnp.concatenate([sa[0:1] + sb]
                           + [sa[a:a + 1] + sb[0:8] for a in range(1, 8)]
                           + [sa[8:16] + sb[0:1]], axis=0)
    top = cand[0:1]
    z = jnp.zeros((1, tb), f32)
    for _ in range(K):
        m = jnp.max(cand, axis=0, keepdims=True)
        hit = cand == m
        if break_ties:
            first = jnp.min(jnp.where(hit, ids, 1e9), axis=0, keepdims=True)
            hit = ids == first
        cand = jnp.where(hit, -jnp.inf, cand)
        z = z + jnp.exp(m - top)
    taken = jnp.where(cand == -jnp.inf, 1.0, 0.0)
    cnt = ([jnp.sum(taken[0:16], axis=0, keepdims=True)]
           + [jnp.sum(taken[8 + 8 * a:16 + 8 * a], axis=0, keepdims=True) for a in range(1, 8)]
           + [taken[72 + a:73 + a] for a in range(8)])
    cnt0 = jnp.zeros((PEER_NKEYS, tb), f32)
    for a in range(K):
        cnt0 = jnp.where(rank0 == float(a), cnt[a], cnt0)
    e0 = jnp.exp(s0 - v0[0]) * (1.0 / z)
    e1 = jnp.exp(s1 - v1[0])
    marked = (jnp.sum((rank0 < float(K)).astype(f32), axis=0, keepdims=True),
              jnp.sum((rank1 < float(K)).astype(f32), axis=0, keepdims=True),
              jnp.sum(taken, axis=0, keepdims=True))
    excess = jnp.max(sum(jnp.abs(n - float(K)) for n in marked))
    return (rank1, cnt0, e0, e1), excess


def _peer_select_kernel(hn_ref, wq_ref, keys_ref, rank1_ref, cnt0_ref, e0_ref, e1_ref, q_sc):
    tb = SEL_TB
    K = PEER_TOPK
    q_sc[...] = _dot(hn_ref[...], wq_ref[...]).astype(bf16)
    iota = lax.broadcasted_iota(jnp.int32, (PEER_NKEYS, tb), 0).astype(f32)
    i16 = lax.broadcasted_iota(jnp.int32, (K, tb), 0).astype(f32)
    i8 = lax.broadcasted_iota(jnp.int32, (8, tb), 0).astype(f32)
    ids = jnp.concatenate([i16] + [a * float(K) + i8 for a in range(1, 8)] + [(i8 + 8.0) * float(K)], axis=0)

    def head(h, carry):
        c0 = pl.multiple_of(h * (2 * PEER_DKEY), 2 * PEER_DKEY)
        s0 = _dot_nt(keys_ref[h, 0], q_sc[:, pl.ds(c0, PEER_DKEY)])
        s1 = _dot_nt(keys_ref[h, 1], q_sc[:, pl.ds(c0 + PEER_DKEY, PEER_DKEY)])

        def store(rank1, cnt0, e0, e1):
            rank1_ref[h] = rank1.astype(bf16)
            cnt0_ref[h] = cnt0
            e0_ref[h] = e0
            e1_ref[h] = e1.astype(bf16)

        tables, excess = _select_head(s0, s1, iota, ids, False)
        store(*tables)

        @pl.when(excess > 0.0)
        def _():
            store(*_select_head(s0, s1, iota, ids, True)[0])

        return carry

    lax.fori_loop(0, PEER_HEADS, head, 0)


def _peer_select(hn, wq, keys):
    tb = SEL_TB
    sel = lambda dt: jax.ShapeDtypeStruct((PEER_HEADS, PEER_NKEYS, TX), dt)
    sel_spec = pl.BlockSpec((PEER_HEADS, PEER_NKEYS, tb), lambda i: (0, 0, i))
    return pl.pallas_call(
        _peer_select_kernel,
        out_shape=(sel(bf16), sel(f32), sel(f32), sel(bf16)),
        grid=(TX // tb,),
        in_specs=[pl.BlockSpec((tb, D_MODEL), lambda i: (i, 0)),
                  pl.BlockSpec((D_MODEL, PEER_HEADS * 2 * PEER_DKEY), lambda i: (0, 0)),
                  pl.BlockSpec((PEER_HEADS, 2, PEER_NKEYS, PEER_DKEY), lambda i: (0, 0, 0, 0))],
        out_specs=(sel_spec,) * 4,
        scratch_shapes=[pltpu.VMEM((tb, PEER_HEADS * 2 * PEER_DKEY), bf16)],
        compiler_params=_params("parallel"),
        name="peer_select",
    )(hn, wq, keys)


def _peer_kernel(hnt_ref, u_ref, v_ref, rank1_ref, cnt0_ref, e0_ref, e1_ref, hs_ref, g_ref, o_ref,
                 acc_ref, a_ref, p_ref):
    j = pl.program_id(1)
    n_blocks = pl.num_programs(1) - 1
    groups = PEER_EB // PEER_NKEYS

    @pl.when(j == 0)
    def _():
        acc_ref[...] = jnp.zeros_like(acc_ref)

    @pl.when(j > 0)
    def _():
        for gi in range(groups):
            i = (j - 1) * groups + gi
            cnts = [jnp.broadcast_to(cnt0_ref[h, pl.ds(i, 1), :], (BF16_ROWS, PEER_TB)).astype(bf16)
                    for h in range(PEER_HEADS)]
            e0s = [jnp.broadcast_to(e0_ref[h, pl.ds(i, 1), :], (BF16_ROWS, PEER_TB)).astype(bf16)
                   for h in range(PEER_HEADS)]
            for r in range(PEER_NKEYS // BF16_ROWS):
                keys = slice(r * BF16_ROWS, (r + 1) * BF16_ROWS)
                rows = slice(gi * PEER_NKEYS + r * BF16_ROWS, gi * PEER_NKEYS + (r + 1) * BF16_ROWS)
                a = a_ref[rows, :]
                act = 0.5 * a * (1.0 + lax.erf(a * (2.0 ** -0.5)))
                w = jnp.zeros((BF16_ROWS, PEER_TB), bf16)
                for h in range(PEER_HEADS):
                    w = w + jnp.where(rank1_ref[h, keys, :] < cnts[h], e1_ref[h, keys, :], jnp.zeros((), bf16)) * e0s[h]
                p_ref[rows, :] = w * act.astype(bf16)
        acc_ref[...] += _dot_tn(v_ref[...], p_ref[...])

    @pl.when(j < n_blocks)
    def _():
        a_ref[...] = _dot(u_ref[...], hnt_ref[...])

    @pl.when(j == n_blocks)
    def _():
        hs3 = hs_ref[...] + acc_ref[...].T
        o_ref[...] = _rms(hs3, g_ref[...])


def _peer(hnt, u, v, rank1, cnt0, e0, e1, hs2, g_final):
    tb, eb = PEER_TB, PEER_EB
    n_blocks = PEER_N // eb
    sel_spec = pl.BlockSpec((PEER_HEADS, PEER_NKEYS, tb), lambda i, j: (0, 0, i))
    return pl.pallas_call(
        _peer_kernel,
        out_shape=jax.ShapeDtypeStruct((TX, D_MODEL), f32),
        grid=(TX // tb, n_blocks + 1),
        in_specs=[pl.BlockSpec((D_MODEL, tb), lambda i, j: (0, i)),
                  pl.BlockSpec((eb, D_MODEL), lambda i, j: (jnp.minimum(j, n_blocks - 1), 0)),
                  pl.BlockSpec((eb, D_MODEL), lambda i, j: (jnp.maximum(j - 1, 0), 0)),
                  sel_spec, sel_spec, sel_spec, sel_spec,
                  pl.BlockSpec((tb, D_MODEL), lambda i, j: (i, 0), pipeline_mode=pl.Buffered(1)),
                  pl.BlockSpec((1, D_MODEL), lambda i, j: (0, 0))],
        out_specs=pl.BlockSpec((tb, D_MODEL), lambda i, j: (i, 0)),
        scratch_shapes=[pltpu.VMEM((D_MODEL, tb), f32), pltpu.VMEM((eb, tb), f32), pltpu.VMEM((eb, tb), bf16)],
        compiler_params=_params("parallel", "arbitrary"),
        name="peer_experts",
    )(hnt, u, v, rank1, cnt0, e0, e1, hs2, g_final)


def _rope_tables():
    pos = jnp.concatenate([jnp.arange(N_META, N_META + SEQ, dtype=jnp.int32),
                           jnp.arange(N_META, dtype=jnp.int32),
                           jnp.zeros((META_PAD,), jnp.int32)])
    inv = 1.0 / (ROPE_THETA ** (jnp.arange(ROPE_HALF, dtype=f32) / ROPE_HALF))
    ang = pos.astype(f32)[:, None] * inv[None, :]
    cos, sin = jnp.cos(ang), jnp.sin(ang)
    rest = DIFF_DH - ROPE_DIMS
    zero, zrest = jnp.zeros_like(sin), jnp.zeros((LK, rest), f32)
    return jnp.stack([
        jnp.concatenate([cos, cos, jnp.ones((LK, rest), f32)], axis=1),
        jnp.concatenate([-sin, zero, zrest], axis=1),
        jnp.concatenate([zero, sin, zrest], axis=1)])


def kernel(x, meta_tokens, g_mix, w_in, gla_w2_fwd, gla_b_fwd, gla_w2_bwd, gla_b_bwd, gla_g_norm, diff_lq1, diff_lk1, diff_lq2, diff_lk2, diff_g_norm, w_branch_gla, w_branch_diff, w_out, g_ffn, peer_w_q, peer_sub_keys, peer_u, peer_v, g_final):
    assert w_in.shape[0] == 1, "single-layer block only"
    l = 0
    lam_init = 0.8 - 0.6 * math.exp(-0.3 * l)
    xs = x.reshape(TX, D_MODEL)
    extra = jnp.concatenate([jnp.zeros((META_PAD, D_MODEL), x.dtype), meta_tokens.astype(x.dtype)], axis=0)
    extras = jnp.tile(extra, (BATCH, 1))
    rope_tab = _rope_tables()

    wt = jnp.swapaxes(w_in[l], 0, 1).astype(bf16)
    o_gq, o_gk, o_gv, o_gr, o_lr, o_dq, o_dk, o_dv, o_za, o_zb = IN_OFFSETS
    w_lr = jnp.pad(wt[o_lr:o_lr + 2 * GLA_LOWRANK], ((0, LANE - 2 * GLA_LOWRANK), (0, 0)))
    w2f = jnp.pad(gla_w2_fwd[l], ((0, LANE - GLA_LOWRANK), (0, 0))).astype(bf16)
    w2b = jnp.pad(gla_w2_bwd[l], ((GLA_LOWRANK, LANE - 2 * GLA_LOWRANK), (0, 0))).astype(bf16)

    h = _norm_rows(xs, extras, g_mix[l][None])
    proj_gla = _project(h, wt, o_gq, 2 * GLA_QK + GLA_V, TP, bf16, "in_proj_gla")
    proj_diff = _project(h, wt, o_dq, 2 * DIFF_QK + DIFF_V, TP, bf16, "in_proj_diff")
    gate_gla = _project(h, wt, o_gr, GLA_V, TX, f32, "in_proj_gla_gate")
    gate_merge = _project(h, wt, o_za, 2 * D_MODEL, TX, f32, "in_proj_merge_gates")
    logf_f, logf_b = _decay(h, w_lr, w2f, w2b, gla_b_fwd[l][None], gla_b_bwd[l][None])
    o_gla = _gla(proj_gla, gate_gla, logf_f, logf_b, gla_g_norm[l][None])
    o_diff = _diff_attention(proj_diff, rope_tab, diff_lq1[l][None], diff_lk1[l][None], diff_lq2[l][None],
                             diff_lk2[l][None], diff_g_norm[l][None], lam_init)
    hs2, hn, hnt = _merge(o_gla, o_diff, gate_merge, xs, w_branch_gla[l].astype(bf16),
                          w_branch_diff[l].astype(bf16), w_out[l].astype(bf16), g_ffn[l][None])
    rank1, cnt0, e0, e1 = _peer_select(hn, peer_w_q[l].astype(bf16), peer_sub_keys[l].astype(bf16))
    out = _peer(hnt, peer_u[l].astype(bf16), peer_v[l].astype(bf16), rank1, cnt0, e0, e1, hs2, g_final[None])
    return out.reshape(BATCH, SEQ, D_MODEL)
```

```python
import functools
import math

import jax
import jax.numpy as jnp
from jax import lax
from jax.experimental import pallas as pl
from jax.experimental.pallas import tpu as pltpu

f32 = jnp.float32
bf16 = jnp.bfloat16

D_MODEL = 2048
BATCH = 4
SEQ = 2048
N_META = 16
EPS = 1e-6
LANE = 128
BF16_ROWS = 16
EXTRA = LANE
META_PAD = EXTRA - N_META
TX = BATCH * SEQ
TP = TX + BATCH * EXTRA
L_REAL = SEQ + N_META
LK = SEQ + EXTRA

GLA_HEADS = 4
GLA_DK = 256
GLA_DV = 512
GLA_QK = GLA_HEADS * GLA_DK
GLA_V = GLA_HEADS * GLA_DV
GLA_LOWRANK = 16
GLA_TAU = 16.0
GLA_CHUNK = 64
GLA_UNROLL = 32

DIFF_HEADS = 8
DIFF_DH = 128
DIFF_DV = 256
DIFF_QK = DIFF_HEADS * 2 * DIFF_DH
DIFF_V = DIFF_HEADS * DIFF_DV
ROPE_THETA = 500000.0
ROPE_DIMS = DIFF_DH // 4
ROPE_HALF = ROPE_DIMS // 2

PEER_HEADS = 8
PEER_NKEYS = 128
PEER_N = PEER_NKEYS * PEER_NKEYS
PEER_DKEY = 128
PEER_TOPK = 16

IN_SIZES = (GLA_QK, GLA_QK, GLA_V, GLA_V, 2 * GLA_LOWRANK, DIFF_QK, DIFF_QK, DIFF_V, D_MODEL, D_MODEL)
IN_OFFSETS = tuple(sum(IN_SIZES[:i]) for i in range(len(IN_SIZES)))

VMEM_LIMIT = 56 * 1024 * 1024
NEG_BIG = -1e30
RANK_MARK = 2.0 ** 100

ROW_BLOCK = 512
COL_BLOCK = 2048
DIFF_TQ = 2048
MERGE_TM = 256
SEL_TB = 512
PEER_TB = 512
PEER_EB = 1024


def _params(*sem):
    return pltpu.CompilerParams(dimension_semantics=sem, vmem_limit_bytes=VMEM_LIMIT)


def _rms(x, g):
    return x * lax.rsqrt(jnp.mean(x * x, axis=-1, keepdims=True) + EPS) * g


def _dot(a, b):
    return jnp.dot(a, b, preferred_element_type=f32)


def _dot_nt(a, b):
    return lax.dot_general(a, b, (((1,), (1,)), ((), ())), preferred_element_type=f32)


def _dot_tn(a, b):
    return lax.dot_general(a, b, (((0,), (0,)), ((), ())), preferred_element_type=f32)


def _norm_kernel(x_ref, e_ref, g_ref, o_ref):
    is_seq = pl.program_id(0) < TX // ROW_BLOCK

    @pl.when(is_seq)
    def _():
        o_ref[...] = _rms(x_ref[...], g_ref[...]).astype(o_ref.dtype)

    @pl.when(jnp.logical_not(is_seq))
    def _():
        o_ref[...] = _rms(e_ref[...], g_ref[...]).astype(o_ref.dtype)


def _norm_rows(x, extras, g):
    assert BATCH * EXTRA == ROW_BLOCK
    last = TX // ROW_BLOCK - 1
    return pl.pallas_call(
        _norm_kernel,
        out_shape=jax.ShapeDtypeStruct((TP, D_MODEL), bf16),
        grid=(TP // ROW_BLOCK,),
        in_specs=[pl.BlockSpec((ROW_BLOCK, D_MODEL), lambda i: (jnp.minimum(i, last), 0)),
                  pl.BlockSpec((ROW_BLOCK, D_MODEL), lambda i: (0, 0)),
                  pl.BlockSpec((1, D_MODEL), lambda i: (0, 0))],
        out_specs=pl.BlockSpec((ROW_BLOCK, D_MODEL), lambda i: (i, 0)),
        compiler_params=_params("arbitrary"),
        name="mix_norm",
    )(x, extras, g)


def _mm_kernel(a_ref, wt_ref, o_ref):
    o_ref[...] = _dot_nt(a_ref[...], wt_ref[...]).astype(o_ref.dtype)


def _project(h, wt, r0, n, rows, out_dtype, name):
    assert r0 % BF16_ROWS == 0 and n % COL_BLOCK == 0 and rows % ROW_BLOCK == 0
    return pl.pallas_call(
        _mm_kernel,
        out_shape=jax.ShapeDtypeStruct((rows, n), out_dtype),
        grid=(n // COL_BLOCK, rows // ROW_BLOCK),
        in_specs=[pl.BlockSpec((ROW_BLOCK, D_MODEL), lambda j, i: (i, 0)),
                  pl.BlockSpec((pl.Element(COL_BLOCK), pl.Element(D_MODEL)),
                               lambda j, i: (pl.multiple_of(r0 + j * COL_BLOCK, BF16_ROWS), 0))],
        out_specs=pl.BlockSpec((ROW_BLOCK, COL_BLOCK), lambda j, i: (i, j)),
        compiler_params=_params("parallel", "parallel"),
        name=name,
    )(h, wt)


def _decay_kernel(h_ref, wlr_ref, w2f_ref, w2b_ref, bf_ref, bb_ref, of_ref, ob_ref):
    lr = _dot_nt(h_ref[...], wlr_ref[...]).astype(bf16)
    zf = _dot(lr, w2f_ref[...]) + bf_ref[...]
    zb = _dot(lr, w2b_ref[...]) + bb_ref[...]
    of_ref[...] = jax.nn.log_sigmoid(zf) * (1.0 / GLA_TAU)
    ob_ref[...] = jax.nn.log_sigmoid(zb) * (1.0 / GLA_TAU)


def _decay(h, wlr, w2f, w2b, b_f, b_b):
    row = lambda i: (i, 0)
    fixed = lambda i: (0, 0)
    return pl.pallas_call(
        _decay_kernel,
        out_shape=(jax.ShapeDtypeStruct((TP, GLA_QK), f32),) * 2,
        grid=(TP // ROW_BLOCK,),
        in_specs=[pl.BlockSpec((ROW_BLOCK, D_MODEL), row),
                  pl.BlockSpec((LANE, D_MODEL), fixed),
                  pl.BlockSpec((LANE, GLA_QK), fixed),
                  pl.BlockSpec((LANE, GLA_QK), fixed),
                  pl.BlockSpec((1, GLA_QK), fixed),
                  pl.BlockSpec((1, GLA_QK), fixed)],
        out_specs=(pl.BlockSpec((ROW_BLOCK, GLA_QK), row),) * 2,
        compiler_params=_params("parallel"),
        name="gla_decay",
    )(h, wlr, w2f, w2b, b_f, b_b)


def _gla_kernel(q_ref, k_ref, v_ref, ke_ref, ve_ref, ff_ref, fb_ref, fe_ref, r_ref, g_ref, o_ref,
                accf_ref, accb_ref, sf_ref, sb_ref):
    C = GLA_CHUNK
    n_chunks = SEQ // C
    row = lax.broadcasted_iota(jnp.int32, (C, C), 0)
    col = lax.broadcasted_iota(jnp.int32, (C, C), 1)
    time = lax.broadcasted_iota(jnp.int32, (C, GLA_DK), 0)

    def decay_sums(lf, prefix):
        cum = lf
        shift = 1
        while shift < C:
            if prefix:
                cum = cum + jnp.where(time >= shift, pltpu.roll(cum, shift, axis=0), 0.0)
            else:
                cum = cum + jnp.where(time < C - shift, pltpu.roll(cum, C - shift, axis=0), 0.0)
            shift *= 2
        total_row = C - 1 if prefix else 0
        return cum, cum[total_row:total_row + 1, :]

    def advance(s_ref, k, v, cum, tot):
        tot_col = jnp.broadcast_to(tot, (8, GLA_DK)).T[:, 0:1]
        k_st = (k * jnp.exp(tot - cum)).astype(bf16)
        s_ref[...] = s_ref[...] * jnp.exp(tot_col) + _dot_tn(k_st, v)

    def chunk(n, f_ref, prefix, keep_mask, acc_ref, s_ref):
        rows = pl.ds(pl.multiple_of(n * C, C), C)
        cum, tot = decay_sums(f_ref[rows, :], prefix)
        q = q_ref[rows, :].astype(f32) * (GLA_DK ** -0.5)
        k = k_ref[rows, :].astype(f32)
        v = v_ref[rows, :]
        q_in = (q * jnp.exp(cum)).astype(bf16)
        k_in = (k * jnp.exp(-cum)).astype(bf16)
        a = jnp.where(keep_mask, _dot_nt(q_in, k_in), 0.0)
        acc_ref[rows, :] = _dot(a.astype(bf16), v) + _dot(q_in, s_ref[...].astype(bf16))
        advance(s_ref, k, v, cum, tot)

    sf_ref[...] = jnp.zeros_like(sf_ref)
    sb_ref[...] = jnp.zeros_like(sb_ref)

    meta_rows = slice(EXTRA - C, EXTRA)
    cum_e, tot_e = decay_sums(fe_ref[meta_rows, :], True)
    advance(sf_ref, ke_ref[meta_rows, :].astype(f32), ve_ref[meta_rows, :], cum_e, tot_e)

    def both(m, carry):
        for u in range(GLA_UNROLL):
            n = m * GLA_UNROLL + u
            chunk(n, ff_ref, True, col <= row, accf_ref, sf_ref)
            chunk(n_chunks - 1 - n, fb_ref, False, col > row, accb_ref, sb_ref)
        return carry

    lax.fori_loop(0, n_chunks // GLA_UNROLL, both, 0)

    def fin(n, carry):
        rows = pl.ds(pl.multiple_of(n * LANE, LANE), LANE)
        o = _rms(accf_ref[rows, :] + accb_ref[rows, :], g_ref[...])
        o_ref[rows, :] = (o * jax.nn.silu(r_ref[rows, :])).astype(o_ref.dtype)
        return carry

    lax.fori_loop(0, SEQ // LANE, fin, 0)


def _gla(proj, gate, logf_f, logf_b, g_norm):
    kq = GLA_QK // GLA_DK
    kv = (2 * GLA_QK) // GLA_DV
    ex = TX // EXTRA
    return pl.pallas_call(
        _gla_kernel,
        out_shape=jax.ShapeDtypeStruct((TX, GLA_V), bf16),
        grid=(BATCH, GLA_HEADS),
        in_specs=[pl.BlockSpec((SEQ, GLA_DK), lambda b, h: (b, h)),
                  pl.BlockSpec((SEQ, GLA_DK), lambda b, h: (b, kq + h)),
                  pl.BlockSpec((SEQ, GLA_DV), lambda b, h: (b, kv + h)),
                  pl.BlockSpec((EXTRA, GLA_DK), lambda b, h: (ex + b, kq + h)),
                  pl.BlockSpec((EXTRA, GLA_DV), lambda b, h: (ex + b, kv + h)),
                  pl.BlockSpec((SEQ, GLA_DK), lambda b, h: (b, h)),
                  pl.BlockSpec((SEQ, GLA_DK), lambda b, h: (b, h)),
                  pl.BlockSpec((EXTRA, GLA_DK), lambda b, h: (ex + b, h)),
                  pl.BlockSpec((SEQ, GLA_DV), lambda b, h: (b, h)),
                  pl.BlockSpec((1, GLA_DV), lambda b, h: (0, 0))],
        out_specs=pl.BlockSpec((SEQ, GLA_DV), lambda b, h: (b, h)),
        scratch_shapes=[pltpu.VMEM((SEQ, GLA_DV), f32), pltpu.VMEM((SEQ, GLA_DV), f32),
                        pltpu.VMEM((GLA_DK, GLA_DV), f32), pltpu.VMEM((GLA_DK, GLA_DV), f32)],
        compiler_params=_params("parallel", "parallel"),
        name="gla_mixer",
    )(proj, proj, proj, proj, proj, logf_f, logf_b, logf_f, gate, g_norm)


def _rope(x, tab):
    return (x * tab[0]
            + pltpu.roll(x, LANE - ROPE_HALF, axis=1) * tab[1]
            + pltpu.roll(x, ROPE_HALF, axis=1) * tab[2])


def _diff_kernel(lam_init, q_ref, k_ref, ke_ref, v_ref, ve_ref, tq_ref, tk_ref, lq1_ref, lk1_ref, lq2_ref, lk2_ref,
                 g_ref, o_ref, kr_ref, vr_ref):
    @pl.when(pl.program_id(2) == 0)
    def _():
        meta = slice(META_PAD, EXTRA)
        zeros = slice(0, META_PAD)
        for m in range(2):
            cols = slice(m * DIFF_DH, (m + 1) * DIFF_DH)
            kr_ref[m, 0:SEQ, :] = _rope(k_ref[:, cols].astype(f32), tk_ref[:, 0:SEQ, :]).astype(bf16)
            kr_ref[m, SEQ:L_REAL, :] = _rope(ke_ref[meta, cols].astype(f32), tk_ref[:, SEQ:L_REAL, :]).astype(bf16)
            kr_ref[m, L_REAL:LK, :] = ke_ref[zeros, cols]
        vr_ref[0:SEQ, :] = v_ref[...]
        vr_ref[SEQ:L_REAL, :] = ve_ref[meta, :]
        vr_ref[L_REAL:LK, :] = ve_ref[zeros, :]

    lam = (jnp.exp(jnp.sum(lq1_ref[...] * lk1_ref[...], axis=1, keepdims=True))
           - jnp.exp(jnp.sum(lq2_ref[...] * lk2_ref[...], axis=1, keepdims=True)) + lam_init)
    tq = tq_ref[...]
    key_ok = lax.broadcasted_iota(jnp.int32, (DIFF_TQ, LK), 1) < L_REAL
    exp2_scale = (DIFF_DH ** -0.5) * math.log2(math.e)
    outs = []
    for m in range(2):
        qx = q_ref[:, m * DIFF_DH:(m + 1) * DIFF_DH].astype(f32)
        qr = _rope(qx, tq).astype(bf16)
        s = jnp.where(key_ok, _dot_nt(qr, kr_ref[m]), NEG_BIG)
        e = jnp.exp2((s - jnp.max(s, axis=1, keepdims=True)) * exp2_scale)
        inv = 1.0 / jnp.sum(e, axis=1, keepdims=True)
        outs.append(_dot(e.astype(bf16), vr_ref[...]) * inv)
    o = outs[0] - lam * outs[1]
    o_ref[...] = (_rms(o, g_ref[...]) * (1.0 - lam_init)).astype(o_ref.dtype)


def _diff_attention(proj, rope_tab, lq1, lk1, lq2, lk2, g_norm, lam_init):
    nq = SEQ // DIFF_TQ
    ck = DIFF_QK // DIFF_DV
    cv = 2 * ck
    ex = TX // EXTRA
    vec = pl.BlockSpec((1, DIFF_DH), lambda b, h, i: (0, 0))
    return pl.pallas_call(
        functools.partial(_diff_kernel, lam_init),
        out_shape=jax.ShapeDtypeStruct((TX, DIFF_V), bf16),
        grid=(BATCH, DIFF_HEADS, nq),
        in_specs=[pl.BlockSpec((DIFF_TQ, DIFF_DV), lambda b, h, i: (b * nq + i, h)),
                  pl.BlockSpec((SEQ, DIFF_DV), lambda b, h, i: (b, ck + h)),
                  pl.BlockSpec((EXTRA, DIFF_DV), lambda b, h, i: (ex + b, ck + h)),
                  pl.BlockSpec((SEQ, DIFF_DV), lambda b, h, i: (b, cv + h)),
                  pl.BlockSpec((EXTRA, DIFF_DV), lambda b, h, i: (ex + b, cv + h)),
                  pl.BlockSpec((3, DIFF_TQ, DIFF_DH), lambda b, h, i: (0, i, 0)),
                  pl.BlockSpec((3, LK, DIFF_DH), lambda b, h, i: (0, 0, 0)),
                  vec, vec, vec, vec,
                  pl.BlockSpec((1, DIFF_DV), lambda b, h, i: (0, 0))],
        out_specs=pl.BlockSpec((DIFF_TQ, DIFF_DV), lambda b, h, i: (b * nq + i, h)),
        scratch_shapes=[pltpu.VMEM((2, LK, DIFF_DH), bf16), pltpu.VMEM((LK, DIFF_DV), bf16)],
        compiler_params=_params("parallel", "parallel", "arbitrary"),
        name="diff_attention",
    )(proj, proj, proj, proj, proj, rope_tab, rope_tab, lq1, lk1, lq2, lk2, g_norm)


def _merge_kernel(og_ref, od_ref, za_ref, zb_ref, hs_ref, wa_ref, wb_ref, wo_ref, g_ref, hs2_ref, hn_ref, hnt_ref):
    y = (jax.nn.sigmoid(za_ref[...]) * _dot(og_ref[...], wa_ref[...])
         + jax.nn.sigmoid(zb_ref[...]) * _dot(od_ref[...], wb_ref[...]))
    hs2 = hs_ref[...] + _dot(y.astype(bf16), wo_ref[...])
    hs2_ref[...] = hs2
    hn = _rms(hs2, g_ref[...])
    hn_ref[...] = hn.astype(hn_ref.dtype)
    hnt_ref[...] = hn.T.astype(hnt_ref.dtype)


def _merge(o_gla, o_diff, zab, hs, wa, wb, wo, g_ffn):
    tm = MERGE_TM
    row = lambda i: (i, 0)
    fixed = lambda i: (0, 0)
    wspec = pl.BlockSpec((D_MODEL, D_MODEL), fixed, pipeline_mode=pl.Buffered(1))
    return pl.pallas_call(
        _merge_kernel,
        out_shape=(jax.ShapeDtypeStruct((TX, D_MODEL), f32), jax.ShapeDtypeStruct((TX, D_MODEL), bf16),
                   jax.ShapeDtypeStruct((D_MODEL, TX), bf16)),
        grid=(TX // tm,),
        in_specs=[pl.BlockSpec((tm, GLA_V), row),
                  pl.BlockSpec((tm, DIFF_V), row),
                  pl.BlockSpec((tm, D_MODEL), lambda i: (i, 0)),
                  pl.BlockSpec((tm, D_MODEL), lambda i: (i, 1)),
                  pl.BlockSpec((tm, D_MODEL), row),
                  wspec, wspec, wspec,
                  pl.BlockSpec((1, D_MODEL), fixed)],
        out_specs=(pl.BlockSpec((tm, D_MODEL), row), pl.BlockSpec((tm, D_MODEL), row),
                   pl.BlockSpec((D_MODEL, tm), lambda i: (0, i))),
        compiler_params=_params("parallel"),
        name="branch_merge",
    )(o_gla, o_diff, zab, zab, hs, wa, wb, wo, g_ffn)


def _top16(s, iota, break_ties):
    vals = []
    if not break_ties:
        for r in range(PEER_TOPK):
            m = jnp.max(s, axis=0, keepdims=True)
            s = jnp.where(s == m, -RANK_MARK * (r + 1), s)
            vals.append(m)
        rank = jnp.where(s <= -RANK_MARK, s * (-1.0 / RANK_MARK) - 1.0, float(PEER_TOPK))
        return vals, rank
    rank = jnp.full(s.shape, float(PEER_TOPK), f32)
    for r in range(PEER_TOPK):
        m = jnp.max(s, axis=0, keepdims=True)
        first = jnp.min(jnp.where(s == m, iota, float(PEER_NKEYS)), axis=0, keepdims=True)
        hit = iota == first
        rank = jnp.where(hit, float(r), rank)
        s = jnp.where(hit, -jnp.inf, s)
        vals.append(m)
    return vals, rank


def _select_head(s0, s1, iota, ids, break_ties):
    K = PEER_TOPK
    tb = s0.shape[1]
    v0, rank0 = _top16(s0, iota, break_ties)
    v1, rank1 = _top16(s1, iota, break_ties)
    sa = jnp.concatenate(v0, axis=0)
    sb = jnp.concatenate(v1, axis=0)
    cand = jnp.concatenate([sa[0:1] + sb]
                           + [sa[a:a + 1] + sb[0:8] for a in range(1, 8)]
                           + [sa[8:16] + sb[0:1]], axis=0)
    top = cand[0:1]
    z = jnp.zeros((1, tb), f32)
    for _ in range(K):
        m = jnp.max(cand, axis=0, keepdims=True)
        hit = cand == m
        if break_ties:
            first = jnp.min(jnp.where(hit, ids, 1e9), axis=0, keepdims=True)
            hit = ids == first
        cand = jnp.where(hit, -jnp.inf, cand)
        z = z + jnp.exp(m - top)
    taken = jnp.where(cand == -jnp.inf, 1.0, 0.0)
    cnt = ([jnp.sum(taken[0:16], axis=0, keepdims=True)]
           + [jnp.sum(taken[8 + 8 * a:16 + 8 * a], axis=0, keepdims=True) for a in range(1, 8)]
           + [taken[72 + a:73 + a] for a in range(8)])
    cnt0 = jnp.zeros((PEER_NKEYS, tb), f32)
    for a in range(K):
        cnt0 = jnp.where(rank0 == float(a), cnt[a], cnt0)
    e0 = jnp.exp(s0 - v0[0]) * (1.0 / z)
    e1 = jnp.exp(s1 - v1[0])
    marked = (jnp.sum((rank0 < float(K)).astype(f32), axis=0, keepdims=True),
              jnp.sum((rank1 < float(K)).astype(f32), axis=0, keepdims=True),
              jnp.sum(taken, axis=0, keepdims=True))
    excess = jnp.max(sum(jnp.abs(n - float(K)) for n in marked))
    return (rank1, cnt0, e0, e1), excess


def _peer_select_kernel(hn_ref, wq_ref, keys_ref, rank1_ref, cnt0_ref, e0_ref, e1_ref, q_sc):
    tb = SEL_TB
    K = PEER_TOPK
    q_sc[...] = _dot(hn_ref[...], wq_ref[...]).astype(bf16)
    iota = lax.broadcasted_iota(jnp.int32, (PEER_NKEYS, tb), 0).astype(f32)
    i16 = lax.broadcasted_iota(jnp.int32, (K, tb), 0).astype(f32)
    i8 = lax.broadcasted_iota(jnp.int32, (8, tb), 0).astype(f32)
    ids = jnp.concatenate([i16] + [a * float(K) + i8 for a in range(1, 8)] + [(i8 + 8.0) * float(K)], axis=0)

    def head(h, carry):
        c0 = pl.multiple_of(h * (2 * PEER_DKEY), 2 * PEER_DKEY)
        s0 = _dot_nt(keys_ref[h, 0], q_sc[:, pl.ds(c0, PEER_DKEY)])
        s1 = _dot_nt(keys_ref[h, 1], q_sc[:, pl.ds(c0 + PEER_DKEY, PEER_DKEY)])

        def store(rank1, cnt0, e0, e1):
            rank1_ref[h] = rank1.astype(bf16)
            cnt0_ref[h] = cnt0
            e0_ref[h] = e0
            e1_ref[h] = e1.astype(bf16)

        tables, excess = _select_head(s0, s1, iota, ids, False)
        store(*tables)

        @pl.when(excess > 0.0)
        def _():
            store(*_select_head(s0, s1, iota, ids, True)[0])

        return carry

    lax.fori_loop(0, PEER_HEADS, head, 0)


def _peer_select(hn, wq, keys):
    tb = SEL_TB
    sel = lambda dt: jax.ShapeDtypeStruct((PEER_HEADS, PEER_NKEYS, TX), dt)
    sel_spec = pl.BlockSpec((PEER_HEADS, PEER_NKEYS, tb), lambda i: (0, 0, i))
    return pl.pallas_call(
        _peer_select_kernel,
        out_shape=(sel(bf16), sel(f32), sel(f32), sel(bf16)),
        grid=(TX // tb,),
        in_specs=[pl.BlockSpec((tb, D_MODEL), lambda i: (i, 0)),
                  pl.BlockSpec((D_MODEL, PEER_HEADS * 2 * PEER_DKEY), lambda i: (0, 0)),
                  pl.BlockSpec((PEER_HEADS, 2, PEER_NKEYS, PEER_DKEY), lambda i: (0, 0, 0, 0))],
        out_specs=(sel_spec,) * 4,
        scratch_shapes=[pltpu.VMEM((tb, PEER_HEADS * 2 * PEER_DKEY), bf16)],
        compiler_params=_params("parallel"),
        name="peer_select",
    )(hn, wq, keys)


def _peer_kernel(hnt_ref, u_ref, v_ref, rank1_ref, cnt0_ref, e0_ref, e1_ref, hs_ref, g_ref, o_ref,
                 acc_ref, a_ref, p_ref):
    j = pl.program_id(1)
    n_blocks = pl.num_programs(1) - 1
    groups = PEER_EB // PEER_NKEYS

    @pl.when(j == 0)
    def _():
        acc_ref[...] = jnp.zeros_like(acc_ref)

    @pl.when(j > 0)
    def _():
        for gi in range(groups):
            i = (j - 1) * groups + gi
            cnts = [jnp.broadcast_to(cnt0_ref[h, pl.ds(i, 1), :], (BF16_ROWS, PEER_TB)).astype(bf16)
                    for h in range(PEER_HEADS)]
            e0s = [jnp.broadcast_to(e0_ref[h, pl.ds(i, 1), :], (BF16_ROWS, PEER_TB)).astype(bf16)
                   for h in range(PEER_HEADS)]
            for r in range(PEER_NKEYS // BF16_ROWS):
                keys = slice(r * BF16_ROWS, (r + 1) * BF16_ROWS)
                rows = slice(gi * PEER_NKEYS + r * BF16_ROWS, gi * PEER_NKEYS + (r + 1) * BF16_ROWS)
                a = a_ref[rows, :]
                act = 0.5 * a * (1.0 + lax.erf(a * (2.0 ** -0.5)))
                w = jnp.zeros((BF16_ROWS, PEER_TB), bf16)
                for h in range(PEER_HEADS):
                    w = w + jnp.where(rank1_ref[h, keys, :] < cnts[h], e1_ref[h, keys, :], jnp.zeros((), bf16)) * e0s[h]
                p_ref[rows, :] = w * act.astype(bf16)
        acc_ref[...] += _dot_tn(v_ref[...], p_ref[...])

    @pl.when(j < n_blocks)
    def _():
        a_ref[...] = _dot(u_ref[...], hnt_ref[...])

    @pl.when(j == n_blocks)
    def _():
        hs3 = hs_ref[...] + acc_ref[...].T
        o_ref[...] = _rms(hs3, g_ref[...])


def _peer(hnt, u, v, rank1, cnt0, e0, e1, hs2, g_final):
    tb, eb = PEER_TB, PEER_EB
    n_blocks = PEER_N // eb
    sel_spec = pl.BlockSpec((PEER_HEADS, PEER_NKEYS, tb), lambda i, j: (0, 0, i))
    return pl.pallas_call(
        _peer_kernel,
        out_shape=jax.ShapeDtypeStruct((TX, D_MODEL), f32),
        grid=(TX // tb, n_blocks + 1),
        in_specs=[pl.BlockSpec((D_MODEL, tb), lambda i, j: (0, i)),
                  pl.BlockSpec((eb, D_MODEL), lambda i, j: (jnp.minimum(j, n_blocks - 1), 0)),
                  pl.BlockSpec((eb, D_MODEL), lambda i, j: (jnp.maximum(j - 1, 0), 0)),
                  sel_spec, sel_spec, sel_spec, sel_spec,
                  pl.BlockSpec((tb, D_MODEL), lambda i, j: (i, 0), pipeline_mode=pl.Buffered(1)),
                  pl.BlockSpec((1, D_MODEL), lambda i, j: (0, 0))],
        out_specs=pl.BlockSpec((tb, D_MODEL), lambda i, j: (i, 0)),
        scratch_shapes=[pltpu.VMEM((D_MODEL, tb), f32), pltpu.VMEM((eb, tb), f32), pltpu.VMEM((eb, tb), bf16)],
        compiler_params=_params("parallel", "arbitrary"),
        name="peer_experts",
    )(hnt, u, v, rank1, cnt0, e0, e1, hs2, g_final)


def _rope_tables():
    pos = jnp.concatenate([jnp.arange(N_META, N_META + SEQ, dtype=jnp.int32),
                           jnp.arange(N_META, dtype=jnp.int32),
                           jnp.zeros((META_PAD,), jnp.int32)])
    inv = 1.0 / (ROPE_THETA ** (jnp.arange(ROPE_HALF, dtype=f32) / ROPE_HALF))
    ang = pos.astype(f32)[:, None] * inv[None, :]
    cos, sin = jnp.cos(ang), jnp.sin(ang)
    rest = DIFF_DH - ROPE_DIMS
    zero, zrest = jnp.zeros_like(sin), jnp.zeros((LK, rest), f32)
    return jnp.stack([
        jnp.concatenate([cos, cos, jnp.ones((LK, rest), f32)], axis=1),
        jnp.concatenate([-sin, zero, zrest], axis=1),
        jnp.concatenate([zero, sin, zrest], axis=1)])


def kernel(x, meta_tokens, g_mix, w_in, gla_w2_fwd, gla_b_fwd, gla_w2_bwd, gla_b_bwd, gla_g_norm, diff_lq1, diff_lk1, diff_lq2, diff_lk2, diff_g_norm, w_branch_gla, w_branch_diff, w_out, g_ffn, peer_w_q, peer_sub_keys, peer_u, peer_v, g_final):
    assert w_in.shape[0] == 1, "single-layer block only"
    l = 0
    lam_init = 0.8 - 0.6 * math.exp(-0.3 * l)
    xs = x.reshape(TX, D_MODEL)
    extra = jnp.concatenate([jnp.zeros((META_PAD, D_MODEL), x.dtype), meta_tokens.astype(x.dtype)], axis=0)
    extras = jnp.tile(extra, (BATCH, 1))
    rope_tab = _rope_tables()

    wt = jnp.swapaxes(w_in[l], 0, 1).astype(bf16)
    o_gq, o_gk, o_gv, o_gr, o_lr, o_dq, o_dk, o_dv, o_za, o_zb = IN_OFFSETS
    w_lr = jnp.pad(wt[o_lr:o_lr + 2 * GLA_LOWRANK], ((0, LANE - 2 * GLA_LOWRANK), (0, 0)))
    w2f = jnp.pad(gla_w2_fwd[l], ((0, LANE - GLA_LOWRANK), (0, 0))).astype(bf16)
    w2b = jnp.pad(gla_w2_bwd[l], ((GLA_LOWRANK, LANE - 2 * GLA_LOWRANK), (0, 0))).astype(bf16)

    h = _norm_rows(xs, extras, g_mix[l][None])
    proj_gla = _project(h, wt, o_gq, 2 * GLA_QK + GLA_V, TP, bf16, "in_proj_gla")
    proj_diff = _project(h, wt, o_dq, 2 * DIFF_QK + DIFF_V, TP, bf16, "in_proj_diff")
    gate_gla = _project(h, wt, o_gr, GLA_V, TX, f32, "in_proj_gla_gate")
    gate_merge = _project(h, wt, o_za, 2 * D_MODEL, TX, f32, "in_proj_merge_gates")
    logf_f, logf_b = _decay(h, w_lr, w2f, w2b, gla_b_fwd[l][None], gla_b_bwd[l][None])
    o_gla = _gla(proj_gla, gate_gla, logf_f, logf_b, gla_g_norm[l][None])
    o_diff = _diff_attention(proj_diff, rope_tab, diff_lq1[l][None], diff_lk1[l][None], diff_lq2[l][None],
                             diff_lk2[l][None], diff_g_norm[l][None], lam_init)
    hs2, hn, hnt = _merge(o_gla, o_diff, gate_merge, xs, w_branch_gla[l].astype(bf16),
                          w_branch_diff[l].astype(bf16), w_out[l].astype(bf16), g_ffn[l][None])
    rank1, cnt0, e0, e1 = _peer_select(hn, peer_w_q[l].astype(bf16), peer_sub_keys[l].astype(bf16))
    out = _peer(hnt, peer_u[l].astype(bf16), peer_v[l].astype(bf16), rank1, cnt0, e0, e1, hs2, g_final[None])
    return out.reshape(BATCH, SEQ, D_MODEL)
```

```python
import functools
import math

import jax
import jax.numpy as jnp
from jax import lax
from jax.experimental import pallas as pl
from jax.experimental.pallas import tpu as pltpu

f32 = jnp.float32
bf16 = jnp.bfloat16

D_MODEL = 2048
BATCH = 4
SEQ = 2048
N_META = 16
EPS = 1e-6
LANE = 128
BF16_ROWS = 16
EXTRA = LANE
META_PAD = EXTRA - N_META
TX = BATCH * SEQ
TP = TX + BATCH * EXTRA
L_REAL = SEQ + N_META
LK = SEQ + EXTRA

GLA_HEADS = 4
GLA_DK = 256
GLA_DV = 512
GLA_QK = GLA_HEADS * GLA_DK
GLA_V = GLA_HEADS * GLA_DV
GLA_LOWRANK = 16
GLA_TAU = 16.0
GLA_CHUNK = 64
GLA_UNROLL = 32

DIFF_HEADS = 8
DIFF_DH = 128
DIFF_DV = 256
DIFF_QK = DIFF_HEADS * 2 * DIFF_DH
DIFF_V = DIFF_HEADS * DIFF_DV
ROPE_THETA = 500000.0
ROPE_DIMS = DIFF_DH // 4
ROPE_HALF = ROPE_DIMS // 2

PEER_HEADS = 8
PEER_NKEYS = 128
PEER_N = PEER_NKEYS * PEER_NKEYS
PEER_DKEY = 128
PEER_TOPK = 16

IN_SIZES = (GLA_QK, GLA_QK, GLA_V, GLA_V, 2 * GLA_LOWRANK, DIFF_QK, DIFF_QK, DIFF_V, D_MODEL, D_MODEL)
IN_OFFSETS = tuple(sum(IN_SIZES[:i]) for i in range(len(IN_SIZES)))

VMEM_LIMIT = 56 * 1024 * 1024
NEG_BIG = -1e30
RANK_MARK = 2.0 ** 100

ROW_BLOCK = 512
COL_BLOCK = 2048
CAST_ROWS = 512
DIFF_TQ = 2048
MERGE_TM = 256
SEL_TB = 512
PEER_TB = 512
PEER_EB = 1024


def _params(*sem):
    return pltpu.CompilerParams(dimension_semantics=sem, vmem_limit_bytes=VMEM_LIMIT)


def _rms(x, g):
    return x * lax.rsqrt(jnp.mean(x * x, axis=-1, keepdims=True) + EPS) * g


def _dot(a, b):
    return jnp.dot(a, b, preferred_element_type=f32)


def _dot_nt(a, b):
    return lax.dot_general(a, b, (((1,), (1,)), ((), ())), preferred_element_type=f32)


def _dot_tn(a, b):
    return lax.dot_general(a, b, (((0,), (0,)), ((), ())), preferred_element_type=f32)


def _norm_kernel(x_ref, e_ref, g_ref, o_ref):
    is_seq = pl.program_id(0) < TX // ROW_BLOCK

    @pl.when(is_seq)
    def _():
        o_ref[...] = _rms(x_ref[...], g_ref[...]).astype(o_ref.dtype)

    @pl.when(jnp.logical_not(is_seq))
    def _():
        o_ref[...] = _rms(e_ref[...], g_ref[...]).astype(o_ref.dtype)


def _norm_rows(x, extras, g):
    assert BATCH * EXTRA == ROW_BLOCK
    last = TX // ROW_BLOCK - 1
    return pl.pallas_call(
        _norm_kernel,
        out_shape=jax.ShapeDtypeStruct((TP, D_MODEL), bf16),
        grid=(TP // ROW_BLOCK,),
        in_specs=[pl.BlockSpec((ROW_BLOCK, D_MODEL), lambda i: (jnp.minimum(i, last), 0)),
                  pl.BlockSpec((ROW_BLOCK, D_MODEL), lambda i: (0, 0)),
                  pl.BlockSpec((1, D_MODEL), lambda i: (0, 0))],
        out_specs=pl.BlockSpec((ROW_BLOCK, D_MODEL), lambda i: (i, 0)),
        compiler_params=_params("arbitrary"),
        name="mix_norm",
    )(x, extras, g)


def _mm_kernel(a_ref, wt_ref, o_ref):
    o_ref[...] = _dot_nt(a_ref[...], wt_ref[...]).astype(o_ref.dtype)


def _mm_cast_kernel(a_ref, wt_ref, t_ref, o_ref, tb_ref):
    o_ref[...] = _dot_nt(a_ref[...], wt_ref[...]).astype(o_ref.dtype)
    tb_ref[...] = t_ref[...].astype(tb_ref.dtype)


def _project(h, wt, r0, n, rows, out_dtype, name, table=None):
    assert r0 % BF16_ROWS == 0 and n % COL_BLOCK == 0 and rows % ROW_BLOCK == 0
    nj, ni = n // COL_BLOCK, rows // ROW_BLOCK
    in_specs = [pl.BlockSpec((ROW_BLOCK, D_MODEL), lambda j, i: (i, 0)),
                pl.BlockSpec((pl.Element(COL_BLOCK), pl.Element(D_MODEL)),
                             lambda j, i: (pl.multiple_of(r0 + j * COL_BLOCK, BF16_ROWS), 0))]
    out_spec = pl.BlockSpec((ROW_BLOCK, COL_BLOCK), lambda j, i: (i, j))
    out_shape = jax.ShapeDtypeStruct((rows, n), out_dtype)
    if table is None:
        return pl.pallas_call(
            _mm_kernel, out_shape=out_shape, grid=(nj, ni), in_specs=in_specs, out_specs=out_spec,
            compiler_params=_params("parallel", "parallel"), name=name,
        )(h, wt)
    n_slabs = table.shape[0] // CAST_ROWS
    assert nj * ni >= n_slabs
    slab = pl.BlockSpec((CAST_ROWS, D_MODEL), lambda j, i: (jnp.minimum(j * ni + i, n_slabs - 1), 0))
    return pl.pallas_call(
        _mm_cast_kernel,
        out_shape=(out_shape, jax.ShapeDtypeStruct(table.shape, bf16)),
        grid=(nj, ni), in_specs=in_specs + [slab], out_specs=(out_spec, slab),
        compiler_params=_params("arbitrary", "arbitrary"), name=name,
    )(h, wt, table)


def _decay_kernel(h_ref, wlr_ref, w2f_ref, w2b_ref, bf_ref, bb_ref, of_ref, ob_ref):
    lr = _dot_nt(h_ref[...], wlr_ref[...]).astype(bf16)
    zf = _dot(lr, w2f_ref[...]) + bf_ref[...]
    zb = _dot(lr, w2b_ref[...]) + bb_ref[...]
    of_ref[...] = jax.nn.log_sigmoid(zf) * (1.0 / GLA_TAU)
    ob_ref[...] = jax.nn.log_sigmoid(zb) * (1.0 / GLA_TAU)


def _decay(h, wlr, w2f, w2b, b_f, b_b):
    row = lambda i: (i, 0)
    fixed = lambda i: (0, 0)
    return pl.pallas_call(
        _decay_kernel,
        out_shape=(jax.ShapeDtypeStruct((TP, GLA_QK), f32),) * 2,
        grid=(TP // ROW_BLOCK,),
        in_specs=[pl.BlockSpec((ROW_BLOCK, D_MODEL), row),
                  pl.BlockSpec((LANE, D_MODEL), fixed),
                  pl.BlockSpec((LANE, GLA_QK), fixed),
                  pl.BlockSpec((LANE, GLA_QK), fixed),
                  pl.BlockSpec((1, GLA_QK), fixed),
                  pl.BlockSpec((1, GLA_QK), fixed)],
        out_specs=(pl.BlockSpec((ROW_BLOCK, GLA_QK), row),) * 2,
        compiler_params=_params("parallel"),
        name="gla_decay",
    )(h, wlr, w2f, w2b, b_f, b_b)


def _gla_kernel(q_ref, k_ref, v_ref, ke_ref, ve_ref, ff_ref, fb_ref, fe_ref, r_ref, g_ref, o_ref,
                accf_ref, accb_ref, sf_ref, sb_ref):
    C = GLA_CHUNK
    n_chunks = SEQ // C
    row = lax.broadcasted_iota(jnp.int32, (C, C), 0)
    col = lax.broadcasted_iota(jnp.int32, (C, C), 1)
    time = lax.broadcasted_iota(jnp.int32, (C, GLA_DK), 0)

    def decay_sums(lf, prefix):
        cum = lf
        shift = 1
        while shift < C:
            if prefix:
                cum = cum + jnp.where(time >= shift, pltpu.roll(cum, shift, axis=0), 0.0)
            else:
                cum = cum + jnp.where(time < C - shift, pltpu.roll(cum, C - shift, axis=0), 0.0)
            shift *= 2
        total_row = C - 1 if prefix else 0
        return cum, cum[total_row:total_row + 1, :]

    def advance(s_ref, k, v, cum, tot):
        tot_col = jnp.broadcast_to(tot, (8, GLA_DK)).T[:, 0:1]
        k_st = (k * jnp.exp(tot - cum)).astype(bf16)
        s_ref[...] = s_ref[...] * jnp.exp(tot_col) + _dot_tn(k_st, v)

    def chunk(n, f_ref, prefix, keep_mask, acc_ref, s_ref):
        rows = pl.ds(pl.multiple_of(n * C, C), C)
        cum, tot = decay_sums(f_ref[rows, :], prefix)
        q = q_ref[rows, :].astype(f32) * (GLA_DK ** -0.5)
        k = k_ref[rows, :].astype(f32)
        v = v_ref[rows, :]
        q_in = (q * jnp.exp(cum)).astype(bf16)
        k_in = (k * jnp.exp(-cum)).astype(bf16)
        a = jnp.where(keep_mask, _dot_nt(q_in, k_in), 0.0)
        acc_ref[rows, :] = _dot(a.astype(bf16), v) + _dot(q_in, s_ref[...].astype(bf16))
        advance(s_ref, k, v, cum, tot)

    sf_ref[...] = jnp.zeros_like(sf_ref)
    sb_ref[...] = jnp.zeros_like(sb_ref)

    meta_rows = slice(EXTRA - C, EXTRA)
    cum_e, tot_e = decay_sums(fe_ref[meta_rows, :], True)
    advance(sf_ref, ke_ref[meta_rows, :].astype(f32), ve_ref[meta_rows, :], cum_e, tot_e)

    def both(m, carry):
        for u in range(GLA_UNROLL):
            n = m * GLA_UNROLL + u
            chunk(n, ff_ref, True, col <= row, accf_ref, sf_ref)
            chunk(n_chunks - 1 - n, fb_ref, False, col > row, accb_ref, sb_ref)
        return carry

    lax.fori_loop(0, n_chunks // GLA_UNROLL, both, 0)

    def fin(n, carry):
        rows = pl.ds(pl.multiple_of(n * LANE, LANE), LANE)
        o = _rms(accf_ref[rows, :] + accb_ref[rows, :], g_ref[...])
        o_ref[rows, :] = (o * jax.nn.silu(r_ref[rows, :])).astype(o_ref.dtype)
        return carry

    lax.fori_loop(0, SEQ // LANE, fin, 0)


def _gla(proj, gate, logf_f, logf_b, g_norm):
    kq = GLA_QK // GLA_DK
    kv = (2 * GLA_QK) // GLA_DV
    ex = TX // EXTRA
    return pl.pallas_call(
        _gla_kernel,
        out_shape=jax.ShapeDtypeStruct((TX, GLA_V), bf16),
        grid=(BATCH, GLA_HEADS),
        in_specs=[pl.BlockSpec((SEQ, GLA_DK), lambda b, h: (b, h)),
                  pl.BlockSpec((SEQ, GLA_DK), lambda b, h: (b, kq + h)),
                  pl.BlockSpec((SEQ, GLA_DV), lambda b, h: (b, kv + h)),
                  pl.BlockSpec((EXTRA, GLA_DK), lambda b, h: (ex + b, kq + h)),
                  pl.BlockSpec((EXTRA, GLA_DV), lambda b, h: (ex + b, kv + h)),
                  pl.BlockSpec((SEQ, GLA_DK), lambda b, h: (b, h)),
                  pl.BlockSpec((SEQ, GLA_DK), lambda b, h: (b, h)),
                  pl.BlockSpec((EXTRA, GLA_DK), lambda b, h: (ex + b, h)),
                  pl.BlockSpec((SEQ, GLA_DV), lambda b, h: (b, h)),
                  pl.BlockSpec((1, GLA_DV), lambda b, h: (0, 0))],
        out_specs=pl.BlockSpec((SEQ, GLA_DV), lambda b, h: (b, h)),
        scratch_shapes=[pltpu.VMEM((SEQ, GLA_DV), f32), pltpu.VMEM((SEQ, GLA_DV), f32),
                        pltpu.VMEM((GLA_DK, GLA_DV), f32), pltpu.VMEM((GLA_DK, GLA_DV), f32)],
        compiler_params=_params("parallel", "parallel"),
        name="gla_mixer",
    )(proj, proj, proj, proj, proj, logf_f, logf_b, logf_f, gate, g_norm)


def _rope(x, tab):
    return (x * tab[0]
            + pltpu.roll(x, LANE - ROPE_HALF, axis=1) * tab[1]
            + pltpu.roll(x, ROPE_HALF, axis=1) * tab[2])


def _diff_kernel(lam_init, q_ref, k_ref, ke_ref, v_ref, ve_ref, tq_ref, tk_ref, lq1_ref, lk1_ref, lq2_ref, lk2_ref,
                 g_ref, o_ref, kr_ref, vr_ref):
    @pl.when(pl.program_id(2) == 0)
    def _():
        meta = slice(META_PAD, EXTRA)
        zeros = slice(0, META_PAD)
        for m in range(2):
            cols = slice(m * DIFF_DH, (m + 1) * DIFF_DH)
            kr_ref[m, 0:SEQ, :] = _rope(k_ref[:, cols].astype(f32), tk_ref[:, 0:SEQ, :]).astype(bf16)
            kr_ref[m, SEQ:L_REAL, :] = _rope(ke_ref[meta, cols].astype(f32), tk_ref[:, SEQ:L_REAL, :]).astype(bf16)
            kr_ref[m, L_REAL:LK, :] = ke_ref[zeros, cols]
        vr_ref[0:SEQ, :] = v_ref[...]
        vr_ref[SEQ:L_REAL, :] = ve_ref[meta, :]
        vr_ref[L_REAL:LK, :] = ve_ref[zeros, :]

    lam = (jnp.exp(jnp.sum(lq1_ref[...] * lk1_ref[...], axis=1, keepdims=True))
           - jnp.exp(jnp.sum(lq2_ref[...] * lk2_ref[...], axis=1, keepdims=True)) + lam_init)
    tq = tq_ref[...]
    key_ok = lax.broadcasted_iota(jnp.int32, (DIFF_TQ, LK), 1) < L_REAL
    exp2_scale = (DIFF_DH ** -0.5) * math.log2(math.e)
    outs = []
    for m in range(2):
        qx = q_ref[:, m * DIFF_DH:(m + 1) * DIFF_DH].astype(f32)
        qr = _rope(qx, tq).astype(bf16)
        s = jnp.where(key_ok, _dot_nt(qr, kr_ref[m]), NEG_BIG)
        e = jnp.exp2((s - jnp.max(s, axis=1, keepdims=True)) * exp2_scale)
        inv = 1.0 / jnp.sum(e, axis=1, keepdims=True)
        outs.append(_dot(e.astype(bf16), vr_ref[...]) * inv)
    o = outs[0] - lam * outs[1]
    o_ref[...] = (_rms(o, g_ref[...]) * (1.0 - lam_init)).astype(o_ref.dtype)


def _diff_attention(proj, rope_tab, lq1, lk1, lq2, lk2, g_norm, lam_init):
    nq = SEQ // DIFF_TQ
    ck = DIFF_QK // DIFF_DV
    cv = 2 * ck
    ex = TX // EXTRA
    vec = pl.BlockSpec((1, DIFF_DH), lambda b, h, i: (0, 0))
    return pl.pallas_call(
        functools.partial(_diff_kernel, lam_init),
        out_shape=jax.ShapeDtypeStruct((TX, DIFF_V), bf16),
        grid=(BATCH, DIFF_HEADS, nq),
        in_specs=[pl.BlockSpec((DIFF_TQ, DIFF_DV), lambda b, h, i: (b * nq + i, h)),
                  pl.BlockSpec((SEQ, DIFF_DV), lambda b, h, i: (b, ck + h)),
                  pl.BlockSpec((EXTRA, DIFF_DV), lambda b, h, i: (ex + b, ck + h)),
                  pl.BlockSpec((SEQ, DIFF_DV), lambda b, h, i: (b, cv + h)),
                  pl.BlockSpec((EXTRA, DIFF_DV), lambda b, h, i: (ex + b, cv + h)),
                  pl.BlockSpec((3, DIFF_TQ, DIFF_DH), lambda b, h, i: (0, i, 0)),
                  pl.BlockSpec((3, LK, DIFF_DH), lambda b, h, i: (0, 0, 0)),
                  vec, vec, vec, vec,
                  pl.BlockSpec((1, DIFF_DV), lambda b, h, i: (0, 0))],
        out_specs=pl.BlockSpec((DIFF_TQ, DIFF_DV), lambda b, h, i: (b * nq + i, h)),
        scratch_shapes=[pltpu.VMEM((2, LK, DIFF_DH), bf16), pltpu.VMEM((LK, DIFF_DV), bf16)],
        compiler_params=_params("parallel", "parallel", "arbitrary"),
        name="diff_attention",
    )(proj, proj, proj, proj, proj, rope_tab, rope_tab, lq1, lk1, lq2, lk2, g_norm)


def _merge_kernel(og_ref, od_ref, za_ref, zb_ref, hs_ref, wa_ref, wb_ref, wo_ref, g_ref, hs2_ref, hn_ref, hnt_ref):
    y = (jax.nn.sigmoid(za_ref[...]) * _dot(og_ref[...], wa_ref[...])
         + jax.nn.sigmoid(zb_ref[...]) * _dot(od_ref[...], wb_ref[...]))
    hs2 = hs_ref[...] + _dot(y.astype(bf16), wo_ref[...])
    hs2_ref[...] = hs2
    hn = _rms(hs2, g_ref[...])
    hn_ref[...] = hn.astype(hn_ref.dtype)
    hnt_ref[...] = hn.T.astype(hnt_ref.dtype)


def _merge(o_gla, o_diff, zab, hs, wa, wb, wo, g_ffn):
    tm = MERGE_TM
    row = lambda i: (i, 0)
    fixed = lambda i: (0, 0)
    wspec = pl.BlockSpec((D_MODEL, D_MODEL), fixed, pipeline_mode=pl.Buffered(1))
    return pl.pallas_call(
        _merge_kernel,
        out_shape=(jax.ShapeDtypeStruct((TX, D_MODEL), f32), jax.ShapeDtypeStruct((TX, D_MODEL), bf16),
                   jax.ShapeDtypeStruct((D_MODEL, TX), bf16)),
        grid=(TX // tm,),
        in_specs=[pl.BlockSpec((tm, GLA_V), row),
                  pl.BlockSpec((tm, DIFF_V), row),
                  pl.BlockSpec((tm, D_MODEL), lambda i: (i, 0)),
                  pl.BlockSpec((tm, D_MODEL), lambda i: (i, 1)),
                  pl.BlockSpec((tm, D_MODEL), row),
                  wspec, wspec, wspec,
                  pl.BlockSpec((1, D_MODEL), fixed)],
        out_specs=(pl.BlockSpec((tm, D_MODEL), row), pl.BlockSpec((tm, D_MODEL), row),
                   pl.BlockSpec((D_MODEL, tm), lambda i: (0, i))),
        compiler_params=_params("parallel"),
        name="branch_merge",
    )(o_gla, o_diff, zab, zab, hs, wa, wb, wo, g_ffn)


def _top16(s, iota, break_ties):
    vals = []
    if not break_ties:
        for r in range(PEER_TOPK):
            m = jnp.max(s, axis=0, keepdims=True)
            s = jnp.where(s == m, -RANK_MARK * (r + 1), s)
            vals.append(m)
        rank = jnp.where(s <= -RANK_MARK, s * (-1.0 / RANK_MARK) - 1.0, float(PEER_TOPK))
        return vals, rank
    rank = jnp.full(s.shape, float(PEER_TOPK), f32)
    for r in range(PEER_TOPK):
        m = jnp.max(s, axis=0, keepdims=True)
        first = jnp.min(jnp.where(s == m, iota, float(PEER_NKEYS)), axis=0, keepdims=True)
        hit = iota == first
        rank = jnp.where(hit, float(r), rank)
        s = jnp.where(hit, -jnp.inf, s)
        vals.append(m)
    return vals, rank


def _select_head(s0, s1, iota, ids, break_ties):
    K = PEER_TOPK
    tb = s0.shape[1]
    v0, rank0 = _top16(s0, iota, break_ties)
    v1, rank1 = _top16(s1, iota, break_ties)
    sa = jnp.concatenate(v0, axis=0)
    sb = jnp.concatenate(v1, axis=0)
    cand = jnp.concatenate([sa[0:1] + sb]
                           + [sa[a:a + 1] + sb[0:8] for a in range(1, 8)]
                           + [sa[8:16] + sb[0:1]], axis=0)
    top = cand[0:1]
    z = jnp.zeros((1, tb), f32)
    for _ in range(K):
        m = jnp.max(cand, axis=0, keepdims=True)
        hit = cand == m
        if break_ties:
            first = jnp.min(jnp.where(hit, ids, 1e9), axis=0, keepdims=True)
            hit = ids == first
        cand = jnp.where(hit, -jnp.inf, cand)
        z = z + jnp.exp(m - top)
    taken = jnp.where(cand == -jnp.inf, 1.0, 0.0)
    cnt = ([jnp.sum(taken[0:16], axis=0, keepdims=True)]
           + [jnp.sum(taken[8 + 8 * a:16 + 8 * a], axis=0, keepdims=True) for a in range(1, 8)]
           + [taken[72 + a:73 + a] for a in range(8)])
    cnt0 = jnp.zeros((PEER_NKEYS, tb), f32)
    for a in range(K):
        cnt0 = jnp.where(rank0 == float(a), cnt[a], cnt0)
    e0 = jnp.exp(s0 - v0[0]) * (1.0 / z)
    e1 = jnp.exp(s1 - v1[0])
    marked = (jnp.sum((rank0 < float(K)).astype(f32), axis=0, keepdims=True),
              jnp.sum((rank1 < float(K)).astype(f32), axis=0, keepdims=True),
              jnp.sum(taken, axis=0, keepdims=True))
    excess = jnp.max(sum(jnp.abs(n - float(K)) for n in marked))
    return (rank1, cnt0, e0, e1), excess


def _peer_select_kernel(hn_ref, wq_ref, keys_ref, rank1_ref, cnt0_ref, e0_ref, e1_ref, q_sc):
    tb = SEL_TB
    K = PEER_TOPK
    q_sc[...] = _dot(hn_ref[...], wq_ref[...]).astype(bf16)
    iota = lax.broadcasted_iota(jnp.int32, (PEER_NKEYS, tb), 0).astype(f32)
    i16 = lax.broadcasted_iota(jnp.int32, (K, tb), 0).astype(f32)
    i8 = lax.broadcasted_iota(jnp.int32, (8, tb), 0).astype(f32)
    ids = jnp.concatenate([i16] + [a * float(K) + i8 for a in range(1, 8)] + [(i8 + 8.0) * float(K)], axis=0)

    def head(h, carry):
        c0 = pl.multiple_of(h * (2 * PEER_DKEY), 2 * PEER_DKEY)
        s0 = _dot_nt(keys_ref[h, 0], q_sc[:, pl.ds(c0, PEER_DKEY)])
        s1 = _dot_nt(keys_ref[h, 1], q_sc[:, pl.ds(c0 + PEER_DKEY, PEER_DKEY)])

        def store(rank1, cnt0, e0, e1):
            rank1_ref[h] = rank1.astype(bf16)
            cnt0_ref[h] = cnt0
            e0_ref[h] = e0
            e1_ref[h] = e1.astype(bf16)

        tables, excess = _select_head(s0, s1, iota, ids, False)
        store(*tables)

        @pl.when(excess > 0.0)
        def _():
            store(*_select_head(s0, s1, iota, ids, True)[0])

        return carry

    lax.fori_loop(0, PEER_HEADS, head, 0)


def _peer_select(hn, wq, keys):
    tb = SEL_TB
    sel = lambda dt: jax.ShapeDtypeStruct((PEER_HEADS, PEER_NKEYS, TX), dt)
    sel_spec = pl.BlockSpec((PEER_HEADS, PEER_NKEYS, tb), lambda i: (0, 0, i))
    return pl.pallas_call(
        _peer_select_kernel,
        out_shape=(sel(bf16), sel(f32), sel(f32), sel(bf16)),
        grid=(TX // tb,),
        in_specs=[pl.BlockSpec((tb, D_MODEL), lambda i: (i, 0)),
                  pl.BlockSpec((D_MODEL, PEER_HEADS * 2 * PEER_DKEY), lambda i: (0, 0)),
                  pl.BlockSpec((PEER_HEADS, 2, PEER_NKEYS, PEER_DKEY), lambda i: (0, 0, 0, 0))],
        out_specs=(sel_spec,) * 4,
        scratch_shapes=[pltpu.VMEM((tb, PEER_HEADS * 2 * PEER_DKEY), bf16)],
        compiler_params=_params("parallel"),
        name="peer_select",
    )(hn, wq, keys)


def _peer_kernel(hnt_ref, u_ref, v_ref, rank1_ref, cnt0_ref, e0_ref, e1_ref, hs_ref, g_ref, o_ref,
                 acc_ref, a_ref, p_ref):
    j = pl.program_id(1)
    n_blocks = pl.num_programs(1) - 1
    groups = PEER_EB // PEER_NKEYS

    @pl.when(j == 0)
    def _():
        acc_ref[...] = jnp.zeros_like(acc_ref)

    @pl.when(j > 0)
    def _():
        for gi in range(groups):
            i = (j - 1) * groups + gi
            cnts = [jnp.broadcast_to(cnt0_ref[h, pl.ds(i, 1), :], (BF16_ROWS, PEER_TB)).astype(bf16)
                    for h in range(PEER_HEADS)]
            e0s = [jnp.broadcast_to(e0_ref[h, pl.ds(i, 1), :], (BF16_ROWS, PEER_TB)).astype(bf16)
                   for h in range(PEER_HEADS)]
            for r in range(PEER_NKEYS // BF16_ROWS):
                keys = slice(r * BF16_ROWS, (r + 1) * BF16_ROWS)
                rows = slice(gi * PEER_NKEYS + r * BF16_ROWS, gi * PEER_NKEYS + (r + 1) * BF16_ROWS)
                a = a_ref[rows, :]
                act = 0.5 * a * (1.0 + lax.erf(a * (2.0 ** -0.5)))
                w = jnp.zeros((BF16_ROWS, PEER_TB), bf16)
                for h in range(PEER_HEADS):
                    w = w + jnp.where(rank1_ref[h, keys, :] < cnts[h], e1_ref[h, keys, :], jnp.zeros((), bf16)) * e0s[h]
                p_ref[rows, :] = w * act.astype(bf16)
        acc_ref[...] += _dot_tn(v_ref[...], p_ref[...])

    @pl.when(j < n_blocks)
    def _():
        a_ref[...] = _dot(u_ref[...], hnt_ref[...])

    @pl.when(j == n_blocks)
    def _():
        hs3 = hs_ref[...] + acc_ref[...].T
        o_ref[...] = _rms(hs3, g_ref[...])


def _peer(hnt, u, v, rank1, cnt0, e0, e1, hs2, g_final):
    tb, eb = PEER_TB, PEER_EB
    n_blocks = PEER_N // eb
    sel_spec = pl.BlockSpec((PEER_HEADS, PEER_NKEYS, tb), lambda i, j: (0, 0, i))
    return pl.pallas_call(
        _peer_kernel,
        out_shape=jax.ShapeDtypeStruct((TX, D_MODEL), f32),
        grid=(TX // tb, n_blocks + 1),
        in_specs=[pl.BlockSpec((D_MODEL, tb), lambda i, j: (0, i)),
                  pl.BlockSpec((eb, D_MODEL), lambda i, j: (jnp.minimum(j, n_blocks - 1), 0)),
                  pl.BlockSpec((eb, D_MODEL), lambda i, j: (jnp.maximum(j - 1, 0), 0)),
                  sel_spec, sel_spec, sel_spec, sel_spec,
                  pl.BlockSpec((tb, D_MODEL), lambda i, j: (i, 0), pipeline_mode=pl.Buffered(1)),
                  pl.BlockSpec((1, D_MODEL), lambda i, j: (0, 0))],
        out_specs=pl.BlockSpec((tb, D_MODEL), lambda i, j: (i, 0)),
        scratch_shapes=[pltpu.VMEM((D_MODEL, tb), f32), pltpu.VMEM((eb, tb), f32), pltpu.VMEM((eb, tb), bf16)],
        compiler_params=_params("parallel", "arbitrary"),
        name="peer_experts",
    )(hnt, u, v, rank1, cnt0, e0, e1, hs2, g_final)


def _rope_tables():
    pos = jnp.concatenate([jnp.arange(N_META, N_META + SEQ, dtype=jnp.int32),
                           jnp.arange(N_META, dtype=jnp.int32),
                           jnp.zeros((META_PAD,), jnp.int32)])
    inv = 1.0 / (ROPE_THETA ** (jnp.arange(ROPE_HALF, dtype=f32) / ROPE_HALF))
    ang = pos.astype(f32)[:, None] * inv[None, :]
    cos, sin = jnp.cos(ang), jnp.sin(ang)
    rest = DIFF_DH - ROPE_DIMS
    zero, zrest = jnp.zeros_like(sin), jnp.zeros((LK, rest), f32)
    return jnp.stack([
        jnp.concatenate([cos, cos, jnp.ones((LK, rest), f32)], axis=1),
        jnp.concatenate([-sin, zero, zrest], axis=1),
        jnp.concatenate([zero, sin, zrest], axis=1)])


def kernel(x, meta_tokens, g_mix, w_in, gla_w2_fwd, gla_b_fwd, gla_w2_bwd, gla_b_bwd, gla_g_norm, diff_lq1, diff_lk1, diff_lq2, diff_lk2, diff_g_norm, w_branch_gla, w_branch_diff, w_out, g_ffn, peer_w_q, peer_sub_keys, peer_u, peer_v, g_final):
    assert w_in.shape[0] == 1, "single-layer block only"
    l = 0
    lam_init = 0.8 - 0.6 * math.exp(-0.3 * l)
    xs = x.reshape(TX, D_MODEL)
    extra = jnp.concatenate([jnp.zeros((META_PAD, D_MODEL), x.dtype), meta_tokens.astype(x.dtype)], axis=0)
    extras = jnp.tile(extra, (BATCH, 1))
    rope_tab = _rope_tables()

    wt = jnp.swapaxes(w_in[l], 0, 1).astype(bf16)
    o_gq, o_gk, o_gv, o_gr, o_lr, o_dq, o_dk, o_dv, o_za, o_zb = IN_OFFSETS
    w_lr = jnp.pad(wt[o_lr:o_lr + 2 * GLA_LOWRANK], ((0, LANE - 2 * GLA_LOWRANK), (0, 0)))
    w2f = jnp.pad(gla_w2_fwd[l], ((0, LANE - GLA_LOWRANK), (0, 0))).astype(bf16)
    w2b = jnp.pad(gla_w2_bwd[l], ((GLA_LOWRANK, LANE - 2 * GLA_LOWRANK), (0, 0))).astype(bf16)

    h = _norm_rows(xs, extras, g_mix[l][None])
    proj_gla, v_bf = _project(h, wt, o_gq, 2 * GLA_QK + GLA_V, TP, bf16, "in_proj_gla", table=peer_v[l])
    proj_diff, u_bf = _project(h, wt, o_dq, 2 * DIFF_QK + DIFF_V, TP, bf16, "in_proj_diff", table=peer_u[l])
    gate_gla = _project(h, wt, o_gr, GLA_V, TX, f32, "in_proj_gla_gate")
    gate_merge = _project(h, wt, o_za, 2 * D_MODEL, TX, f32, "in_proj_merge_gates")
    logf_f, logf_b = _decay(h, w_lr, w2f, w2b, gla_b_fwd[l][None], gla_b_bwd[l][None])
    o_gla = _gla(proj_gla, gate_gla, logf_f, logf_b, gla_g_norm[l][None])
    o_diff = _diff_attention(proj_diff, rope_tab, diff_lq1[l][None], diff_lk1[l][None], diff_lq2[l][None],
                             diff_lk2[l][None], diff_g_norm[l][None], lam_init)
    hs2, hn, hnt = _merge(o_gla, o_diff, gate_merge, xs, w_branch_gla[l].astype(bf16),
                          w_branch_diff[l].astype(bf16), w_out[l].astype(bf16), g_ffn[l][None])
    rank1, cnt0, e0, e1 = _peer_select(hn, peer_w_q[l].astype(bf16), peer_sub_keys[l].astype(bf16))
    out = _peer(hnt, u_bf, v_bf, rank1, cnt0, e0, e1, hs2, g_final[None])
    return out.reshape(BATCH, SEQ, D_MODEL)
```

```python
import functools
import math

import jax
import jax.numpy as jnp
from jax import lax
from jax.experimental import pallas as pl
from jax.experimental.pallas import tpu as pltpu

f32 = jnp.float32
bf16 = jnp.bfloat16

D_MODEL = 2048
BATCH = 4
SEQ = 2048
N_META = 16
EPS = 1e-6
LANE = 128
BF16_ROWS = 16
EXTRA = LANE
META_PAD = EXTRA - N_META
TX = BATCH * SEQ
TP = TX + BATCH * EXTRA
L_REAL = SEQ + N_META
LK = SEQ + EXTRA

GLA_HEADS = 4
GLA_DK = 256
GLA_DV = 512
GLA_QK = GLA_HEADS * GLA_DK
GLA_V = GLA_HEADS * GLA_DV
GLA_LOWRANK = 16
GLA_TAU = 16.0
GLA_CHUNK = 64
GLA_UNROLL = 32

DIFF_HEADS = 8
DIFF_DH = 128
DIFF_DV = 256
DIFF_QK = DIFF_HEADS * 2 * DIFF_DH
DIFF_V = DIFF_HEADS * DIFF_DV
ROPE_THETA = 500000.0
ROPE_DIMS = DIFF_DH // 4
ROPE_HALF = ROPE_DIMS // 2

PEER_HEADS = 8
PEER_NKEYS = 128
PEER_N = PEER_NKEYS * PEER_NKEYS
PEER_DKEY = 128
PEER_TOPK = 16

IN_SIZES = (GLA_QK, GLA_QK, GLA_V, GLA_V, 2 * GLA_LOWRANK, DIFF_QK, DIFF_QK, DIFF_V, D_MODEL, D_MODEL)
IN_OFFSETS = tuple(sum(IN_SIZES[:i]) for i in range(len(IN_SIZES)))

VMEM_LIMIT = 56 * 1024 * 1024
NEG_BIG = -1e30
RANK_MARK = 2.0 ** 100

ROW_BLOCK = 512
COL_BLOCK = 2048
CAST_ROWS = 512
DIFF_TQ = 2048
MERGE_TM = 256
SEL_TB = 512
PEER_TB = 512
PEER_EB = 1024


def _params(*sem):
    return pltpu.CompilerParams(dimension_semantics=sem, vmem_limit_bytes=VMEM_LIMIT)


def _rms(x, g):
    return x * lax.rsqrt(jnp.mean(x * x, axis=-1, keepdims=True) + EPS) * g


def _dot(a, b):
    return jnp.dot(a, b, preferred_element_type=f32)


def _dot_nt(a, b):
    return lax.dot_general(a, b, (((1,), (1,)), ((), ())), preferred_element_type=f32)


def _dot_tn(a, b):
    return lax.dot_general(a, b, (((0,), (0,)), ((), ())), preferred_element_type=f32)


def _norm_kernel(x_ref, e_ref, g_ref, o_ref):
    is_seq = pl.program_id(0) < TX // ROW_BLOCK

    @pl.when(is_seq)
    def _():
        o_ref[...] = _rms(x_ref[...], g_ref[...]).astype(o_ref.dtype)

    @pl.when(jnp.logical_not(is_seq))
    def _():
        o_ref[...] = _rms(e_ref[...], g_ref[...]).astype(o_ref.dtype)


def _norm_rows(x, extras, g):
    assert BATCH * EXTRA == ROW_BLOCK
    last = TX // ROW_BLOCK - 1
    return pl.pallas_call(
        _norm_kernel,
        out_shape=jax.ShapeDtypeStruct((TP, D_MODEL), bf16),
        grid=(TP // ROW_BLOCK,),
        in_specs=[pl.BlockSpec((ROW_BLOCK, D_MODEL), lambda i: (jnp.minimum(i, last), 0)),
                  pl.BlockSpec((ROW_BLOCK, D_MODEL), lambda i: (0, 0)),
                  pl.BlockSpec((1, D_MODEL), lambda i: (0, 0))],
        out_specs=pl.BlockSpec((ROW_BLOCK, D_MODEL), lambda i: (i, 0)),
        compiler_params=_params("arbitrary"),
        name="mix_norm",
    )(x, extras, g)


def _mm_kernel(a_ref, wt_ref, o_ref):
    o_ref[...] = _dot_nt(a_ref[...], wt_ref[...]).astype(o_ref.dtype)


def _mm_cast_kernel(transpose, a_ref, wt_ref, t_ref, o_ref, tb_ref):
    o_ref[...] = _dot_nt(a_ref[...], wt_ref[...]).astype(o_ref.dtype)
    if transpose:
        tb_ref[0] = t_ref[...].T.astype(tb_ref.dtype)
    else:
        tb_ref[...] = t_ref[...].astype(tb_ref.dtype)


def _project(h, wt, r0, n, rows, out_dtype, name, table=None, transpose=False):
    assert r0 % BF16_ROWS == 0 and n % COL_BLOCK == 0 and rows % ROW_BLOCK == 0
    nj, ni = n // COL_BLOCK, rows // ROW_BLOCK
    in_specs = [pl.BlockSpec((ROW_BLOCK, D_MODEL), lambda j, i: (i, 0)),
                pl.BlockSpec((pl.Element(COL_BLOCK), pl.Element(D_MODEL)),
                             lambda j, i: (pl.multiple_of(r0 + j * COL_BLOCK, BF16_ROWS), 0))]
    out_spec = pl.BlockSpec((ROW_BLOCK, COL_BLOCK), lambda j, i: (i, j))
    out_shape = jax.ShapeDtypeStruct((rows, n), out_dtype)
    if table is None:
        return pl.pallas_call(
            _mm_kernel, out_shape=out_shape, grid=(nj, ni), in_specs=in_specs, out_specs=out_spec,
            compiler_params=_params("parallel", "parallel"), name=name,
        )(h, wt)
    n_slabs = table.shape[0] // CAST_ROWS
    assert nj * ni >= n_slabs
    slab = pl.BlockSpec((CAST_ROWS, D_MODEL), lambda j, i: (jnp.minimum(j * ni + i, n_slabs - 1), 0))
    if transpose:
        cast_shape = jax.ShapeDtypeStruct((n_slabs, D_MODEL, CAST_ROWS), bf16)
        cast_spec = pl.BlockSpec((1, D_MODEL, CAST_ROWS), lambda j, i: (jnp.minimum(j * ni + i, n_slabs - 1), 0, 0))
    else:
        cast_shape, cast_spec = jax.ShapeDtypeStruct(table.shape, bf16), slab
    return pl.pallas_call(
        functools.partial(_mm_cast_kernel, transpose),
        out_shape=(out_shape, cast_shape),
        grid=(nj, ni), in_specs=in_specs + [slab], out_specs=(out_spec, cast_spec),
        compiler_params=_params("arbitrary", "arbitrary"), name=name,
    )(h, wt, table)


def _decay_kernel(h_ref, wlr_ref, w2f_ref, w2b_ref, bf_ref, bb_ref, of_ref, ob_ref):
    lr = _dot_nt(h_ref[...], wlr_ref[...]).astype(bf16)
    zf = _dot(lr, w2f_ref[...]) + bf_ref[...]
    zb = _dot(lr, w2b_ref[...]) + bb_ref[...]
    of_ref[...] = jax.nn.log_sigmoid(zf) * (1.0 / GLA_TAU)
    ob_ref[...] = jax.nn.log_sigmoid(zb) * (1.0 / GLA_TAU)


def _decay(h, wlr, w2f, w2b, b_f, b_b):
    row = lambda i: (i, 0)
    fixed = lambda i: (0, 0)
    return pl.pallas_call(
        _decay_kernel,
        out_shape=(jax.ShapeDtypeStruct((TP, GLA_QK), f32),) * 2,
        grid=(TP // ROW_BLOCK,),
        in_specs=[pl.BlockSpec((ROW_BLOCK, D_MODEL), row),
                  pl.BlockSpec((LANE, D_MODEL), fixed),
                  pl.BlockSpec((LANE, GLA_QK), fixed),
                  pl.BlockSpec((LANE, GLA_QK), fixed),
                  pl.BlockSpec((1, GLA_QK), fixed),
                  pl.BlockSpec((1, GLA_QK), fixed)],
        out_specs=(pl.BlockSpec((ROW_BLOCK, GLA_QK), row),) * 2,
        compiler_params=_params("parallel"),
        name="gla_decay",
    )(h, wlr, w2f, w2b, b_f, b_b)


def _gla_kernel(q_ref, k_ref, v_ref, ke_ref, ve_ref, ff_ref, fb_ref, fe_ref, r_ref, g_ref, o_ref,
                accf_ref, accb_ref, sf_ref, sb_ref):
    C = GLA_CHUNK
    n_chunks = SEQ // C
    row = lax.broadcasted_iota(jnp.int32, (C, C), 0)
    col = lax.broadcasted_iota(jnp.int32, (C, C), 1)
    time = lax.broadcasted_iota(jnp.int32, (C, GLA_DK), 0)

    def decay_sums(lf, prefix):
        cum = lf
        shift = 1
        while shift < C:
            if prefix:
                cum = cum + jnp.where(time >= shift, pltpu.roll(cum, shift, axis=0), 0.0)
            else:
                cum = cum + jnp.where(time < C - shift, pltpu.roll(cum, C - shift, axis=0), 0.0)
            shift *= 2
        total_row = C - 1 if prefix else 0
        return cum, cum[total_row:total_row + 1, :]

    def advance(s_ref, k, v, cum, tot):
        tot_col = jnp.broadcast_to(tot, (8, GLA_DK)).T[:, 0:1]
        k_st = (k * jnp.exp(tot - cum)).astype(bf16)
        s_ref[...] = s_ref[...] * jnp.exp(tot_col) + _dot_tn(k_st, v)

    def chunk(n, f_ref, prefix, keep_mask, acc_ref, s_ref):
        rows = pl.ds(pl.multiple_of(n * C, C), C)
        cum, tot = decay_sums(f_ref[rows, :], prefix)
        q = q_ref[rows, :].astype(f32) * (GLA_DK ** -0.5)
        k = k_ref[rows, :].astype(f32)
        v = v_ref[rows, :]
        q_in = (q * jnp.exp(cum)).astype(bf16)
        k_in = (k * jnp.exp(-cum)).astype(bf16)
        a = jnp.where(keep_mask, _dot_nt(q_in, k_in), 0.0)
        acc_ref[rows, :] = _dot(a.astype(bf16), v) + _dot(q_in, s_ref[...].astype(bf16))
        advance(s_ref, k, v, cum, tot)

    sf_ref[...] = jnp.zeros_like(sf_ref)
    sb_ref[...] = jnp.zeros_like(sb_ref)

    meta_rows = slice(EXTRA - C, EXTRA)
    cum_e, tot_e = decay_sums(fe_ref[meta_rows, :], True)
    advance(sf_ref, ke_ref[meta_rows, :].astype(f32), ve_ref[meta_rows, :], cum_e, tot_e)

    def both(m, carry):
        for u in range(GLA_UNROLL):
            n = m * GLA_UNROLL + u
            chunk(n, ff_ref, True, col <= row, accf_ref, sf_ref)
            chunk(n_chunks - 1 - n, fb_ref, False, col > row, accb_ref, sb_ref)
        return carry

    lax.fori_loop(0, n_chunks // GLA_UNROLL, both, 0)

    def fin(n, carry):
        rows = pl.ds(pl.multiple_of(n * LANE, LANE), LANE)
        o = _rms(accf_ref[rows, :] + accb_ref[rows, :], g_ref[...])
        o_ref[rows, :] = (o * jax.nn.silu(r_ref[rows, :])).astype(o_ref.dtype)
        return carry

    lax.fori_loop(0, SEQ // LANE, fin, 0)


def _gla(proj, gate, logf_f, logf_b, g_norm):
    kq = GLA_QK // GLA_DK
    kv = (2 * GLA_QK) // GLA_DV
    ex = TX // EXTRA
    return pl.pallas_call(
        _gla_kernel,
        out_shape=jax.ShapeDtypeStruct((TX, GLA_V), bf16),
        grid=(BATCH, GLA_HEADS),
        in_specs=[pl.BlockSpec((SEQ, GLA_DK), lambda b, h: (b, h)),
                  pl.BlockSpec((SEQ, GLA_DK), lambda b, h: (b, kq + h)),
                  pl.BlockSpec((SEQ, GLA_DV), lambda b, h: (b, kv + h)),
                  pl.BlockSpec((EXTRA, GLA_DK), lambda b, h: (ex + b, kq + h)),
                  pl.BlockSpec((EXTRA, GLA_DV), lambda b, h: (ex + b, kv + h)),
                  pl.BlockSpec((SEQ, GLA_DK), lambda b, h: (b, h)),
                  pl.BlockSpec((SEQ, GLA_DK), lambda b, h: (b, h)),
                  pl.BlockSpec((EXTRA, GLA_DK), lambda b, h: (ex + b, h)),
                  pl.BlockSpec((SEQ, GLA_DV), lambda b, h: (b, h)),
                  pl.BlockSpec((1, GLA_DV), lambda b, h: (0, 0))],
        out_specs=pl.BlockSpec((SEQ, GLA_DV), lambda b, h: (b, h)),
        scratch_shapes=[pltpu.VMEM((SEQ, GLA_DV), f32), pltpu.VMEM((SEQ, GLA_DV), f32),
                        pltpu.VMEM((GLA_DK, GLA_DV), f32), pltpu.VMEM((GLA_DK, GLA_DV), f32)],
        compiler_params=_params("parallel", "parallel"),
        name="gla_mixer",
    )(proj, proj, proj, proj, proj, logf_f, logf_b, logf_f, gate, g_norm)


def _rope(x, tab):
    return (x * tab[0]
            + pltpu.roll(x, LANE - ROPE_HALF, axis=1) * tab[1]
            + pltpu.roll(x, ROPE_HALF, axis=1) * tab[2])


def _diff_kernel(lam_init, q_ref, k_ref, ke_ref, v_ref, ve_ref, tq_ref, tk_ref, lq1_ref, lk1_ref, lq2_ref, lk2_ref,
                 g_ref, o_ref, kr_ref, vr_ref):
    @pl.when(pl.program_id(2) == 0)
    def _():
        meta = slice(META_PAD, EXTRA)
        zeros = slice(0, META_PAD)
        for m in range(2):
            cols = slice(m * DIFF_DH, (m + 1) * DIFF_DH)
            kr_ref[m, 0:SEQ, :] = _rope(k_ref[:, cols].astype(f32), tk_ref[:, 0:SEQ, :]).astype(bf16)
            kr_ref[m, SEQ:L_REAL, :] = _rope(ke_ref[meta, cols].astype(f32), tk_ref[:, SEQ:L_REAL, :]).astype(bf16)
            kr_ref[m, L_REAL:LK, :] = ke_ref[zeros, cols]
        vr_ref[0:SEQ, :] = v_ref[...]
        vr_ref[SEQ:L_REAL, :] = ve_ref[meta, :]
        vr_ref[L_REAL:LK, :] = ve_ref[zeros, :]

    lam = (jnp.exp(jnp.sum(lq1_ref[...] * lk1_ref[...], axis=1, keepdims=True))
           - jnp.exp(jnp.sum(lq2_ref[...] * lk2_ref[...], axis=1, keepdims=True)) + lam_init)
    tq = tq_ref[...]
    key_ok = lax.broadcasted_iota(jnp.int32, (DIFF_TQ, LK), 1) < L_REAL
    exp2_scale = (DIFF_DH ** -0.5) * math.log2(math.e)
    outs = []
    for m in range(2):
        qx = q_ref[:, m * DIFF_DH:(m + 1) * DIFF_DH].astype(f32)
        qr = _rope(qx, tq).astype(bf16)
        s = jnp.where(key_ok, _dot_nt(qr, kr_ref[m]), NEG_BIG)
        e = jnp.exp2((s - jnp.max(s, axis=1, keepdims=True)) * exp2_scale)
        inv = 1.0 / jnp.sum(e, axis=1, keepdims=True)
        outs.append(_dot(e.astype(bf16), vr_ref[...]) * inv)
    o = outs[0] - lam * outs[1]
    o_ref[...] = (_rms(o, g_ref[...]) * (1.0 - lam_init)).astype(o_ref.dtype)


def _diff_attention(proj, rope_tab, lq1, lk1, lq2, lk2, g_norm, lam_init):
    nq = SEQ // DIFF_TQ
    ck = DIFF_QK // DIFF_DV
    cv = 2 * ck
    ex = TX // EXTRA
    vec = pl.BlockSpec((1, DIFF_DH), lambda b, h, i: (0, 0))
    return pl.pallas_call(
        functools.partial(_diff_kernel, lam_init),
        out_shape=jax.ShapeDtypeStruct((TX, DIFF_V), bf16),
        grid=(BATCH, DIFF_HEADS, nq),
        in_specs=[pl.BlockSpec((DIFF_TQ, DIFF_DV), lambda b, h, i: (b * nq + i, h)),
                  pl.BlockSpec((SEQ, DIFF_DV), lambda b, h, i: (b, ck + h)),
                  pl.BlockSpec((EXTRA, DIFF_DV), lambda b, h, i: (ex + b, ck + h)),
                  pl.BlockSpec((SEQ, DIFF_DV), lambda b, h, i: (b, cv + h)),
                  pl.BlockSpec((EXTRA, DIFF_DV), lambda b, h, i: (ex + b, cv + h)),
                  pl.BlockSpec((3, DIFF_TQ, DIFF_DH), lambda b, h, i: (0, i, 0)),
                  pl.BlockSpec((3, LK, DIFF_DH), lambda b, h, i: (0, 0, 0)),
                  vec, vec, vec, vec,
                  pl.BlockSpec((1, DIFF_DV), lambda b, h, i: (0, 0))],
        out_specs=pl.BlockSpec((DIFF_TQ, DIFF_DV), lambda b, h, i: (b * nq + i, h)),
        scratch_shapes=[pltpu.VMEM((2, LK, DIFF_DH), bf16), pltpu.VMEM((LK, DIFF_DV), bf16)],
        compiler_params=_params("parallel", "parallel", "arbitrary"),
        name="diff_attention",
    )(proj, proj, proj, proj, proj, rope_tab, rope_tab, lq1, lk1, lq2, lk2, g_norm)


def _merge_kernel(og_ref, od_ref, za_ref, zb_ref, hs_ref, wa_ref, wb_ref, wo_ref, g_ref, hs2_ref, hn_ref, hnt_ref):
    y = (jax.nn.sigmoid(za_ref[...]) * _dot(og_ref[...], wa_ref[...])
         + jax.nn.sigmoid(zb_ref[...]) * _dot(od_ref[...], wb_ref[...]))
    hs2 = hs_ref[...] + _dot(y.astype(bf16), wo_ref[...])
    hs2_ref[...] = hs2
    hn = _rms(hs2, g_ref[...])
    hn_ref[...] = hn.astype(hn_ref.dtype)
    hnt_ref[...] = hn.T.astype(hnt_ref.dtype)


def _merge(o_gla, o_diff, zab, hs, wa, wb, wo, g_ffn):
    tm = MERGE_TM
    row = lambda i: (i, 0)
    fixed = lambda i: (0, 0)
    wspec = pl.BlockSpec((D_MODEL, D_MODEL), fixed, pipeline_mode=pl.Buffered(1))
    return pl.pallas_call(
        _merge_kernel,
        out_shape=(jax.ShapeDtypeStruct((TX, D_MODEL), f32), jax.ShapeDtypeStruct((TX, D_MODEL), bf16),
                   jax.ShapeDtypeStruct((D_MODEL, TX), bf16)),
        grid=(TX // tm,),
        in_specs=[pl.BlockSpec((tm, GLA_V), row),
                  pl.BlockSpec((tm, DIFF_V), row),
                  pl.BlockSpec((tm, D_MODEL), lambda i: (i, 0)),
                  pl.BlockSpec((tm, D_MODEL), lambda i: (i, 1)),
                  pl.BlockSpec((tm, D_MODEL), row),
                  wspec, wspec, wspec,
                  pl.BlockSpec((1, D_MODEL), fixed)],
        out_specs=(pl.BlockSpec((tm, D_MODEL), row), pl.BlockSpec((tm, D_MODEL), row),
                   pl.BlockSpec((D_MODEL, tm), lambda i: (0, i))),
        compiler_params=_params("parallel"),
        name="branch_merge",
    )(o_gla, o_diff, zab, zab, hs, wa, wb, wo, g_ffn)


def _top16(s, iota, break_ties):
    vals = []
    if not break_ties:
        for r in range(PEER_TOPK):
            m = jnp.max(s, axis=0, keepdims=True)
            s = jnp.where(s == m, -RANK_MARK * (r + 1), s)
            vals.append(m)
        rank = jnp.where(s <= -RANK_MARK, s * (-1.0 / RANK_MARK) - 1.0, float(PEER_TOPK))
        return vals, rank
    rank = jnp.full(s.shape, float(PEER_TOPK), f32)
    for r in range(PEER_TOPK):
        m = jnp.max(s, axis=0, keepdims=True)
        first = jnp.min(jnp.where(s == m, iota, float(PEER_NKEYS)), axis=0, keepdims=True)
        hit = iota == first
        rank = jnp.where(hit, float(r), rank)
        s = jnp.where(hit, -jnp.inf, s)
        vals.append(m)
    return vals, rank


def _select_head(s0, s1, iota, ids, break_ties):
    K = PEER_TOPK
    tb = s0.shape[1]
    v0, rank0 = _top16(s0, iota, break_ties)
    v1, rank1 = _top16(s1, iota, break_ties)
    sa = jnp.concatenate(v0, axis=0)
    sb = jnp.concatenate(v1, axis=0)
    cand = jnp.concatenate([sa[0:1] + sb]
                           + [sa[a:a + 1] + sb[0:8] for a in range(1, 8)]
                           + [sa[8:16] + sb[0:1]], axis=0)
    top = cand[0:1]
    z = jnp.zeros((1, tb), f32)
    for _ in range(K):
        m = jnp.max(cand, axis=0, keepdims=True)
        hit = cand == m
        if break_ties:
            first = jnp.min(jnp.where(hit, ids, 1e9), axis=0, keepdims=True)
            hit = ids == first
        cand = jnp.where(hit, -jnp.inf, cand)
        z = z + jnp.exp(m - top)
    taken = jnp.where(cand == -jnp.inf, 1.0, 0.0)
    cnt = ([jnp.sum(taken[0:16], axis=0, keepdims=True)]
           + [jnp.sum(taken[8 + 8 * a:16 + 8 * a], axis=0, keepdims=True) for a in range(1, 8)]
           + [taken[72 + a:73 + a] for a in range(8)])
    cnt0 = jnp.zeros((PEER_NKEYS, tb), f32)
    for a in range(K):
        cnt0 = jnp.where(rank0 == float(a), cnt[a], cnt0)
    e0 = jnp.exp(s0 - v0[0]) * (1.0 / z)
    e1 = jnp.exp(s1 - v1[0])
    marked = (jnp.sum((rank0 < float(K)).astype(f32), axis=0, keepdims=True),
              jnp.sum((rank1 < float(K)).astype(f32), axis=0, keepdims=True),
              jnp.sum(taken, axis=0, keepdims=True))
    excess = jnp.max(sum(jnp.abs(n - float(K)) for n in marked))
    return (rank1, cnt0, e0, e1), excess


def _peer_select_kernel(hn_ref, wq_ref, keys_ref, rank1_ref, cnt0_ref, e0_ref, e1_ref, q_sc):
    tb = SEL_TB
    K = PEER_TOPK
    q_sc[...] = _dot(hn_ref[...], wq_ref[...]).astype(bf16)
    iota = lax.broadcasted_iota(jnp.int32, (PEER_NKEYS, tb), 0).astype(f32)
    i16 = lax.broadcasted_iota(jnp.int32, (K, tb), 0).astype(f32)
    i8 = lax.broadcasted_iota(jnp.int32, (8, tb), 0).astype(f32)
    ids = jnp.concatenate([i16] + [a * float(K) + i8 for a in range(1, 8)] + [(i8 + 8.0) * float(K)], axis=0)

    def head(h, carry):
        c0 = pl.multiple_of(h * (2 * PEER_DKEY), 2 * PEER_DKEY)
        s0 = _dot_nt(keys_ref[h, 0], q_sc[:, pl.ds(c0, PEER_DKEY)])
        s1 = _dot_nt(keys_ref[h, 1], q_sc[:, pl.ds(c0 + PEER_DKEY, PEER_DKEY)])

        def store(rank1, cnt0, e0, e1):
            rank1_ref[h] = rank1.astype(bf16)
            cnt0_ref[h] = cnt0
            e0_ref[h] = e0
            e1_ref[h] = e1.astype(bf16)

        tables, excess = _select_head(s0, s1, iota, ids, False)
        store(*tables)

        @pl.when(excess > 0.0)
        def _():
            store(*_select_head(s0, s1, iota, ids, True)[0])

        return carry

    lax.fori_loop(0, PEER_HEADS, head, 0)


def _peer_select(hn, wq, keys):
    tb = SEL_TB
    sel = lambda dt: jax.ShapeDtypeStruct((PEER_HEADS, PEER_NKEYS, TX), dt)
    sel_spec = pl.BlockSpec((PEER_HEADS, PEER_NKEYS, tb), lambda i: (0, 0, i))
    return pl.pallas_call(
        _peer_select_kernel,
        out_shape=(sel(bf16), sel(f32), sel(f32), sel(bf16)),
        grid=(TX // tb,),
        in_specs=[pl.BlockSpec((tb, D_MODEL), lambda i: (i, 0)),
                  pl.BlockSpec((D_MODEL, PEER_HEADS * 2 * PEER_DKEY), lambda i: (0, 0)),
                  pl.BlockSpec((PEER_HEADS, 2, PEER_NKEYS, PEER_DKEY), lambda i: (0, 0, 0, 0))],
        out_specs=(sel_spec,) * 4,
        scratch_shapes=[pltpu.VMEM((tb, PEER_HEADS * 2 * PEER_DKEY), bf16)],
        compiler_params=_params("parallel"),
        name="peer_select",
    )(hn, wq, keys)


def _peer_kernel(hnt_ref, u_ref, v_ref, rank1_ref, cnt0_ref, e0_ref, e1_ref, hs_ref, g_ref, o_ref,
                 acc_ref, a_ref, p_ref):
    j = pl.program_id(1)
    n_blocks = pl.num_programs(1) - 1
    groups = PEER_EB // PEER_NKEYS

    @pl.when(j == 0)
    def _():
        acc_ref[...] = jnp.zeros_like(acc_ref)

    @pl.when(j > 0)
    def _():
        for gi in range(groups):
            i = (j - 1) * groups + gi
            cnts = [jnp.broadcast_to(cnt0_ref[h, pl.ds(i, 1), :], (BF16_ROWS, PEER_TB)).astype(bf16)
                    for h in range(PEER_HEADS)]
            e0s = [jnp.broadcast_to(e0_ref[h, pl.ds(i, 1), :], (BF16_ROWS, PEER_TB)).astype(bf16)
                   for h in range(PEER_HEADS)]
            for r in range(PEER_NKEYS // BF16_ROWS):
                keys = slice(r * BF16_ROWS, (r + 1) * BF16_ROWS)
                rows = slice(gi * PEER_NKEYS + r * BF16_ROWS, gi * PEER_NKEYS + (r + 1) * BF16_ROWS)
                a = a_ref[rows, :]
                act = 0.5 * a * (1.0 + lax.erf(a * (2.0 ** -0.5)))
                w = jnp.zeros((BF16_ROWS, PEER_TB), bf16)
                for h in range(PEER_HEADS):
                    w = w + jnp.where(rank1_ref[h, keys, :] < cnts[h], e1_ref[h, keys, :], jnp.zeros((), bf16)) * e0s[h]
                p_ref[rows, :] = w * act.astype(bf16)
        acc_ref[...] += sum(_dot(v_ref[s], p_ref[s * CAST_ROWS:(s + 1) * CAST_ROWS, :])
                            for s in range(PEER_EB // CAST_ROWS))

    @pl.when(j < n_blocks)
    def _():
        a_ref[...] = _dot(u_ref[...], hnt_ref[...])

    @pl.when(j == n_blocks)
    def _():
        hs3 = hs_ref[...] + acc_ref[...].T
        o_ref[...] = _rms(hs3, g_ref[...])


def _peer(hnt, u, v, rank1, cnt0, e0, e1, hs2, g_final):
    tb, eb = PEER_TB, PEER_EB
    n_blocks = PEER_N // eb
    sel_spec = pl.BlockSpec((PEER_HEADS, PEER_NKEYS, tb), lambda i, j: (0, 0, i))
    return pl.pallas_call(
        _peer_kernel,
        out_shape=jax.ShapeDtypeStruct((TX, D_MODEL), f32),
        grid=(TX // tb, n_blocks + 1),
        in_specs=[pl.BlockSpec((D_MODEL, tb), lambda i, j: (0, i)),
                  pl.BlockSpec((eb, D_MODEL), lambda i, j: (jnp.minimum(j, n_blocks - 1), 0)),
                  pl.BlockSpec((eb // CAST_ROWS, D_MODEL, CAST_ROWS), lambda i, j: (jnp.maximum(j - 1, 0), 0, 0)),
                  sel_spec, sel_spec, sel_spec, sel_spec,
                  pl.BlockSpec((tb, D_MODEL), lambda i, j: (i, 0), pipeline_mode=pl.Buffered(1)),
                  pl.BlockSpec((1, D_MODEL), lambda i, j: (0, 0))],
        out_specs=pl.BlockSpec((tb, D_MODEL), lambda i, j: (i, 0)),
        scratch_shapes=[pltpu.VMEM((D_MODEL, tb), f32), pltpu.VMEM((eb, tb), f32), pltpu.VMEM((eb, tb), bf16)],
        compiler_params=_params("parallel", "arbitrary"),
        name="peer_experts",
    )(hnt, u, v, rank1, cnt0, e0, e1, hs2, g_final)


def _rope_tables():
    pos = jnp.concatenate([jnp.arange(N_META, N_META + SEQ, dtype=jnp.int32),
                           jnp.arange(N_META, dtype=jnp.int32),
                           jnp.zeros((META_PAD,), jnp.int32)])
    inv = 1.0 / (ROPE_THETA ** (jnp.arange(ROPE_HALF, dtype=f32) / ROPE_HALF))
    ang = pos.astype(f32)[:, None] * inv[None, :]
    cos, sin = jnp.cos(ang), jnp.sin(ang)
    rest = DIFF_DH - ROPE_DIMS
    zero, zrest = jnp.zeros_like(sin), jnp.zeros((LK, rest), f32)
    return jnp.stack([
        jnp.concatenate([cos, cos, jnp.ones((LK, rest), f32)], axis=1),
        jnp.concatenate([-sin, zero, zrest], axis=1),
        jnp.concatenate([zero, sin, zrest], axis=1)])


def kernel(x, meta_tokens, g_mix, w_in, gla_w2_fwd, gla_b_fwd, gla_w2_bwd, gla_b_bwd, gla_g_norm, diff_lq1, diff_lk1, diff_lq2, diff_lk2, diff_g_norm, w_branch_gla, w_branch_diff, w_out, g_ffn, peer_w_q, peer_sub_keys, peer_u, peer_v, g_final):
    assert w_in.shape[0] == 1, "single-layer block only"
    l = 0
    lam_init = 0.8 - 0.6 * math.exp(-0.3 * l)
    xs = x.reshape(TX, D_MODEL)
    extra = jnp.concatenate([jnp.zeros((META_PAD, D_MODEL), x.dtype), meta_tokens.astype(x.dtype)], axis=0)
    extras = jnp.tile(extra, (BATCH, 1))
    rope_tab = _rope_tables()

    wt = jnp.swapaxes(w_in[l], 0, 1).astype(bf16)
    o_gq, o_gk, o_gv, o_gr, o_lr, o_dq, o_dk, o_dv, o_za, o_zb = IN_OFFSETS
    w_lr = jnp.pad(wt[o_lr:o_lr + 2 * GLA_LOWRANK], ((0, LANE - 2 * GLA_LOWRANK), (0, 0)))
    w2f = jnp.pad(gla_w2_fwd[l], ((0, LANE - GLA_LOWRANK), (0, 0))).astype(bf16)
    w2b = jnp.pad(gla_w2_bwd[l], ((GLA_LOWRANK, LANE - 2 * GLA_LOWRANK), (0, 0))).astype(bf16)

    h = _norm_rows(xs, extras, g_mix[l][None])
    proj_gla, v_bf = _project(h, wt, o_gq, 2 * GLA_QK + GLA_V, TP, bf16, "in_proj_gla", table=peer_v[l],
                              transpose=True)
    proj_diff, u_bf = _project(h, wt, o_dq, 2 * DIFF_QK + DIFF_V, TP, bf16, "in_proj_diff", table=peer_u[l])
    gate_gla = _project(h, wt, o_gr, GLA_V, TX, f32, "in_proj_gla_gate")
    gate_merge = _project(h, wt, o_za, 2 * D_MODEL, TX, f32, "in_proj_merge_gates")
    logf_f, logf_b = _decay(h, w_lr, w2f, w2b, gla_b_fwd[l][None], gla_b_bwd[l][None])
    o_gla = _gla(proj_gla, gate_gla, logf_f, logf_b, gla_g_norm[l][None])
    o_diff = _diff_attention(proj_diff, rope_tab, diff_lq1[l][None], diff_lk1[l][None], diff_lq2[l][None],
                             diff_lk2[l][None], diff_g_norm[l][None], lam_init)
    hs2, hn, hnt = _merge(o_gla, o_diff, gate_merge, xs, w_branch_gla[l].astype(bf16),
                          w_branch_diff[l].astype(bf16), w_out[l].astype(bf16), g_ffn[l][None])
    rank1, cnt0, e0, e1 = _peer_select(hn, peer_w_q[l].astype(bf16), peer_sub_keys[l].astype(bf16))
    out = _peer(hnt, u_bf, v_bf, rank1, cnt0, e0, e1, hs2, g_final[None])
    return out.reshape(BATCH, SEQ, D_MODEL)
```

```python
import functools
import math

import jax
import jax.numpy as jnp
from jax import lax
from jax.experimental import pallas as pl
from jax.experimental.pallas import tpu as pltpu

f32 = jnp.float32
bf16 = jnp.bfloat16

D_MODEL = 2048
BATCH = 4
SEQ = 2048
N_META = 16
EPS = 1e-6
LANE = 128
BF16_ROWS = 16
EXTRA = LANE
META_PAD = EXTRA - N_META
TX = BATCH * SEQ
TP = TX + BATCH * EXTRA
L_REAL = SEQ + N_META
LK = SEQ + EXTRA

GLA_HEADS = 4
GLA_DK = 256
GLA_DV = 512
GLA_QK = GLA_HEADS * GLA_DK
GLA_V = GLA_HEADS * GLA_DV
GLA_LOWRANK = 16
GLA_TAU = 16.0
GLA_CHUNK = 64
GLA_UNROLL = 32

DIFF_HEADS = 8
DIFF_DH = 128
DIFF_DV = 256
DIFF_QK = DIFF_HEADS * 2 * DIFF_DH
DIFF_V = DIFF_HEADS * DIFF_DV
ROPE_THETA = 500000.0
ROPE_DIMS = DIFF_DH // 4
ROPE_HALF = ROPE_DIMS // 2

PEER_HEADS = 8
PEER_NKEYS = 128
PEER_N = PEER_NKEYS * PEER_NKEYS
PEER_DKEY = 128
PEER_TOPK = 16

IN_SIZES = (GLA_QK, GLA_QK, GLA_V, GLA_V, 2 * GLA_LOWRANK, DIFF_QK, DIFF_QK, DIFF_V, D_MODEL, D_MODEL)
IN_OFFSETS = tuple(sum(IN_SIZES[:i]) for i in range(len(IN_SIZES)))

VMEM_LIMIT = 56 * 1024 * 1024
NEG_BIG = -1e30
RANK_MARK = 2.0 ** 100

ROW_BLOCK = 512
COL_BLOCK = 2048
CAST_ROWS = 512
DIFF_TQ = 2048
MERGE_TM = 256
SEL_TB = 512
PEER_TB = 512
PEER_EB = 1024


def _params(*sem):
    return pltpu.CompilerParams(dimension_semantics=sem, vmem_limit_bytes=VMEM_LIMIT)


def _rms(x, g):
    return x * lax.rsqrt(jnp.mean(x * x, axis=-1, keepdims=True) + EPS) * g


def _dot(a, b):
    return jnp.dot(a, b, preferred_element_type=f32)


def _dot_nt(a, b):
    return lax.dot_general(a, b, (((1,), (1,)), ((), ())), preferred_element_type=f32)


def _dot_tn(a, b):
    return lax.dot_general(a, b, (((0,), (0,)), ((), ())), preferred_element_type=f32)


def _norm_decay_kernel(x_ref, e_ref, g_ref, wlr_ref, w2f_ref, w2b_ref, bf_ref, bb_ref, h_ref, of_ref, ob_ref):
    is_seq = pl.program_id(0) < TX // ROW_BLOCK

    @pl.when(is_seq)
    def _():
        h_ref[...] = _rms(x_ref[...], g_ref[...]).astype(h_ref.dtype)

    @pl.when(jnp.logical_not(is_seq))
    def _():
        h_ref[...] = _rms(e_ref[...], g_ref[...]).astype(h_ref.dtype)

    lr = _dot_nt(h_ref[...], wlr_ref[...]).astype(bf16)
    zf = _dot(lr, w2f_ref[...]) + bf_ref[...]
    zb = _dot(lr, w2b_ref[...]) + bb_ref[...]
    of_ref[...] = jax.nn.log_sigmoid(zf) * (1.0 / GLA_TAU)
    ob_ref[...] = jax.nn.log_sigmoid(zb) * (1.0 / GLA_TAU)


def _norm_decay(x, extras, g, wlr, w2f, w2b, b_f, b_b):
    assert BATCH * EXTRA == ROW_BLOCK
    last = TX // ROW_BLOCK - 1
    row = lambda i: (i, 0)
    fixed = lambda i: (0, 0)
    return pl.pallas_call(
        _norm_decay_kernel,
        out_shape=(jax.ShapeDtypeStruct((TP, D_MODEL), bf16),
                   jax.ShapeDtypeStruct((TP, GLA_QK), f32), jax.ShapeDtypeStruct((TP, GLA_QK), f32)),
        grid=(TP // ROW_BLOCK,),
        in_specs=[pl.BlockSpec((ROW_BLOCK, D_MODEL), lambda i: (jnp.minimum(i, last), 0)),
                  pl.BlockSpec((ROW_BLOCK, D_MODEL), fixed),
                  pl.BlockSpec((1, D_MODEL), fixed),
                  pl.BlockSpec((LANE, D_MODEL), fixed),
                  pl.BlockSpec((LANE, GLA_QK), fixed),
                  pl.BlockSpec((LANE, GLA_QK), fixed),
                  pl.BlockSpec((1, GLA_QK), fixed),
                  pl.BlockSpec((1, GLA_QK), fixed)],
        out_specs=(pl.BlockSpec((ROW_BLOCK, D_MODEL), row),
                   pl.BlockSpec((ROW_BLOCK, GLA_QK), row), pl.BlockSpec((ROW_BLOCK, GLA_QK), row)),
        compiler_params=_params("arbitrary"),
        name="mix_norm_decay",
    )(x, extras, g, wlr, w2f, w2b, b_f, b_b)


def _mm_kernel(a_ref, wt_ref, o_ref):
    o_ref[...] = _dot_nt(a_ref[...], wt_ref[...]).astype(o_ref.dtype)


def _mm_cast_kernel(transpose, a_ref, wt_ref, t_ref, o_ref, tb_ref):
    o_ref[...] = _dot_nt(a_ref[...], wt_ref[...]).astype(o_ref.dtype)
    if transpose:
        tb_ref[0] = t_ref[...].T.astype(tb_ref.dtype)
    else:
        tb_ref[...] = t_ref[...].astype(tb_ref.dtype)


def _project(h, wt, r0, n, rows, out_dtype, name, table=None, transpose=False):
    assert r0 % BF16_ROWS == 0 and n % COL_BLOCK == 0 and rows % ROW_BLOCK == 0
    nj, ni = n // COL_BLOCK, rows // ROW_BLOCK
    in_specs = [pl.BlockSpec((ROW_BLOCK, D_MODEL), lambda j, i: (i, 0)),
                pl.BlockSpec((pl.Element(COL_BLOCK), pl.Element(D_MODEL)),
                             lambda j, i: (pl.multiple_of(r0 + j * COL_BLOCK, BF16_ROWS), 0))]
    out_spec = pl.BlockSpec((ROW_BLOCK, COL_BLOCK), lambda j, i: (i, j))
    out_shape = jax.ShapeDtypeStruct((rows, n), out_dtype)
    if table is None:
        return pl.pallas_call(
            _mm_kernel, out_shape=out_shape, grid=(nj, ni), in_specs=in_specs, out_specs=out_spec,
            compiler_params=_params("parallel", "parallel"), name=name,
        )(h, wt)
    n_slabs = table.shape[0] // CAST_ROWS
    assert nj * ni >= n_slabs
    slab = pl.BlockSpec((CAST_ROWS, D_MODEL), lambda j, i: (jnp.minimum(j * ni + i, n_slabs - 1), 0))
    if transpose:
        cast_shape = jax.ShapeDtypeStruct((n_slabs, D_MODEL, CAST_ROWS), bf16)
        cast_spec = pl.BlockSpec((1, D_MODEL, CAST_ROWS), lambda j, i: (jnp.minimum(j * ni + i, n_slabs - 1), 0, 0))
    else:
        cast_shape, cast_spec = jax.ShapeDtypeStruct(table.shape, bf16), slab
    return pl.pallas_call(
        functools.partial(_mm_cast_kernel, transpose),
        out_shape=(out_shape, cast_shape),
        grid=(nj, ni), in_specs=in_specs + [slab], out_specs=(out_spec, cast_spec),
        compiler_params=_params("arbitrary", "arbitrary"), name=name,
    )(h, wt, table)


def _gla_kernel(q_ref, k_ref, v_ref, ke_ref, ve_ref, ff_ref, fb_ref, fe_ref, r_ref, g_ref, o_ref,
                accf_ref, accb_ref, sf_ref, sb_ref):
    C = GLA_CHUNK
    n_chunks = SEQ // C
    row = lax.broadcasted_iota(jnp.int32, (C, C), 0)
    col = lax.broadcasted_iota(jnp.int32, (C, C), 1)
    time = lax.broadcasted_iota(jnp.int32, (C, GLA_DK), 0)

    def decay_sums(lf, prefix):
        cum = lf
        shift = 1
        while shift < C:
            if prefix:
                cum = cum + jnp.where(time >= shift, pltpu.roll(cum, shift, axis=0), 0.0)
            else:
                cum = cum + jnp.where(time < C - shift, pltpu.roll(cum, C - shift, axis=0), 0.0)
            shift *= 2
        total_row = C - 1 if prefix else 0
        return cum, cum[total_row:total_row + 1, :]

    def advance(s_ref, k, v, cum, tot):
        tot_col = jnp.broadcast_to(tot, (8, GLA_DK)).T[:, 0:1]
        k_st = (k * jnp.exp(tot - cum)).astype(bf16)
        s_ref[...] = s_ref[...] * jnp.exp(tot_col) + _dot_tn(k_st, v)

    def chunk(n, f_ref, prefix, keep_mask, acc_ref, s_ref):
        rows = pl.ds(pl.multiple_of(n * C, C), C)
        cum, tot = decay_sums(f_ref[rows, :], prefix)
        q = q_ref[rows, :].astype(f32) * (GLA_DK ** -0.5)
        k = k_ref[rows, :].astype(f32)
        v = v_ref[rows, :]
        q_in = (q * jnp.exp(cum)).astype(bf16)
        k_in = (k * jnp.exp(-cum)).astype(bf16)
        a = jnp.where(keep_mask, _dot_nt(q_in, k_in), 0.0)
        acc_ref[rows, :] = _dot(a.astype(bf16), v) + _dot(q_in, s_ref[...].astype(bf16))
        advance(s_ref, k, v, cum, tot)

    sf_ref[...] = jnp.zeros_like(sf_ref)
    sb_ref[...] = jnp.zeros_like(sb_ref)

    meta_rows = slice(EXTRA - C, EXTRA)
    cum_e, tot_e = decay_sums(fe_ref[meta_rows, :], True)
    advance(sf_ref, ke_ref[meta_rows, :].astype(f32), ve_ref[meta_rows, :], cum_e, tot_e)

    def both(m, carry):
        for u in range(GLA_UNROLL):
            n = m * GLA_UNROLL + u
            chunk(n, ff_ref, True, col <= row, accf_ref, sf_ref)
            chunk(n_chunks - 1 - n, fb_ref, False, col > row, accb_ref, sb_ref)
        return carry

    lax.fori_loop(0, n_chunks // GLA_UNROLL, both, 0)

    def fin(n, carry):
        rows = pl.ds(pl.multiple_of(n * LANE, LANE), LANE)
        o = _rms(accf_ref[rows, :] + accb_ref[rows, :], g_ref[...])
        o_ref[rows, :] = (o * jax.nn.silu(r_ref[rows, :])).astype(o_ref.dtype)
        return carry

    lax.fori_loop(0, SEQ // LANE, fin, 0)


def _gla(proj, gate, logf_f, logf_b, g_norm):
    kq = GLA_QK // GLA_DK
    kv = (2 * GLA_QK) // GLA_DV
    ex = TX // EXTRA
    return pl.pallas_call(
        _gla_kernel,
        out_shape=jax.ShapeDtypeStruct((TX, GLA_V), bf16),
        grid=(BATCH, GLA_HEADS),
        in_specs=[pl.BlockSpec((SEQ, GLA_DK), lambda b, h: (b, h)),
                  pl.BlockSpec((SEQ, GLA_DK), lambda b, h: (b, kq + h)),
                  pl.BlockSpec((SEQ, GLA_DV), lambda b, h: (b, kv + h)),
                  pl.BlockSpec((EXTRA, GLA_DK), lambda b, h: (ex + b, kq + h)),
                  pl.BlockSpec((EXTRA, GLA_DV), lambda b, h: (ex + b, kv + h)),
                  pl.BlockSpec((SEQ, GLA_DK), lambda b, h: (b, h)),
                  pl.BlockSpec((SEQ, GLA_DK), lambda b, h: (b, h)),
                  pl.BlockSpec((EXTRA, GLA_DK), lambda b, h: (ex + b, h)),
                  pl.BlockSpec((SEQ, GLA_DV), lambda b, h: (b, h)),
                  pl.BlockSpec((1, GLA_DV), lambda b, h: (0, 0))],
        out_specs=pl.BlockSpec((SEQ, GLA_DV), lambda b, h: (b, h)),
        scratch_shapes=[pltpu.VMEM((SEQ, GLA_DV), f32), pltpu.VMEM((SEQ, GLA_DV), f32),
                        pltpu.VMEM((GLA_DK, GLA_DV), f32), pltpu.VMEM((GLA_DK, GLA_DV), f32)],
        compiler_params=_params("parallel", "parallel"),
        name="gla_mixer",
    )(proj, proj, proj, proj, proj, logf_f, logf_b, logf_f, gate, g_norm)


def _rope(x, tab):
    return (x * tab[0]
            + pltpu.roll(x, LANE - ROPE_HALF, axis=1) * tab[1]
            + pltpu.roll(x, ROPE_HALF, axis=1) * tab[2])


def _diff_kernel(lam_init, q_ref, k_ref, ke_ref, v_ref, ve_ref, tq_ref, tk_ref, lq1_ref, lk1_ref, lq2_ref, lk2_ref,
                 g_ref, o_ref, kr_ref, vr_ref):
    @pl.when(pl.program_id(2) == 0)
    def _():
        meta = slice(META_PAD, EXTRA)
        zeros = slice(0, META_PAD)
        for m in range(2):
            cols = slice(m * DIFF_DH, (m + 1) * DIFF_DH)
            kr_ref[m, 0:SEQ, :] = _rope(k_ref[:, cols].astype(f32), tk_ref[:, 0:SEQ, :]).astype(bf16)
            kr_ref[m, SEQ:L_REAL, :] = _rope(ke_ref[meta, cols].astype(f32), tk_ref[:, SEQ:L_REAL, :]).astype(bf16)
            kr_ref[m, L_REAL:LK, :] = ke_ref[zeros, cols]
        vr_ref[0:SEQ, :] = v_ref[...]
        vr_ref[SEQ:L_REAL, :] = ve_ref[meta, :]
        vr_ref[L_REAL:LK, :] = ve_ref[zeros, :]

    lam = (jnp.exp(jnp.sum(lq1_ref[...] * lk1_ref[...], axis=1, keepdims=True))
           - jnp.exp(jnp.sum(lq2_ref[...] * lk2_ref[...], axis=1, keepdims=True)) + lam_init)
    tq = tq_ref[...]
    key_ok = lax.broadcasted_iota(jnp.int32, (DIFF_TQ, LK), 1) < L_REAL
    exp2_scale = (DIFF_DH ** -0.5) * math.log2(math.e)
    outs = []
    for m in range(2):
        qx = q_ref[:, m * DIFF_DH:(m + 1) * DIFF_DH].astype(f32)
        qr = _rope(qx, tq).astype(bf16)
        s = jnp.where(key_ok, _dot_nt(qr, kr_ref[m]), NEG_BIG)
        e = jnp.exp2((s - jnp.max(s, axis=1, keepdims=True)) * exp2_scale)
        inv = 1.0 / jnp.sum(e, axis=1, keepdims=True)
        outs.append(_dot(e.astype(bf16), vr_ref[...]) * inv)
    o = outs[0] - lam * outs[1]
    o_ref[...] = (_rms(o, g_ref[...]) * (1.0 - lam_init)).astype(o_ref.dtype)


def _diff_attention(proj, rope_tab, lq1, lk1, lq2, lk2, g_norm, lam_init):
    nq = SEQ // DIFF_TQ
    ck = DIFF_QK // DIFF_DV
    cv = 2 * ck
    ex = TX // EXTRA
    vec = pl.BlockSpec((1, DIFF_DH), lambda b, h, i: (0, 0))
    return pl.pallas_call(
        functools.partial(_diff_kernel, lam_init),
        out_shape=jax.ShapeDtypeStruct((TX, DIFF_V), bf16),
        grid=(BATCH, DIFF_HEADS, nq),
        in_specs=[pl.BlockSpec((DIFF_TQ, DIFF_DV), lambda b, h, i: (b * nq + i, h)),
                  pl.BlockSpec((SEQ, DIFF_DV), lambda b, h, i: (b, ck + h)),
                  pl.BlockSpec((EXTRA, DIFF_DV), lambda b, h, i: (ex + b, ck + h)),
                  pl.BlockSpec((SEQ, DIFF_DV), lambda b, h, i: (b, cv + h)),
                  pl.BlockSpec((EXTRA, DIFF_DV), lambda b, h, i: (ex + b, cv + h)),
                  pl.BlockSpec((3, DIFF_TQ, DIFF_DH), lambda b, h, i: (0, i, 0)),
                  pl.BlockSpec((3, LK, DIFF_DH), lambda b, h, i: (0, 0, 0)),
                  vec, vec, vec, vec,
                  pl.BlockSpec((1, DIFF_DV), lambda b, h, i: (0, 0))],
        out_specs=pl.BlockSpec((DIFF_TQ, DIFF_DV), lambda b, h, i: (b * nq + i, h)),
        scratch_shapes=[pltpu.VMEM((2, LK, DIFF_DH), bf16), pltpu.VMEM((LK, DIFF_DV), bf16)],
        compiler_params=_params("parallel", "parallel", "arbitrary"),
        name="diff_attention",
    )(proj, proj, proj, proj, proj, rope_tab, rope_tab, lq1, lk1, lq2, lk2, g_norm)


def _merge_kernel(og_ref, od_ref, za_ref, zb_ref, hs_ref, wa_ref, wb_ref, wo_ref, g_ref, hs2_ref, hn_ref, hnt_ref):
    y = (jax.nn.sigmoid(za_ref[...]) * _dot(og_ref[...], wa_ref[...])
         + jax.nn.sigmoid(zb_ref[...]) * _dot(od_ref[...], wb_ref[...]))
    hs2 = hs_ref[...] + _dot(y.astype(bf16), wo_ref[...])
    hs2_ref[...] = hs2
    hn = _rms(hs2, g_ref[...])
    hn_ref[...] = hn.astype(hn_ref.dtype)
    hnt_ref[...] = hn.T.astype(hnt_ref.dtype)


def _merge(o_gla, o_diff, zab, hs, wa, wb, wo, g_ffn):
    tm = MERGE_TM
    row = lambda i: (i, 0)
    fixed = lambda i: (0, 0)
    wspec = pl.BlockSpec((D_MODEL, D_MODEL), fixed, pipeline_mode=pl.Buffered(1))
    return pl.pallas_call(
        _merge_kernel,
        out_shape=(jax.ShapeDtypeStruct((TX, D_MODEL), f32), jax.ShapeDtypeStruct((TX, D_MODEL), bf16),
                   jax.ShapeDtypeStruct((D_MODEL, TX), bf16)),
        grid=(TX // tm,),
        in_specs=[pl.BlockSpec((tm, GLA_V), row),
                  pl.BlockSpec((tm, DIFF_V), row),
                  pl.BlockSpec((tm, D_MODEL), lambda i: (i, 0)),
                  pl.BlockSpec((tm, D_MODEL), lambda i: (i, 1)),
                  pl.BlockSpec((tm, D_MODEL), row),
                  wspec, wspec, wspec,
                  pl.BlockSpec((1, D_MODEL), fixed)],
        out_specs=(pl.BlockSpec((tm, D_MODEL), row), pl.BlockSpec((tm, D_MODEL), row),
                   pl.BlockSpec((D_MODEL, tm), lambda i: (0, i))),
        compiler_params=_params("parallel"),
        name="branch_merge",
    )(o_gla, o_diff, zab, zab, hs, wa, wb, wo, g_ffn)


def _top16(s, iota, break_ties):
    vals = []
    if not break_ties:
        for r in range(PEER_TOPK):
            m = jnp.max(s, axis=0, keepdims=True)
            s = jnp.where(s == m, -RANK_MARK * (r + 1), s)
            vals.append(m)
        rank = jnp.where(s <= -RANK_MARK, s * (-1.0 / RANK_MARK) - 1.0, float(PEER_TOPK))
        return vals, rank
    rank = jnp.full(s.shape, float(PEER_TOPK), f32)
    for r in range(PEER_TOPK):
        m = jnp.max(s, axis=0, keepdims=True)
        first = jnp.min(jnp.where(s == m, iota, float(PEER_NKEYS)), axis=0, keepdims=True)
        hit = iota == first
        rank = jnp.where(hit, float(r), rank)
        s = jnp.where(hit, -jnp.inf, s)
        vals.append(m)
    return vals, rank


def _select_head(s0, s1, iota, ids, break_ties):
    K = PEER_TOPK
    tb = s0.shape[1]
    v0, rank0 = _top16(s0, iota, break_ties)
    v1, rank1 = _top16(s1, iota, break_ties)
    sa = jnp.concatenate(v0, axis=0)
    sb = jnp.concatenate(v1, axis=0)
    cand = jnp.concatenate([sa[0:1] + sb]
                           + [sa[a:a + 1] + sb[0:8] for a in range(1, 8)]
                           + [sa[8:16] + sb[0:1]], axis=0)
    top = cand[0:1]
    z = jnp.zeros((1, tb), f32)
    for _ in range(K):
        m = jnp.max(cand, axis=0, keepdims=True)
        hit = cand == m
        if break_ties:
            first = jnp.min(jnp.where(hit, ids, 1e9), axis=0, keepdims=True)
            hit = ids == first
        cand = jnp.where(hit, -jnp.inf, cand)
        z = z + jnp.exp(m - top)
    taken = jnp.where(cand == -jnp.inf, 1.0, 0.0)
    cnt = ([jnp.sum(taken[0:16], axis=0, keepdims=True)]
           + [jnp.sum(taken[8 + 8 * a:16 + 8 * a], axis=0, keepdims=True) for a in range(1, 8)]
           + [taken[72 + a:73 + a] for a in range(8)])
    cnt0 = jnp.zeros((PEER_NKEYS, tb), f32)
    for a in range(K):
        cnt0 = jnp.where(rank0 == float(a), cnt[a], cnt0)
    e0 = jnp.exp(s0 - v0[0]) * (1.0 / z)
    e1 = jnp.exp(s1 - v1[0])
    marked = (jnp.sum((rank0 < float(K)).astype(f32), axis=0, keepdims=True),
              jnp.sum((rank1 < float(K)).astype(f32), axis=0, keepdims=True),
              jnp.sum(taken, axis=0, keepdims=True))
    excess = jnp.max(sum(jnp.abs(n - float(K)) for n in marked))
    return (rank1, cnt0, e0, e1), excess


def _peer_select_kernel(hn_ref, wq_ref, keys_ref, rank1_ref, cnt0_ref, e0_ref, e1_ref, q_sc):
    tb = SEL_TB
    K = PEER_TOPK
    q_sc[...] = _dot(hn_ref[...], wq_ref[...]).astype(bf16)
    iota = lax.broadcasted_iota(jnp.int32, (PEER_NKEYS, tb), 0).astype(f32)
    i16 = lax.broadcasted_iota(jnp.int32, (K, tb), 0).astype(f32)
    i8 = lax.broadcasted_iota(jnp.int32, (8, tb), 0).astype(f32)
    ids = jnp.concatenate([i16] + [a * float(K) + i8 for a in range(1, 8)] + [(i8 + 8.0) * float(K)], axis=0)

    def head(h, carry):
        c0 = pl.multiple_of(h * (2 * PEER_DKEY), 2 * PEER_DKEY)
        s0 = _dot_nt(keys_ref[h, 0], q_sc[:, pl.ds(c0, PEER_DKEY)])
        s1 = _dot_nt(keys_ref[h, 1], q_sc[:, pl.ds(c0 + PEER_DKEY, PEER_DKEY)])

        def store(rank1, cnt0, e0, e1):
            rank1_ref[h] = rank1.astype(bf16)
            cnt0_ref[h] = cnt0
            e0_ref[h] = e0
            e1_ref[h] = e1.astype(bf16)

        tables, excess = _select_head(s0, s1, iota, ids, False)
        store(*tables)

        @pl.when(excess > 0.0)
        def _():
            store(*_select_head(s0, s1, iota, ids, True)[0])

        return carry

    lax.fori_loop(0, PEER_HEADS, head, 0)


def _peer_select(hn, wq, keys):
    tb = SEL_TB
    sel = lambda dt: jax.ShapeDtypeStruct((PEER_HEADS, PEER_NKEYS, TX), dt)
    sel_spec = pl.BlockSpec((PEER_HEADS, PEER_NKEYS, tb), lambda i: (0, 0, i))
    return pl.pallas_call(
        _peer_select_kernel,
        out_shape=(sel(bf16), sel(f32), sel(f32), sel(bf16)),
        grid=(TX // tb,),
        in_specs=[pl.BlockSpec((tb, D_MODEL), lambda i: (i, 0)),
                  pl.BlockSpec((D_MODEL, PEER_HEADS * 2 * PEER_DKEY), lambda i: (0, 0)),
                  pl.BlockSpec((PEER_HEADS, 2, PEER_NKEYS, PEER_DKEY), lambda i: (0, 0, 0, 0))],
        out_specs=(sel_spec,) * 4,
        scratch_shapes=[pltpu.VMEM((tb, PEER_HEADS * 2 * PEER_DKEY), bf16)],
        compiler_params=_params("parallel"),
        name="peer_select",
    )(hn, wq, keys)


def _peer_kernel(hnt_ref, u_ref, v_ref, rank1_ref, cnt0_ref, e0_ref, e1_ref, hs_ref, g_ref, o_ref,
                 acc_ref, a_ref, p_ref):
    j = pl.program_id(1)
    n_blocks = pl.num_programs(1) - 1
    groups = PEER_EB // PEER_NKEYS

    @pl.when(j == 0)
    def _():
        acc_ref[...] = jnp.zeros_like(acc_ref)

    @pl.when(j > 0)
    def _():
        for gi in range(groups):
            i = (j - 1) * groups + gi
            cnts = [jnp.broadcast_to(cnt0_ref[h, pl.ds(i, 1), :], (BF16_ROWS, PEER_TB)).astype(bf16)
                    for h in range(PEER_HEADS)]
            e0s = [jnp.broadcast_to(e0_ref[h, pl.ds(i, 1), :], (BF16_ROWS, PEER_TB)).astype(bf16)
                   for h in range(PEER_HEADS)]
            for r in range(PEER_NKEYS // BF16_ROWS):
                keys = slice(r * BF16_ROWS, (r + 1) * BF16_ROWS)
                rows = slice(gi * PEER_NKEYS + r * BF16_ROWS, gi * PEER_NKEYS + (r + 1) * BF16_ROWS)
                a = a_ref[rows, :]
                act = 0.5 * a * (1.0 + lax.erf(a * (2.0 ** -0.5)))
                w = jnp.zeros((BF16_ROWS, PEER_TB), bf16)
                for h in range(PEER_HEADS):
                    w = w + jnp.where(rank1_ref[h, keys, :] < cnts[h], e1_ref[h, keys, :], jnp.zeros((), bf16)) * e0s[h]
                p_ref[rows, :] = w * act.astype(bf16)
        acc_ref[...] += sum(_dot(v_ref[s], p_ref[s * CAST_ROWS:(s + 1) * CAST_ROWS, :])
                            for s in range(PEER_EB // CAST_ROWS))

    @pl.when(j < n_blocks)
    def _():
        a_ref[...] = _dot(u_ref[...], hnt_ref[...])

    @pl.when(j == n_blocks)
    def _():
        hs3 = hs_ref[...] + acc_ref[...].T
        o_ref[...] = _rms(hs3, g_ref[...])


def _peer(hnt, u, v, rank1, cnt0, e0, e1, hs2, g_final):
    tb, eb = PEER_TB, PEER_EB
    n_blocks = PEER_N // eb
    sel_spec = pl.BlockSpec((PEER_HEADS, PEER_NKEYS, tb), lambda i, j: (0, 0, i))
    return pl.pallas_call(
        _peer_kernel,
        out_shape=jax.ShapeDtypeStruct((TX, D_MODEL), f32),
        grid=(TX // tb, n_blocks + 1),
        in_specs=[pl.BlockSpec((D_MODEL, tb), lambda i, j: (0, i)),
                  pl.BlockSpec((eb, D_MODEL), lambda i, j: (jnp.minimum(j, n_blocks - 1), 0)),
                  pl.BlockSpec((eb // CAST_ROWS, D_MODEL, CAST_ROWS), lambda i, j: (jnp.maximum(j - 1, 0), 0, 0)),
                  sel_spec, sel_spec, sel_spec, sel_spec,
                  pl.BlockSpec((tb, D_MODEL), lambda i, j: (i, 0), pipeline_mode=pl.Buffered(1)),
                  pl.BlockSpec((1, D_MODEL), lambda i, j: (0, 0))],
        out_specs=pl.BlockSpec((tb, D_MODEL), lambda i, j: (i, 0)),
        scratch_shapes=[pltpu.VMEM((D_MODEL, tb), f32), pltpu.VMEM((eb, tb), f32), pltpu.VMEM((eb, tb), bf16)],
        compiler_params=_params("parallel", "arbitrary"),
        name="peer_experts",
    )(hnt, u, v, rank1, cnt0, e0, e1, hs2, g_final)


def _rope_tables():
    pos = jnp.concatenate([jnp.arange(N_META, N_META + SEQ, dtype=jnp.int32),
                           jnp.arange(N_META, dtype=jnp.int32),
                           jnp.zeros((META_PAD,), jnp.int32)])
    inv = 1.0 / (ROPE_THETA ** (jnp.arange(ROPE_HALF, dtype=f32) / ROPE_HALF))
    ang = pos.astype(f32)[:, None] * inv[None, :]
    cos, sin = jnp.cos(ang), jnp.sin(ang)
    rest = DIFF_DH - ROPE_DIMS
    zero, zrest = jnp.zeros_like(sin), jnp.zeros((LK, rest), f32)
    return jnp.stack([
        jnp.concatenate([cos, cos, jnp.ones((LK, rest), f32)], axis=1),
        jnp.concatenate([-sin, zero, zrest], axis=1),
        jnp.concatenate([zero, sin, zrest], axis=1)])


def kernel(x, meta_tokens, g_mix, w_in, gla_w2_fwd, gla_b_fwd, gla_w2_bwd, gla_b_bwd, gla_g_norm, diff_lq1, diff_lk1, diff_lq2, diff_lk2, diff_g_norm, w_branch_gla, w_branch_diff, w_out, g_ffn, peer_w_q, peer_sub_keys, peer_u, peer_v, g_final):
    assert w_in.shape[0] == 1, "single-layer block only"
    l = 0
    lam_init = 0.8 - 0.6 * math.exp(-0.3 * l)
    xs = x.reshape(TX, D_MODEL)
    extra = jnp.concatenate([jnp.zeros((META_PAD, D_MODEL), x.dtype), meta_tokens.astype(x.dtype)], axis=0)
    extras = jnp.tile(extra, (BATCH, 1))
    rope_tab = _rope_tables()

    wt = jnp.swapaxes(w_in[l], 0, 1).astype(bf16)
    o_gq, o_gk, o_gv, o_gr, o_lr, o_dq, o_dk, o_dv, o_za, o_zb = IN_OFFSETS
    w_lr = jnp.pad(wt[o_lr:o_lr + 2 * GLA_LOWRANK], ((0, LANE - 2 * GLA_LOWRANK), (0, 0)))
    w2f = jnp.pad(gla_w2_fwd[l], ((0, LANE - GLA_LOWRANK), (0, 0))).astype(bf16)
    w2b = jnp.pad(gla_w2_bwd[l], ((GLA_LOWRANK, LANE - 2 * GLA_LOWRANK), (0, 0))).astype(bf16)

    h, logf_f, logf_b = _norm_decay(xs, extras, g_mix[l][None], w_lr, w2f, w2b,
                                    gla_b_fwd[l][None], gla_b_bwd[l][None])
    proj_gla, v_bf = _project(h, wt, o_gq, 2 * GLA_QK + GLA_V, TP, bf16, "in_proj_gla", table=peer_v[l],
                              transpose=True)
    proj_diff, u_bf = _project(h, wt, o_dq, 2 * DIFF_QK + DIFF_V, TP, bf16, "in_proj_diff", table=peer_u[l])
    gate_gla = _project(h, wt, o_gr, GLA_V, TX, f32, "in_proj_gla_gate")
    gate_merge = _project(h, wt, o_za, 2 * D_MODEL, TX, f32, "in_proj_merge_gates")
    o_gla = _gla(proj_gla, gate_gla, logf_f, logf_b, gla_g_norm[l][None])
    o_diff = _diff_attention(proj_diff, rope_tab, diff_lq1[l][None], diff_lk1[l][None], diff_lq2[l][None],
                             diff_lk2[l][None], diff_g_norm[l][None], lam_init)
    hs2, hn, hnt = _merge(o_gla, o_diff, gate_merge, xs, w_branch_gla[l].astype(bf16),
                          w_branch_diff[l].astype(bf16), w_out[l].astype(bf16), g_ffn[l][None])
    rank1, cnt0, e0, e1 = _peer_select(hn, peer_w_q[l].astype(bf16), peer_sub_keys[l].astype(bf16))
    out = _peer(hnt, u_bf, v_bf, rank1, cnt0, e0, e1, hs2, g_final[None])
    return out.reshape(BATCH, SEQ, D_MODEL)
```
